```python
import math
import jax, jax.numpy as jnp
from jax import lax
import numpy as np

D_MODEL = 4096
BATCH = 8
SEQ = 2048
DEPTH = 2
DEC_BATCH = 4
DEC_SEQ = 4096
PAST_LEN = 128

MIX_WIDTH = D_MODEL
RET_HEADS = 8
RET_QK_DIM = MIX_WIDTH // (4 * RET_HEADS)
RET_V_DIM = 2 * RET_QK_DIM
RET_QK_WIDTH = RET_HEADS * RET_QK_DIM
RET_WIDTH = RET_HEADS * RET_V_DIM
ATT_HEADS = 16
ATT_HEAD_DIM = MIX_WIDTH // (2 * ATT_HEADS)
ATT_WIDTH = ATT_HEADS * ATT_HEAD_DIM
IN_WIDTH = 2 * RET_QK_WIDTH + 2 * RET_WIDTH + 3 * ATT_WIDTH
D_FF = 4 * D_MODEL
RET_CHUNK = 128
ROPE_BASE = 10000.0
DILATED_BRANCHES = ((128, 1), (512, 4), (2048, 16))
REL_BUCKETS = 32
REL_MAX_DISTANCE = 1024
NORM_EPS = 1e-6
NEG_INF = -1e30

kernel_name = 'hymba_retention_dilated_attn_encoder'


def _rmsnorm(x, gain):
    xf = x.astype(jnp.float32)
    y = xf * lax.rsqrt(jnp.mean(xf * xf, axis=-1, keepdims=True) + NORM_EPS)
    return (y * gain.astype(jnp.float32)).astype(x.dtype)


def _rotary(x):
    S, d = x.shape[1], x.shape[-1]
    half = d // 2
    inv = ROPE_BASE ** (-jnp.arange(half, dtype=jnp.float32) / half)
    ang = jnp.arange(S, dtype=jnp.float32)[:, None] * inv[None, :]
    cos = jnp.cos(ang)[None, :, None, :]
    sin = jnp.sin(ang)[None, :, None, :]
    xf = x.astype(jnp.float32)
    x1, x2 = xf[..., :half], xf[..., half:]
    return jnp.concatenate([x1 * cos - x2 * sin, x1 * sin + x2 * cos], axis=-1).astype(x.dtype)


def _retention_dir(q, k, v, log_decay, include_diag):
    B, H, S, dk = q.shape
    dv = v.shape[-1]
    C = RET_CHUNK
    nc = S // C
    dt = q.dtype
    qc = q.reshape(B, H, nc, C, dk)
    kc = k.reshape(B, H, nc, C, dk)
    vc = v.reshape(B, H, nc, C, dv)
    idx = jnp.arange(C, dtype=jnp.float32)
    diff = idx[:, None] - idx[None, :]
    allowed = diff >= 0 if include_diag else diff > 0
    ld = log_decay[:, None, None]
    inner_mask = jnp.where(allowed[None], jnp.exp(ld * jnp.maximum(diff, 0.0)[None]), 0.0).astype(dt)
    scores = jnp.einsum('bhnid,bhnjd->bhnij', qc, kc) * inner_mask[None, :, None]
    inner = jnp.einsum('bhnij,bhnje->bhnie', scores, vc)
    k_decay = jnp.exp(log_decay[:, None] * (C - 1 - idx)[None, :]).astype(dt)
    q_decay = jnp.exp(log_decay[:, None] * (idx + 1)[None, :]).astype(dt)
    chunk_decay = jnp.exp(log_decay * C).astype(dt)[None, :, None, None]
    delta = jnp.einsum('bhnjd,bhnje->nbhde', kc * k_decay[None, :, None, :, None], vc)

    def step(state, d):
        return state * chunk_decay + d, state

    _, prev = lax.scan(step, jnp.zeros((B, H, dk, dv), dt), delta)
    cross = jnp.einsum('bhnid,nbhde->bhnie', qc * q_decay[None, :, None, :, None], prev)
    return (inner + cross).reshape(B, H, S, dv)


def _retention_group(rq, rk, rv, rg, ret_log_decay, ret_norm_gain):
    B, S, _ = rq.shape
    q = _rotary(rq.reshape(B, S, RET_HEADS, RET_QK_DIM))
    k = _rotary(rk.reshape(B, S, RET_HEADS, RET_QK_DIM)) * (RET_QK_DIM ** -0.5)
    v = rv.reshape(B, S, RET_HEADS, RET_V_DIM)
    q, k, v = (t.transpose(0, 2, 1, 3) for t in (q, k, v))
    log_decay = -jnp.exp(ret_log_decay.astype(jnp.float32))
    y_fwd = _retention_dir(q, k, v, log_decay[0], True)
    y_bwd = _retention_dir(q[:, :, ::-1], k[:, :, ::-1], v[:, :, ::-1], log_decay[1], False)[:, :, ::-1]
    y = (y_fwd + y_bwd).transpose(0, 2, 1, 3)
    y = _rmsnorm(y, ret_norm_gain)
    y = jax.nn.silu(rg.reshape(B, S, RET_HEADS, RET_V_DIM)) * y
    return y.reshape(B, S, RET_WIDTH)


def _rel_bucket(rel):
    nbk = REL_BUCKETS // 2
    max_exact = nbk // 2
    base = jnp.where(rel > 0, nbk, 0)
    n = jnp.abs(rel)
    nf = jnp.maximum(n, 1).astype(jnp.float32)
    large = max_exact + (jnp.log(nf / max_exact) / math.log(REL_MAX_DISTANCE / max_exact)
                         * (nbk - max_exact)).astype(jnp.int32)
    large = jnp.minimum(large, nbk - 1)
    return base + jnp.where(n < max_exact, n, large)


def _key_windows(x, L, blk, nb):
    B, G, _, H, d = x.shape
    Lp = nb * blk
    xp = jnp.pad(x, ((0, 0), (0, 0), (blk, Lp - L + blk), (0, 0), (0, 0)))
    xb = xp.reshape(B, G, nb + 2, blk, H, d)
    return jnp.concatenate([xb[:, :, :-2], xb[:, :, 1:-1], xb[:, :, 2:]], axis=3)


def _dilated_branch(q, k, v, rel_bias_table, window, dilation):
    B, S, H, dh = q.shape
    half = window // (2 * dilation)
    blk = half
    L = S // dilation
    nb = -(-L // blk)
    Lp = nb * blk

    def to_res(t):
        return t.reshape(B, L, dilation, H, dh).transpose(0, 2, 1, 3, 4)

    qr, kr, vr = to_res(q), to_res(k), to_res(v)
    qb = jnp.pad(qr, ((0, 0), (0, 0), (0, Lp - L), (0, 0), (0, 0))).reshape(B, dilation, nb, blk, H, dh)
    kw = _key_windows(kr, L, blk, nb)
    vw = _key_windows(vr, L, blk, nb)
    s = jnp.einsum('bgnqhd,bgnkhd->bgnhqk', qb, kw, preferred_element_type=jnp.float32)
    delta = jnp.arange(3 * blk)[None, :] - blk - jnp.arange(blk)[:, None]
    bias = rel_bias_table[_rel_bucket(delta * dilation)]
    s = s + jnp.transpose(bias, (2, 0, 1)).astype(jnp.float32)
    key_pos = jnp.arange(nb)[:, None] * blk + jnp.arange(3 * blk)[None, :] - blk
    valid = (jnp.abs(delta) <= half)[None] & ((key_pos >= 0) & (key_pos < L))[:, None, :]
    s = jnp.where(valid[None, None, :, None], s, NEG_INF)
    m = jnp.max(s, axis=-1, keepdims=True)
    p = jnp.exp(s - m)
    den = jnp.sum(p, axis=-1)
    o = jnp.einsum('bgnhqk,bgnkhd->bgnqhd', p.astype(v.dtype), vw, preferred_element_type=jnp.float32)
    o = o / jnp.transpose(den, (0, 1, 2, 4, 3))[..., None]
    lse = jnp.transpose(m[..., 0] + jnp.log(den), (0, 1, 2, 4, 3))
    o = o.reshape(B, dilation, Lp, H, dh)[:, :, :L].transpose(0, 2, 1, 3, 4).reshape(B, S, H, dh)
    lse = lse.reshape(B, dilation, Lp, H)[:, :, :L].transpose(0, 2, 1, 3).reshape(B, S, H)
    return o, lse


def _dilated_attention_group(aq, ak, av, rel_bias_table):
    B, S, _ = aq.shape
    q = aq.reshape(B, S, ATT_HEADS, ATT_HEAD_DIM) * (ATT_HEAD_DIM ** -0.5)
    k = ak.reshape(B, S, ATT_HEADS, ATT_HEAD_DIM)
    v = av.reshape(B, S, ATT_HEADS, ATT_HEAD_DIM)
    outs, lses = [], []
    for window, dilation in DILATED_BRANCHES:
        o, l = _dilated_branch(q, k, v, rel_bias_table, window, dilation)
        outs.append(o)
        lses.append(l)
    w = jax.nn.softmax(jnp.stack(lses, axis=0), axis=0)
    o = jnp.sum(w[..., None] * jnp.stack(outs, axis=0), axis=0)
    return o.astype(aq.dtype).reshape(B, S, ATT_WIDTH)


def _mixer(h, rel_bias_table, w_in, ret_log_decay, ret_norm_gain, w_out):
    proj = h @ w_in
    c1 = RET_QK_WIDTH
    c2 = c1 + RET_QK_WIDTH
    c3 = c2 + RET_WIDTH
    c4 = c3 + RET_WIDTH
    c5 = c4 + ATT_WIDTH
    c6 = c5 + ATT_WIDTH
    rq, rk, rv, rg, aq, ak, av = jnp.split(proj, (c1, c2, c3, c4, c5, c6), axis=-1)
    y_ret = _retention_group(rq, rk, rv, rg, ret_log_decay, ret_norm_gain)
    y_att = _dilated_attention_group(aq, ak, av, rel_bias_table)
    return jnp.concatenate([y_ret, y_att], axis=-1) @ w_out


def _trunk(x, rel_bias_table, w_in, ret_log_decay, ret_norm_gain, w_out, w_up, w_down,
           norm_mix_pre, norm_mix_post, norm_mlp_pre, norm_mlp_post):
    for l in range(DEPTH):
        h = _rmsnorm(x, norm_mix_pre[l])
        x = x + _rmsnorm(_mixer(h, rel_bias_table, w_in[l], ret_log_decay[l], ret_norm_gain[l], w_out[l]),
                         norm_mix_post[l])
        h = _rmsnorm(x, norm_mlp_pre[l])
        u = jnp.maximum(h @ w_up[l], 0)
        x = x + _rmsnorm((u * u) @ w_down[l], norm_mlp_post[l])
    return x


def setup_inputs(seed: int = 0) -> dict:
    key = jax.random.key(seed)
    ks = jax.random.split(key, 13)
    f32 = jnp.float32
    base_decay = np.log(-np.log(1.0 - 2.0 ** (-5.0 - np.arange(RET_HEADS)))).astype(np.float32)
    return {
        'x_prompt': jax.random.normal(ks[0], (BATCH, SEQ, D_MODEL), f32),
        'x_sample': jax.random.normal(ks[1], (DEC_BATCH, DEC_SEQ, D_MODEL), f32),
        'rel_bias_table': 0.5 * jax.random.normal(ks[2], (REL_BUCKETS, ATT_HEADS), f32),
        'w_in': jax.random.normal(ks[3], (DEPTH, D_MODEL, IN_WIDTH), f32) * D_MODEL ** -0.5,
        'ret_log_decay': jnp.asarray(base_decay)[None, None, :]
                         + 0.1 * jax.random.normal(ks[4], (DEPTH, 2, RET_HEADS), f32),
        'ret_norm_gain': 1.0 + 0.02 * jax.random.normal(ks[5], (DEPTH, RET_HEADS, RET_V_DIM), f32),
        'w_out': jax.random.normal(ks[6], (DEPTH, MIX_WIDTH, D_MODEL), f32) * MIX_WIDTH ** -0.5,
        'w_up': jax.random.normal(ks[7], (DEPTH, D_MODEL, D_FF), f32) * D_MODEL ** -0.5,
        'w_down': jax.random.normal(ks[8], (DEPTH, D_FF, D_MODEL), f32) * D_FF ** -0.5,
        'norm_mix_pre': 1.0 + 0.02 * jax.random.normal(ks[9], (DEPTH, D_MODEL), f32),
        'norm_mix_post': 1.0 + 0.02 * jax.random.normal(ks[10], (DEPTH, D_MODEL), f32),
        'norm_mlp_pre': 1.0 + 0.02 * jax.random.normal(ks[11], (DEPTH, D_MODEL), f32),
        'norm_mlp_post': 1.0 + 0.02 * jax.random.normal(ks[12], (DEPTH, D_MODEL), f32),
    }


def reference(x_prompt, x_sample, rel_bias_table, w_in, ret_log_decay, ret_norm_gain, w_out, w_up, w_down,
              norm_mix_pre, norm_mix_post, norm_mlp_pre, norm_mlp_post):
    y_prompt = _trunk(x_prompt, rel_bias_table, w_in, ret_log_decay, ret_norm_gain, w_out, w_up, w_down,
                      norm_mix_pre, norm_mix_post, norm_mlp_pre, norm_mlp_post)
    y_sample = _trunk(x_sample, rel_bias_table, w_in, ret_log_decay, ret_norm_gain, w_out, w_up, w_down,
                      norm_mix_pre, norm_mix_post, norm_mlp_pre, norm_mlp_post)
    return (y_prompt, y_sample)
```

```python
import functools
import math

import numpy as np
import jax
import jax.numpy as jnp
from jax import lax
from jax.experimental import pallas as pl
from jax.experimental.pallas import tpu as pltpu

BF16 = jnp.bfloat16
F32 = jnp.float32

D_MODEL = 4096
RET_HEADS = 8
RET_QK_DIM = 128
RET_V_DIM = 256
RET_QK_WIDTH = RET_HEADS * RET_QK_DIM
RET_WIDTH = RET_HEADS * RET_V_DIM
ATT_HEADS = 16
ATT_HEAD_DIM = 128
ATT_WIDTH = ATT_HEADS * ATT_HEAD_DIM
IN_WIDTH = 2 * RET_QK_WIDTH + 2 * RET_WIDTH + 3 * ATT_WIDTH
RET_CHUNK = 128
ROPE_BASE = 10000.0
DILATIONS = (1, 4, 16)
ATT_HALF = 64
REL_BUCKETS = 32
REL_MAX_DISTANCE = 1024
NORM_EPS = 1e-6
NEG_INF = -1e30

ATT_QB = 128
ATT_KW = 256
BIAS_OFFSETS = (0, -ATT_HALF, -2 * ATT_HALF)

V7X_VMEM_LIMIT_BYTES = 56 * 1024 * 1024


def _params(sem, vmem=V7X_VMEM_LIMIT_BYTES):
    return pltpu.CompilerParams(dimension_semantics=sem, vmem_limit_bytes=vmem)


NORM_ROWS = 16


def _norm_matmul_kernel(x_ref, g_ref, w_ref, o_ref, h_ref, *, relu2):
    tm = x_ref.shape[0]

    @pl.when(pl.program_id(1) == 0)
    def _():
        gain = g_ref[...]

        def body(c, carry):
            rows = pl.ds(pl.multiple_of(c * NORM_ROWS, NORM_ROWS), NORM_ROWS)
            x = x_ref[rows, :]
            ms = jnp.mean(x * x, axis=-1, keepdims=True)
            h_ref[rows, :] = ((x * lax.rsqrt(ms + NORM_EPS)) * gain).astype(BF16)
            return carry

        lax.fori_loop(0, tm // NORM_ROWS, body, 0)

    y = jnp.dot(h_ref[...], w_ref[...], preferred_element_type=F32)
    if relu2:
        y = jnp.maximum(y, 0.0)
        y = y * y
    o_ref[...] = y.astype(o_ref.dtype)


def _norm_matmul(x, gain, w, *, relu2, tm=512, tn=512, name):
    t, k = x.shape
    n = w.shape[1]
    return pl.pallas_call(
        functools.partial(_norm_matmul_kernel, relu2=relu2),
        grid=(t // tm, n // tn),
        in_specs=[
            pl.BlockSpec((tm, k), lambda i, j: (i, 0)),
            pl.BlockSpec((1, k), lambda i, j: (0, 0)),
            pl.BlockSpec((k, tn), lambda i, j: (0, j)),
        ],
        out_specs=pl.BlockSpec((tm, tn), lambda i, j: (i, j)),
        out_shape=jax.ShapeDtypeStruct((t, n), BF16),
        scratch_shapes=[pltpu.VMEM((tm, k), BF16)],
        compiler_params=_params(("parallel", "arbitrary")),
        name=name,
    )(x, gain.reshape(1, k), w)


def _matmul_norm_res_kernel(a_ref, w_ref, x_ref, g_ref, o_ref, acc_ref):
    kk = pl.program_id(1)
    nk = pl.num_programs(1)
    tm = acc_ref.shape[0]
    part = jnp.dot(a_ref[...], w_ref[...], preferred_element_type=F32)

    @pl.when(kk == 0)
    def _():
        acc_ref[...] = part

    @pl.when(kk > 0)
    def _():
        acc_ref[...] += part

    @pl.when(kk == nk - 1)
    def _():
        gain = g_ref[...]

        def body(c, carry):
            rows = pl.ds(pl.multiple_of(c * NORM_ROWS, NORM_ROWS), NORM_ROWS)
            y = acc_ref[rows, :]
            ms = jnp.mean(y * y, axis=-1, keepdims=True)
            o_ref[rows, :] = x_ref[rows, :] + (y * lax.rsqrt(ms + NORM_EPS)) * gain
            return carry

        lax.fori_loop(0, tm // NORM_ROWS, body, 0)


def _matmul_norm_res(a, w, x, gain, *, tm=512, tk=512, name):
    t, k = a.shape
    n = w.shape[1]
    return pl.pallas_call(
        _matmul_norm_res_kernel,
        grid=(t // tm, k // tk),
        in_specs=[
            pl.BlockSpec((tm, tk), lambda i, kk: (i, kk)),
            pl.BlockSpec((tk, n), lambda i, kk: (kk, 0)),
            pl.BlockSpec((tm, n), lambda i, kk: (i, 0), pipeline_mode=pl.Buffered(1)),
            pl.BlockSpec((1, n), lambda i, kk: (0, 0)),
        ],
        out_specs=pl.BlockSpec((tm, n), lambda i, kk: (i, 0)),
        out_shape=jax.ShapeDtypeStruct((t, n), F32),
        scratch_shapes=[pltpu.VMEM((tm, n), F32)],
        compiler_params=_params(("parallel", "arbitrary")),
        name=name,
    )(a, w, x, gain.reshape(1, n))


def _retention_kernel(ld_ref, q_ref, k_ref, v_ref, g_ref, cos_ref, sin_ref, gain_ref, o_ref,
                      qr_ref, kr_ref, yb_ref, st_ref):
    seq = q_ref.shape[1]
    c = RET_CHUNK
    nc = seq // c
    h = pl.program_id(1)
    ld_f = ld_ref[0, h]
    ld_b = ld_ref[1, h]

    ri = lax.broadcasted_iota(jnp.int32, (c, c), 0)
    ci = lax.broadcasted_iota(jnp.int32, (c, c), 1)
    diff = (ri - ci).astype(F32)
    mask = jnp.where(diff >= 0, jnp.exp(ld_f * jnp.maximum(diff, 0.0)), jnp.exp(ld_b * jnp.maximum(-diff, 0.0)))
    pos = lax.broadcasted_iota(jnp.int32, (c, RET_QK_DIM), 0).astype(F32)
    qdec_f = jnp.exp(ld_f * (pos + 1.0))
    kdec_f = jnp.exp(ld_f * (c - 1.0 - pos))
    qdec_b = jnp.exp(ld_b * (c - pos))
    kdec_b = jnp.exp(ld_b * pos)
    cd_f = jnp.exp(jnp.full((1, RET_V_DIM), ld_f * c, F32))
    cd_b = jnp.exp(jnp.full((1, RET_V_DIM), ld_b * c, F32))
    gain = gain_ref[0]
    k_scale = RET_QK_DIM ** -0.5

    def chunk_rows(n):
        return pl.ds(pl.multiple_of(n * c, c), c)

    def rotate(n, carry):
        rows = chunk_rows(n)
        cos = cos_ref[rows, :]
        sin = sin_ref[rows, :]
        xq = q_ref[0, rows, :].astype(F32)
        xk = k_ref[0, rows, :].astype(F32)
        qr_ref[rows, :] = xq * cos + pltpu.roll(xq, RET_QK_DIM // 2, 1) * sin
        kr_ref[rows, :] = (xk * cos + pltpu.roll(xk, RET_QK_DIM // 2, 1) * sin) * k_scale
        return carry

    lax.fori_loop(0, nc, rotate, 0)

    def kv_outer(kd, v):
        return lax.dot_general(kd, v, (((0,), (0,)), ((), ())), preferred_element_type=F32)

    st_ref[...] = jnp.zeros_like(st_ref)

    def backward(t, carry):
        rows = chunk_rows(nc - 1 - t)
        st = st_ref[...]
        qd = (qr_ref[rows, :] * qdec_b).astype(BF16)
        yb_ref[rows, :] = jnp.dot(qd, st.astype(BF16), preferred_element_type=F32)
        kd = (kr_ref[rows, :] * kdec_b).astype(BF16)
        st_ref[...] = st * cd_b + kv_outer(kd, v_ref[0, rows, :])
        return carry

    lax.fori_loop(0, nc, backward, 0)

    st_ref[...] = jnp.zeros_like(st_ref)

    def forward(n, carry):
        rows = chunk_rows(n)
        st = st_ref[...]
        q = qr_ref[rows, :]
        k = kr_ref[rows, :]
        v = v_ref[0, rows, :]
        s = lax.dot_general(q.astype(BF16), k.astype(BF16), (((1,), (1,)), ((), ())),
                            preferred_element_type=F32) * mask
        inner = jnp.dot(s.astype(BF16), v, preferred_element_type=F32)
        cross = jnp.dot((q * qdec_f).astype(BF16), st.astype(BF16), preferred_element_type=F32)
        y = yb_ref[rows, :] + inner + cross
        st_ref[...] = st * cd_f + kv_outer((k * kdec_f).astype(BF16), v)
        ms = jnp.mean(y * y, axis=-1, keepdims=True)
        yn = (y * lax.rsqrt(ms + NORM_EPS)) * gain
        gate = g_ref[0, rows, :].astype(F32)
        o_ref[0, rows, :] = ((gate * jax.nn.sigmoid(gate)) * yn).astype(o_ref.dtype)
        return carry

    lax.fori_loop(0, nc, forward, 0)


def _retention(proj, log_decay, gain, cos2, sin2):
    b, seq, _ = proj.shape
    qk_blocks = RET_QK_WIDTH // RET_QK_DIM
    v_block0 = 2 * RET_QK_WIDTH // RET_V_DIM
    g_block0 = v_block0 + RET_HEADS
    return pl.pallas_call(
        _retention_kernel,
        grid=(b, RET_HEADS),
        in_specs=[
            pl.BlockSpec(memory_space=pltpu.SMEM),
            pl.BlockSpec((1, seq, RET_QK_DIM), lambda i, h: (i, 0, h)),
            pl.BlockSpec((1, seq, RET_QK_DIM), lambda i, h: (i, 0, qk_blocks + h)),
            pl.BlockSpec((1, seq, RET_V_DIM), lambda i, h: (i, 0, v_block0 + h)),
            pl.BlockSpec((1, seq, RET_V_DIM), lambda i, h: (i, 0, g_block0 + h)),
            pl.BlockSpec((seq, RET_QK_DIM), lambda i, h: (0, 0)),
            pl.BlockSpec((seq, RET_QK_DIM), lambda i, h: (0, 0)),
            pl.BlockSpec((1, 1, RET_V_DIM), lambda i, h: (h, 0, 0)),
        ],
        out_specs=pl.BlockSpec((1, seq, RET_V_DIM), lambda i, h: (i, 0, h)),
        out_shape=jax.ShapeDtypeStruct((b, seq, D_MODEL), BF16),
        scratch_shapes=[
            pltpu.VMEM((seq, RET_QK_DIM), F32),
            pltpu.VMEM((seq, RET_QK_DIM), F32),
            pltpu.VMEM((seq, RET_V_DIM), F32),
            pltpu.VMEM((RET_QK_DIM, RET_V_DIM), F32),
        ],
        compiler_params=_params(("parallel", "arbitrary")),
        name="retention",
    )(log_decay, proj, proj, proj, proj, cos2, sin2, gain.reshape(RET_HEADS, 1, RET_V_DIM))


COPY_ROWS = 128


def _attention_kernel(q_ref, k_ref, v_ref, t_ref, mix_ref, o_ref,
                      stage_ref, q1_ref, q4_ref, k4_ref, v4_ref, q16_ref, k16_ref, v16_ref,
                      ob_ref, lb_ref, to_ref, tl_ref):
    del mix_ref
    seq = q_ref.shape[1]
    q_scale = ATT_HEAD_DIM ** -0.5

    def deinterleave(dst_ref, d):
        length = seq // d
        ch = min(COPY_ROWS, length)
        for r in range(d):
            for cc in range(length // ch):
                src = stage_ref[pl.ds(r + cc * ch * d, ch, stride=d), :]
                dst_ref[pl.ds(r * length + cc * ch, ch), :] = src.astype(BF16)

    def stage(src_ref, scale):
        def body(cc, carry):
            rows = pl.ds(pl.multiple_of(cc * COPY_ROWS, COPY_ROWS), COPY_ROWS)
            x = src_ref[0, rows, :].astype(F32)
            stage_ref[rows, :] = x * scale if scale is not None else x
            return carry

        lax.fori_loop(0, seq // COPY_ROWS, body, 0)

    stage(q_ref, q_scale)

    def q1_body(cc, carry):
        rows = pl.ds(pl.multiple_of(cc * COPY_ROWS, COPY_ROWS), COPY_ROWS)
        q1_ref[rows, :] = stage_ref[rows, :].astype(BF16)
        return carry

    lax.fori_loop(0, seq // COPY_ROWS, q1_body, 0)
    deinterleave(q4_ref, 4)
    deinterleave(q16_ref, 16)
    stage(k_ref, None)
    deinterleave(k4_ref, 4)
    deinterleave(k16_ref, 16)
    stage(v_ref, None)
    deinterleave(v4_ref, 4)
    deinterleave(v16_ref, 16)

    def run_branch(bi, d, load_q, load_k, load_v, out_o_ref, out_l_ref):
        length = seq // d
        kw = min(length, ATT_KW)
        nblk = length // ATT_QB

        def body(t, carry):
            r = t // nblk
            nb = t - r * nblk
            i0 = nb * ATT_QB
            ws = jnp.clip(i0 - ATT_HALF, 0, length - kw)
            var = jnp.where(nb == 0, 0, jnp.where(nb == nblk - 1, 2, 1))
            base = r * length
            qrows = pl.ds(pl.multiple_of(base + i0, ATT_QB), ATT_QB)
            krows = pl.ds(pl.multiple_of(base + ws, ATT_HALF), kw)
            q = load_q(qrows)
            k = load_k(krows)
            v = load_v(krows)
            s = lax.dot_general(q, k, (((1,), (1,)), ((), ())), preferred_element_type=F32)
            s = s + t_ref[0, bi, var, :, :kw]
            m = jnp.max(s, axis=-1, keepdims=True)
            p = jnp.exp(s - m)
            den = jnp.sum(p, axis=-1, keepdims=True)
            acc = jnp.dot(p.astype(BF16), v, preferred_element_type=F32)
            out_o_ref[qrows, :] = acc / den
            out_l_ref[qrows, :] = jnp.broadcast_to(m + jnp.log(den), (ATT_QB, ATT_HEAD_DIM))
            return carry

        lax.fori_loop(0, d * nblk, body, 0)

    def reinterleave(bi, d):
        length = seq // d
        ch = min(COPY_ROWS, length)
        for r in range(d):
            for cc in range(length // ch):
                src = pl.ds(r * length + cc * ch, ch)
                dst = pl.ds(r + cc * ch * d, ch, stride=d)
                ob_ref[bi, dst, :] = to_ref[src, :]
                lb_ref[bi, dst, :] = tl_ref[src, :]

    run_branch(0, 1, lambda rows: q1_ref[rows, :], lambda rows: k_ref[0, rows, :], lambda rows: v_ref[0, rows, :],
               ob_ref.at[0], lb_ref.at[0])
    run_branch(1, 4, lambda rows: q4_ref[rows, :], lambda rows: k4_ref[rows, :], lambda rows: v4_ref[rows, :],
               to_ref, tl_ref)
    reinterleave(1, 4)
    run_branch(2, 16, lambda rows: q16_ref[rows, :], lambda rows: k16_ref[rows, :], lambda rows: v16_ref[rows, :],
               to_ref, tl_ref)
    reinterleave(2, 16)

    def combine(cc, carry):
        rows = pl.ds(pl.multiple_of(cc * COPY_ROWS, COPY_ROWS), COPY_ROWS)
        l0 = lb_ref[0, rows, :]
        l1 = lb_ref[1, rows, :]
        l2 = lb_ref[2, rows, :]
        top = jnp.maximum(jnp.maximum(l0, l1), l2)
        e0 = jnp.exp(l0 - top)
        e1 = jnp.exp(l1 - top)
        e2 = jnp.exp(l2 - top)
        tot = e0 + e1 + e2
        o = (e0 / tot) * ob_ref[0, rows, :] + (e1 / tot) * ob_ref[1, rows, :] + (e2 / tot) * ob_ref[2, rows, :]
        o_ref[0, rows, :] = o.astype(o_ref.dtype)
        return carry

    lax.fori_loop(0, seq // COPY_ROWS, combine, 0)


def _attention(proj, bias_tiles, mix):
    b, seq, _ = proj.shape
    q_block0 = (2 * RET_QK_WIDTH + 2 * RET_WIDTH) // ATT_HEAD_DIM
    k_block0 = q_block0 + ATT_HEADS
    v_block0 = k_block0 + ATT_HEADS
    out_block0 = RET_WIDTH // ATT_HEAD_DIM
    nb, nv = len(DILATIONS), len(BIAS_OFFSETS)
    head_rows = pltpu.VMEM((seq, ATT_HEAD_DIM), BF16)
    head_rows_f32 = pltpu.VMEM((seq, ATT_HEAD_DIM), F32)
    return pl.pallas_call(
        _attention_kernel,
        grid=(b, ATT_HEADS),
        in_specs=[
            pl.BlockSpec((1, seq, ATT_HEAD_DIM), lambda i, h: (i, 0, q_block0 + h)),
            pl.BlockSpec((1, seq, ATT_HEAD_DIM), lambda i, h: (i, 0, k_block0 + h)),
            pl.BlockSpec((1, seq, ATT_HEAD_DIM), lambda i, h: (i, 0, v_block0 + h)),
            pl.BlockSpec((1, nb, nv, ATT_QB, ATT_KW), lambda i, h: (h, 0, 0, 0, 0)),
            pl.BlockSpec(memory_space=pl.ANY),
        ],
        out_specs=pl.BlockSpec((1, seq, ATT_HEAD_DIM), lambda i, h: (i, 0, out_block0 + h)),
        out_shape=jax.ShapeDtypeStruct((b, seq, D_MODEL), BF16),
        scratch_shapes=[
            head_rows_f32,
            head_rows,
            head_rows, head_rows, head_rows,
            head_rows, head_rows, head_rows,
            pltpu.VMEM((nb, seq, ATT_HEAD_DIM), F32),
            pltpu.VMEM((nb, seq, ATT_HEAD_DIM), F32),
            head_rows_f32, head_rows_f32,
        ],
        input_output_aliases={4: 0},
        compiler_params=_params(("parallel", "arbitrary")),
        name="dilated_attention",
    )(proj, proj, proj, bias_tiles, mix)


def _rotary_tables(seq):
    half = RET_QK_DIM // 2
    inv = ROPE_BASE ** (-jnp.arange(half, dtype=F32) / half)
    ang = jnp.arange(seq, dtype=F32)[:, None] * inv[None, :]
    cos, sin = jnp.cos(ang), jnp.sin(ang)
    return jnp.concatenate([cos, cos], axis=-1), jnp.concatenate([-sin, sin], axis=-1)


def _bucket_of(rel):
    nbk = REL_BUCKETS // 2
    max_exact = nbk // 2
    n = np.abs(rel)
    nf = np.maximum(n, 1).astype(np.float32)
    large = max_exact + (np.log(nf / max_exact) / math.log(REL_MAX_DISTANCE / max_exact)
                         * (nbk - max_exact)).astype(np.int32)
    large = np.minimum(large, nbk - 1)
    return np.where(rel > 0, nbk, 0) + np.where(n < max_exact, n, large)


def _bias_tiles(rel_bias_table):
    row = np.arange(ATT_QB)[:, None]
    col = np.arange(ATT_KW)[None, :]
    buckets, inside = [], []
    for d in DILATIONS:
        for off in BIAS_OFFSETS:
            delta = col - row + off
            buckets.append(_bucket_of(delta * d))
            inside.append(np.abs(delta) <= ATT_HALF)
    buckets = np.stack(buckets).reshape(len(DILATIONS), len(BIAS_OFFSETS), ATT_QB, ATT_KW)
    inside = np.stack(inside).reshape(buckets.shape)
    bias = jnp.transpose(rel_bias_table.astype(F32))[:, buckets]
    return jnp.where(inside[None], bias, NEG_INF)


def _trunk(x, bias_tiles, weights):
    b, seq, dm = x.shape
    cos2, sin2 = _rotary_tables(seq)
    xt = x.reshape(b * seq, dm)
    for layer in weights:
        proj = _norm_matmul(xt, layer["norm_mix_pre"], layer["w_in"], relu2=False, name="in_proj")
        proj = proj.reshape(b, seq, IN_WIDTH)
        mix = _retention(proj, layer["log_decay"], layer["ret_norm_gain"], cos2, sin2)
        mix = _attention(proj, bias_tiles, mix)
        xt = _matmul_norm_res(mix.reshape(b * seq, dm), layer["w_out"], xt, layer["norm_mix_post"], name="out_proj")
        u = _norm_matmul(xt, layer["norm_mlp_pre"], layer["w_up"], relu2=True, name="up_proj")
        xt = _matmul_norm_res(u, layer["w_down"], xt, layer["norm_mlp_post"], name="down_proj")
    return xt.reshape(b, seq, dm)


def kernel(x_prompt, x_sample, rel_bias_table, w_in, ret_log_decay, ret_norm_gain, w_out, w_up, w_down,
           norm_mix_pre, norm_mix_post, norm_mlp_pre, norm_mlp_post):
    depth = w_in.shape[0]
    weights = []
    for l in range(depth):
        weights.append({
            "w_in": w_in[l].astype(BF16),
            "w_out": w_out[l].astype(BF16),
            "w_up": w_up[l].astype(BF16),
            "w_down": w_down[l].astype(BF16),
            "log_decay": -jnp.exp(ret_log_decay[l].astype(F32)),
            "ret_norm_gain": ret_norm_gain[l].astype(F32),
            "norm_mix_pre": norm_mix_pre[l].astype(F32),
            "norm_mix_post": norm_mix_post[l].astype(F32),
            "norm_mlp_pre": norm_mlp_pre[l].astype(F32),
            "norm_mlp_post": norm_mlp_post[l].astype(F32),
        })
    bias_tiles = _bias_tiles(rel_bias_table)
    y_prompt = _trunk(x_prompt, bias_tiles, weights)
    y_sample = _trunk(x_sample, bias_tiles, weights)
    return (y_prompt, y_sample)
```

```python
import functools
import math

import numpy as np
import jax
import jax.numpy as jnp
from jax import lax
from jax.experimental import pallas as pl
from jax.experimental.pallas import tpu as pltpu

BF16 = jnp.bfloat16
F32 = jnp.float32

D_MODEL = 4096
RET_HEADS = 8
RET_QK_DIM = 128
RET_V_DIM = 256
RET_QK_WIDTH = RET_HEADS * RET_QK_DIM
RET_WIDTH = RET_HEADS * RET_V_DIM
ATT_HEADS = 16
ATT_HEAD_DIM = 128
ATT_WIDTH = ATT_HEADS * ATT_HEAD_DIM
IN_WIDTH = 2 * RET_QK_WIDTH + 2 * RET_WIDTH + 3 * ATT_WIDTH
RET_CHUNK = 128
ROPE_BASE = 10000.0
DILATIONS = (1, 4, 16)
ATT_HALF = 64
REL_BUCKETS = 32
REL_MAX_DISTANCE = 1024
NORM_EPS = 1e-6
NEG_INF = -1e30

ATT_QB = 128
ATT_KW = 256
BIAS_OFFSETS = (0, -ATT_HALF, -2 * ATT_HALF)
ATT_TILE_GROUP = 8

V7X_VMEM_LIMIT_BYTES = 56 * 1024 * 1024
LANES = 128


def _params(sem, vmem=V7X_VMEM_LIMIT_BYTES):
    return pltpu.CompilerParams(dimension_semantics=sem, vmem_limit_bytes=vmem)


NORM_ROWS = 16
NORM_UNROLL = 4


def _norm_matmul_kernel(x_ref, g_ref, w_ref, o_ref, h_ref, *, relu2):
    tm = x_ref.shape[0]

    @pl.when(pl.program_id(1) == 0)
    def _():
        gain = g_ref[...]

        def body(c, carry):
            rows = pl.ds(pl.multiple_of(c * NORM_ROWS, NORM_ROWS), NORM_ROWS)
            x = x_ref[rows, :]
            ms = jnp.mean(x * x, axis=-1, keepdims=True)
            h_ref[rows, :] = ((x * lax.rsqrt(ms + NORM_EPS)) * gain).astype(BF16)
            return carry

        lax.fori_loop(0, tm // NORM_ROWS, body, 0, unroll=NORM_UNROLL)

    y = jnp.dot(h_ref[...], w_ref[...], preferred_element_type=F32)
    if relu2:
        y = jnp.maximum(y, 0.0)
        y = y * y
    o_ref[...] = y.astype(o_ref.dtype)


def _norm_matmul(x, gain, w, *, relu2, tm=512, tn=1024, name):
    t, k = x.shape
    n = w.shape[1]
    return pl.pallas_call(
        functools.partial(_norm_matmul_kernel, relu2=relu2),
        grid=(t // tm, n // tn),
        in_specs=[
            pl.BlockSpec((tm, k), lambda i, j: (i, 0)),
            pl.BlockSpec((1, k), lambda i, j: (0, 0)),
            pl.BlockSpec((k, tn), lambda i, j: (0, j)),
        ],
        out_specs=pl.BlockSpec((tm, tn), lambda i, j: (i, j)),
        out_shape=jax.ShapeDtypeStruct((t, n), BF16),
        scratch_shapes=[pltpu.VMEM((tm, k), BF16)],
        compiler_params=_params(("parallel", "arbitrary")),
        name=name,
    )(x, gain.reshape(1, k), w)


def _matmul_norm_res_kernel(a_ref, w_ref, x_ref, g_ref, o_ref, rs_ref):
    kk = pl.program_id(1)
    nk = pl.num_programs(1)
    tm, n = o_ref.shape

    @pl.when(kk == 0)
    def _():
        o_ref[...] = jnp.dot(a_ref[...], w_ref[...], preferred_element_type=F32)

    @pl.when(kk > 0)
    def _():
        o_ref[...] += jnp.dot(a_ref[...], w_ref[...], preferred_element_type=F32)

    @pl.when(kk == nk - 1)
    def _():
        def scale(c, carry):
            rows = pl.ds(pl.multiple_of(c * NORM_ROWS, NORM_ROWS), NORM_ROWS)
            y = o_ref[rows, :]
            ms = jnp.mean(y * y, axis=-1, keepdims=True)
            rs_ref[rows, :] = jnp.broadcast_to(lax.rsqrt(ms + NORM_EPS), (NORM_ROWS, LANES))
            return carry

        lax.fori_loop(0, tm // NORM_ROWS, scale, 0, unroll=NORM_UNROLL)

        def apply(c, carry):
            rows = pl.ds(pl.multiple_of(c * NORM_ROWS, NORM_ROWS), NORM_ROWS)
            rs = rs_ref[rows, :]
            for lo in range(0, n, LANES):
                cols = pl.ds(lo, LANES)
                o_ref[rows, cols] = x_ref[rows, cols] + (o_ref[rows, cols] * rs) * g_ref[:, cols]
            return carry

        lax.fori_loop(0, tm // NORM_ROWS, apply, 0, unroll=NORM_UNROLL)


def _matmul_norm_res(a, w, x, gain, *, tm=512, tk=1024, name):
    t, k = a.shape
    n = w.shape[1]
    return pl.pallas_call(
        _matmul_norm_res_kernel,
        grid=(t // tm, k // tk),
        in_specs=[
            pl.BlockSpec((tm, tk), lambda i, kk: (i, kk)),
            pl.BlockSpec((tk, n), lambda i, kk: (kk, 0)),
            pl.BlockSpec((tm, n), lambda i, kk: (i, 0), pipeline_mode=pl.Buffered(1)),
            pl.BlockSpec((1, n), lambda i, kk: (0, 0)),
        ],
        out_specs=pl.BlockSpec((tm, n), lambda i, kk: (i, 0)),
        out_shape=jax.ShapeDtypeStruct((t, n), F32),
        scratch_shapes=[pltpu.VMEM((tm, LANES), F32)],
        compiler_params=_params(("parallel", "arbitrary")),
        name=name,
    )(a, w, x, gain.reshape(1, n))


def _retention_kernel(ld_ref, q_ref, k_ref, v_ref, g_ref, cos_ref, sin_ref, gain_ref, o_ref,
                      qr_ref, kr_ref, yb_ref, st_ref):
    seq = q_ref.shape[1]
    c = RET_CHUNK
    nc = seq // c
    h = pl.program_id(1)
    ld_f = ld_ref[0, h]
    ld_b = ld_ref[1, h]

    ri = lax.broadcasted_iota(jnp.int32, (c, c), 0)
    ci = lax.broadcasted_iota(jnp.int32, (c, c), 1)
    diff = (ri - ci).astype(F32)
    mask = jnp.where(diff >= 0, jnp.exp(ld_f * jnp.maximum(diff, 0.0)), jnp.exp(ld_b * jnp.maximum(-diff, 0.0)))
    pos = lax.broadcasted_iota(jnp.int32, (c, RET_QK_DIM), 0).astype(F32)
    qdec_f = jnp.exp(ld_f * (pos + 1.0))
    kdec_f = jnp.exp(ld_f * (c - 1.0 - pos))
    qdec_b = jnp.exp(ld_b * (c - pos))
    kdec_b = jnp.exp(ld_b * pos)
    cd_f = jnp.exp(jnp.full((1, RET_V_DIM), ld_f * c, F32))
    cd_b = jnp.exp(jnp.full((1, RET_V_DIM), ld_b * c, F32))
    gain = gain_ref[0]
    k_scale = RET_QK_DIM ** -0.5

    def chunk_rows(n):
        return pl.ds(pl.multiple_of(n * c, c), c)

    def rotate(n, carry):
        rows = chunk_rows(n)
        cos = cos_ref[rows, :]
        sin = sin_ref[rows, :]
        xq = q_ref[0, rows, :].astype(F32)
        xk = k_ref[0, rows, :].astype(F32)
        qr_ref[rows, :] = xq * cos + pltpu.roll(xq, RET_QK_DIM // 2, 1) * sin
        kr_ref[rows, :] = (xk * cos + pltpu.roll(xk, RET_QK_DIM // 2, 1) * sin) * k_scale
        return carry

    lax.fori_loop(0, nc, rotate, 0)

    def kv_outer(kd, v):
        return lax.dot_general(kd, v, (((0,), (0,)), ((), ())), preferred_element_type=F32)

    st_ref[...] = jnp.zeros_like(st_ref)

    def backward(t, carry):
        rows = chunk_rows(nc - 1 - t)
        st = st_ref[...]
        qd = (qr_ref[rows, :] * qdec_b).astype(BF16)
        yb_ref[rows, :] = jnp.dot(qd, st.astype(BF16), preferred_element_type=F32)
        kd = (kr_ref[rows, :] * kdec_b).astype(BF16)
        st_ref[...] = st * cd_b + kv_outer(kd, v_ref[0, rows, :])
        return carry

    lax.fori_loop(0, nc, backward, 0)

    st_ref[...] = jnp.zeros_like(st_ref)

    def forward(n, carry):
        rows = chunk_rows(n)
        st = st_ref[...]
        q = qr_ref[rows, :]
        k = kr_ref[rows, :]
        v = v_ref[0, rows, :]
        s = lax.dot_general(q.astype(BF16), k.astype(BF16), (((1,), (1,)), ((), ())),
                            preferred_element_type=F32) * mask
        inner = jnp.dot(s.astype(BF16), v, preferred_element_type=F32)
        cross = jnp.dot((q * qdec_f).astype(BF16), st.astype(BF16), preferred_element_type=F32)
        y = yb_ref[rows, :] + inner + cross
        st_ref[...] = st * cd_f + kv_outer((k * kdec_f).astype(BF16), v)
        ms = jnp.mean(y * y, axis=-1, keepdims=True)
        yn = (y * lax.rsqrt(ms + NORM_EPS)) * gain
        gate = g_ref[0, rows, :].astype(F32)
        o_ref[0, rows, :] = ((gate * jax.nn.sigmoid(gate)) * yn).astype(o_ref.dtype)
        return carry

    lax.fori_loop(0, nc, forward, 0)


def _retention(proj, log_decay, gain, cos2, sin2):
    b, seq, _ = proj.shape
    qk_blocks = RET_QK_WIDTH // RET_QK_DIM
    v_block0 = 2 * RET_QK_WIDTH // RET_V_DIM
    g_block0 = v_block0 + RET_HEADS
    return pl.pallas_call(
        _retention_kernel,
        grid=(b, RET_HEADS),
        in_specs=[
            pl.BlockSpec(memory_space=pltpu.SMEM),
            pl.BlockSpec((1, seq, RET_QK_DIM), lambda i, h: (i, 0, h)),
            pl.BlockSpec((1, seq, RET_QK_DIM), lambda i, h: (i, 0, qk_blocks + h)),
            pl.BlockSpec((1, seq, RET_V_DIM), lambda i, h: (i, 0, v_block0 + h)),
            pl.BlockSpec((1, seq, RET_V_DIM), lambda i, h: (i, 0, g_block0 + h)),
            pl.BlockSpec((seq, RET_QK_DIM), lambda i, h: (0, 0)),
            pl.BlockSpec((seq, RET_QK_DIM), lambda i, h: (0, 0)),
            pl.BlockSpec((1, 1, RET_V_DIM), lambda i, h: (h, 0, 0)),
        ],
        out_specs=pl.BlockSpec((1, seq, RET_V_DIM), lambda i, h: (i, 0, h)),
        out_shape=jax.ShapeDtypeStruct((b, seq, D_MODEL), BF16),
        scratch_shapes=[
            pltpu.VMEM((seq, RET_QK_DIM), F32),
            pltpu.VMEM((seq, RET_QK_DIM), F32),
            pltpu.VMEM((seq, RET_V_DIM), F32),
            pltpu.VMEM((RET_QK_DIM, RET_V_DIM), F32),
        ],
        compiler_params=_params(("parallel", "arbitrary")),
        name="retention",
    )(log_decay, proj, proj, proj, proj, cos2, sin2, gain.reshape(RET_HEADS, 1, RET_V_DIM))


COPY_ROWS = 128


def _attention_kernel(q_ref, k_ref, v_ref, diag_ref, mix_ref, o_ref,
                      t_ref, stage_ref, q1_ref, q4_ref, k4_ref, v4_ref, q16_ref, k16_ref, v16_ref,
                      ob_ref, lb_ref, to_ref, tl_ref, s_ref, p_ref, den_ref):
    del mix_ref
    seq = q_ref.shape[1]
    q_scale = ATT_HEAD_DIM ** -0.5

    @pl.when(pl.program_id(1) == 0)
    def _():
        for bi in range(len(DILATIONS)):
            wide = jnp.broadcast_to(diag_ref[0, bi], (ATT_QB, 2 * ATT_KW))
            for vi, off in enumerate(BIAS_OFFSETS):
                rolled = pltpu.roll(wide, ATT_KW - off, 1, stride=1, stride_axis=0)
                t_ref[bi, vi] = rolled[:, :ATT_KW]

    def deinterleave(dst_ref, d):
        length = seq // d
        ch = min(COPY_ROWS, length)
        for r in range(d):
            for cc in range(length // ch):
                src = stage_ref[pl.ds(r + cc * ch * d, ch, stride=d), :]
                dst_ref[pl.ds(r * length + cc * ch, ch), :] = src.astype(BF16)

    def stage(src_ref, scale):
        def body(cc, carry):
            rows = pl.ds(pl.multiple_of(cc * COPY_ROWS, COPY_ROWS), COPY_ROWS)
            x = src_ref[0, rows, :].astype(F32)
            stage_ref[rows, :] = x * scale if scale is not None else x
            return carry

        lax.fori_loop(0, seq // COPY_ROWS, body, 0)

    stage(q_ref, q_scale)

    def q1_body(cc, carry):
        rows = pl.ds(pl.multiple_of(cc * COPY_ROWS, COPY_ROWS), COPY_ROWS)
        q1_ref[rows, :] = stage_ref[rows, :].astype(BF16)
        return carry

    lax.fori_loop(0, seq // COPY_ROWS, q1_body, 0)
    deinterleave(q4_ref, 4)
    deinterleave(q16_ref, 16)
    stage(k_ref, None)
    deinterleave(k4_ref, 4)
    deinterleave(k16_ref, 16)
    stage(v_ref, None)
    deinterleave(v4_ref, 4)
    deinterleave(v16_ref, 16)

    def run_branch(bi, d, load_q, load_k, load_v, out_o_ref, out_l_ref):
        length = seq // d
        kw = min(length, ATT_KW)
        nblk = length // ATT_QB

        def tile_rows(t):
            r = t // nblk
            nb = t - r * nblk
            i0 = nb * ATT_QB
            ws = jnp.clip(i0 - ATT_HALF, 0, length - kw)
            var = jnp.where(nb == 0, 0, jnp.where(nb == nblk - 1, 2, 1))
            base = r * length
            qrows = pl.ds(pl.multiple_of(base + i0, ATT_QB), ATT_QB)
            krows = pl.ds(pl.multiple_of(base + ws, ATT_HALF), kw)
            return qrows, krows, var

        def body(g, carry):
            tiles = [tile_rows(g * ATT_TILE_GROUP + j) for j in range(ATT_TILE_GROUP)]
            for j, (qrows, krows, var) in enumerate(tiles):
                s = lax.dot_general(load_q(qrows), load_k(krows), (((1,), (1,)), ((), ())),
                                    preferred_element_type=F32)
                s_ref[j, :, :kw] = s + t_ref[bi, var, :, :kw]
            for j, (qrows, krows, var) in enumerate(tiles):
                s = s_ref[j, :, :kw]
                m = jnp.max(s, axis=-1, keepdims=True)
                p = jnp.exp(s - m)
                den = jnp.sum(p, axis=-1, keepdims=True)
                p_ref[j, :, :kw] = p.astype(BF16)
                den_ref[j] = jnp.broadcast_to(den, (ATT_QB, ATT_HEAD_DIM))
                out_l_ref[qrows, :] = jnp.broadcast_to(m + jnp.log(den), (ATT_QB, ATT_HEAD_DIM))
            for j, (qrows, krows, var) in enumerate(tiles):
                acc = jnp.dot(p_ref[j, :, :kw], load_v(krows), preferred_element_type=F32)
                out_o_ref[qrows, :] = acc / den_ref[j]
            return carry

        lax.fori_loop(0, d * nblk // ATT_TILE_GROUP, body, 0)

    def reinterleave(bi, d):
        length = seq // d
        ch = min(COPY_ROWS, length)
        for r in range(d):
            for cc in range(length // ch):
                src = pl.ds(r * length + cc * ch, ch)
                dst = pl.ds(r + cc * ch * d, ch, stride=d)
                ob_ref[bi, dst, :] = to_ref[src, :]
                lb_ref[bi, dst, :] = tl_ref[src, :]

    run_branch(0, 1, lambda rows: q1_ref[rows, :], lambda rows: k_ref[0, rows, :], lambda rows: v_ref[0, rows, :],
               ob_ref.at[0], lb_ref.at[0])
    run_branch(1, 4, lambda rows: q4_ref[rows, :], lambda rows: k4_ref[rows, :], lambda rows: v4_ref[rows, :],
               to_ref, tl_ref)
    reinterleave(1, 4)
    run_branch(2, 16, lambda rows: q16_ref[rows, :], lambda rows: k16_ref[rows, :], lambda rows: v16_ref[rows, :],
               to_ref, tl_ref)
    reinterleave(2, 16)

    def combine(cc, carry):
        rows = pl.ds(pl.multiple_of(cc * COPY_ROWS, COPY_ROWS), COPY_ROWS)
        l0 = lb_ref[0, rows, :]
        l1 = lb_ref[1, rows, :]
        l2 = lb_ref[2, rows, :]
        top = jnp.maximum(jnp.maximum(l0, l1), l2)
        e0 = jnp.exp(l0 - top)
        e1 = jnp.exp(l1 - top)
        e2 = jnp.exp(l2 - top)
        tot = e0 + e1 + e2
        o = (e0 / tot) * ob_ref[0, rows, :] + (e1 / tot) * ob_ref[1, rows, :] + (e2 / tot) * ob_ref[2, rows, :]
        o_ref[0, rows, :] = o.astype(o_ref.dtype)
        return carry

    lax.fori_loop(0, seq // COPY_ROWS, combine, 0)


def _attention(proj, bias_diags, mix):
    b, seq, _ = proj.shape
    q_block0 = (2 * RET_QK_WIDTH + 2 * RET_WIDTH) // ATT_HEAD_DIM
    k_block0 = q_block0 + ATT_HEADS
    v_block0 = k_block0 + ATT_HEADS
    out_block0 = RET_WIDTH // ATT_HEAD_DIM
    nb, nv = len(DILATIONS), len(BIAS_OFFSETS)
    head_rows = pltpu.VMEM((seq, ATT_HEAD_DIM), BF16)
    head_rows_f32 = pltpu.VMEM((seq, ATT_HEAD_DIM), F32)
    return pl.pallas_call(
        _attention_kernel,
        grid=(ATT_HEADS, b),
        in_specs=[
            pl.BlockSpec((1, seq, ATT_HEAD_DIM), lambda h, i: (i, 0, q_block0 + h)),
            pl.BlockSpec((1, seq, ATT_HEAD_DIM), lambda h, i: (i, 0, k_block0 + h)),
            pl.BlockSpec((1, seq, ATT_HEAD_DIM), lambda h, i: (i, 0, v_block0 + h)),
            pl.BlockSpec((1, nb, 1, 2 * ATT_KW), lambda h, i: (h, 0, 0, 0)),
            pl.BlockSpec(memory_space=pl.ANY),
        ],
        out_specs=pl.BlockSpec((1, seq, ATT_HEAD_DIM), lambda h, i: (i, 0, out_block0 + h)),
        out_shape=jax.ShapeDtypeStruct((b, seq, D_MODEL), BF16),
        scratch_shapes=[
            pltpu.VMEM((nb, nv, ATT_QB, ATT_KW), F32),
            head_rows_f32,
            head_rows,
            head_rows, head_rows, head_rows,
            head_rows, head_rows, head_rows,
            pltpu.VMEM((nb, seq, ATT_HEAD_DIM), F32),
            pltpu.VMEM((nb, seq, ATT_HEAD_DIM), F32),
            head_rows_f32, head_rows_f32,
            pltpu.VMEM((ATT_TILE_GROUP, ATT_QB, ATT_KW), F32),
            pltpu.VMEM((ATT_TILE_GROUP, ATT_QB, ATT_KW), BF16),
            pltpu.VMEM((ATT_TILE_GROUP, ATT_QB, ATT_HEAD_DIM), F32),
        ],
        input_output_aliases={4: 0},
        compiler_params=_params(("arbitrary", "arbitrary")),
        name="dilated_attention",
    )(proj, proj, proj, bias_diags, mix)


def _rotary_tables(seq):
    half = RET_QK_DIM // 2
    inv = ROPE_BASE ** (-jnp.arange(half, dtype=F32) / half)
    ang = jnp.arange(seq, dtype=F32)[:, None] * inv[None, :]
    cos, sin = jnp.cos(ang), jnp.sin(ang)
    return jnp.concatenate([cos, cos], axis=-1), jnp.concatenate([-sin, sin], axis=-1)


def _bucket_of(rel):
    nbk = REL_BUCKETS // 2
    max_exact = nbk // 2
    n = np.abs(rel)
    nf = np.maximum(n, 1).astype(np.float32)
    large = max_exact + (np.log(nf / max_exact) / math.log(REL_MAX_DISTANCE / max_exact)
                         * (nbk - max_exact)).astype(np.int32)
    large = np.minimum(large, nbk - 1)
    return np.where(rel > 0, nbk, 0) + np.where(n < max_exact, n, large)


def _bias_diags(rel_bias_table):
    delta = np.arange(2 * ATT_KW) - ATT_KW
    buckets = np.stack([_bucket_of(delta * d) for d in DILATIONS])
    inside = np.abs(delta) <= ATT_HALF
    bias = jnp.transpose(rel_bias_table.astype(F32))[:, buckets]
    return jnp.where(inside[None, None], bias, NEG_INF)[:, :, None, :]


def _trunk(x, bias_diags, weights):
    b, seq, dm = x.shape
    cos2, sin2 = _rotary_tables(seq)
    xt = x.reshape(b * seq, dm)
    for layer in weights:
        proj = _norm_matmul(xt, layer["norm_mix_pre"], layer["w_in"], relu2=False, name="in_proj")
        proj = proj.reshape(b, seq, IN_WIDTH)
        mix = _retention(proj, layer["log_decay"], layer["ret_norm_gain"], cos2, sin2)
        mix = _attention(proj, bias_diags, mix)
        xt = _matmul_norm_res(mix.reshape(b * seq, dm), layer["w_out"], xt, layer["norm_mix_post"], name="out_proj")
        u = _norm_matmul(xt, layer["norm_mlp_pre"], layer["w_up"], relu2=True, name="up_proj")
        xt = _matmul_norm_res(u, layer["w_down"], xt, layer["norm_mlp_post"], name="down_proj")
    return xt.reshape(b, seq, dm)


def kernel(x_prompt, x_sample, rel_bias_table, w_in, ret_log_decay, ret_norm_gain, w_out, w_up, w_down,
           norm_mix_pre, norm_mix_post, norm_mlp_pre, norm_mlp_post):
    depth = w_in.shape[0]
    weights = []
    for l in range(depth):
        weights.append({
            "w_in": w_in[l].astype(BF16),
            "w_out": w_out[l].astype(BF16),
            "w_up": w_up[l].astype(BF16),
            "w_down": w_down[l].astype(BF16),
            "log_decay": -jnp.exp(ret_log_decay[l].astype(F32)),
            "ret_norm_gain": ret_norm_gain[l].astype(F32),
            "norm_mix_pre": norm_mix_pre[l].astype(F32),
            "norm_mix_post": norm_mix_post[l].astype(F32),
            "norm_mlp_pre": norm_mlp_pre[l].astype(F32),
            "norm_mlp_post": norm_mlp_post[l].astype(F32),
        })
    bias_diags = _bias_diags(rel_bias_table)
    y_prompt = _trunk(x_prompt, bias_diags, weights)
    y_sample = _trunk(x_sample, bias_diags, weights)
    return (y_prompt, y_sample)
```

```python
import functools
import math

import numpy as np
import jax
import jax.numpy as jnp
from jax import lax
from jax.experimental import pallas as pl
from jax.experimental.pallas import tpu as pltpu

BF16 = jnp.bfloat16
F32 = jnp.float32

D_MODEL = 4096
RET_HEADS = 8
RET_QK_DIM = 128
RET_V_DIM = 256
RET_QK_WIDTH = RET_HEADS * RET_QK_DIM
RET_WIDTH = RET_HEADS * RET_V_DIM
ATT_HEADS = 16
ATT_HEAD_DIM = 128
ATT_WIDTH = ATT_HEADS * ATT_HEAD_DIM
IN_WIDTH = 2 * RET_QK_WIDTH + 2 * RET_WIDTH + 3 * ATT_WIDTH
RET_CHUNK = 128
RET_CHUNK_GROUP = 4
ROPE_BASE = 10000.0
DILATIONS = (1, 4, 16)
ATT_HALF = 64
REL_BUCKETS = 32
REL_MAX_DISTANCE = 1024
NORM_EPS = 1e-6
NEG_INF = -1e30

ATT_QB = 128
ATT_KW = 256
BIAS_OFFSETS = (0, -ATT_HALF, -2 * ATT_HALF)
ATT_TILE_GROUP = 8

V7X_VMEM_LIMIT_BYTES = 56 * 1024 * 1024
LANES = 128


def _params(sem, vmem=V7X_VMEM_LIMIT_BYTES):
    return pltpu.CompilerParams(dimension_semantics=sem, vmem_limit_bytes=vmem)


NORM_ROWS = 16
NORM_UNROLL = 4


def _norm_matmul_kernel(x_ref, g_ref, w_ref, o_ref, h_ref, *, relu2):
    tm = x_ref.shape[0]

    @pl.when(pl.program_id(1) == 0)
    def _():
        gain = g_ref[...]

        def body(c, carry):
            rows = pl.ds(pl.multiple_of(c * NORM_ROWS, NORM_ROWS), NORM_ROWS)
            x = x_ref[rows, :]
            ms = jnp.mean(x * x, axis=-1, keepdims=True)
            h_ref[rows, :] = ((x * lax.rsqrt(ms + NORM_EPS)) * gain).astype(BF16)
            return carry

        lax.fori_loop(0, tm // NORM_ROWS, body, 0, unroll=NORM_UNROLL)

    y = jnp.dot(h_ref[...], w_ref[...], preferred_element_type=F32)
    if relu2:
        y = jnp.maximum(y, 0.0)
        y = y * y
    o_ref[...] = y.astype(o_ref.dtype)


def _norm_matmul(x, gain, w, *, relu2, tm=512, tn=1024, name):
    t, k = x.shape
    n = w.shape[1]
    return pl.pallas_call(
        functools.partial(_norm_matmul_kernel, relu2=relu2),
        grid=(t // tm, n // tn),
        in_specs=[
            pl.BlockSpec((tm, k), lambda i, j: (i, 0)),
            pl.BlockSpec((1, k), lambda i, j: (0, 0)),
            pl.BlockSpec((k, tn), lambda i, j: (0, j)),
        ],
        out_specs=pl.BlockSpec((tm, tn), lambda i, j: (i, j)),
        out_shape=jax.ShapeDtypeStruct((t, n), BF16),
        scratch_shapes=[pltpu.VMEM((tm, k), BF16)],
        compiler_params=_params(("parallel", "arbitrary")),
        name=name,
    )(x, gain.reshape(1, k), w)


def _matmul_norm_res_kernel(a_ref, w_ref, x_ref, g_ref, o_ref, rs_ref):
    kk = pl.program_id(1)
    nk = pl.num_programs(1)
    tm, n = o_ref.shape

    @pl.when(kk == 0)
    def _():
        o_ref[...] = jnp.dot(a_ref[...], w_ref[...], preferred_element_type=F32)

    @pl.when(kk > 0)
    def _():
        o_ref[...] += jnp.dot(a_ref[...], w_ref[...], preferred_element_type=F32)

    @pl.when(kk == nk - 1)
    def _():
        def scale(c, carry):
            rows = pl.ds(pl.multiple_of(c * NORM_ROWS, NORM_ROWS), NORM_ROWS)
            y = o_ref[rows, :]
            ms = jnp.mean(y * y, axis=-1, keepdims=True)
            rs_ref[rows, :] = jnp.broadcast_to(lax.rsqrt(ms + NORM_EPS), (NORM_ROWS, LANES))
            return carry

        lax.fori_loop(0, tm // NORM_ROWS, scale, 0, unroll=NORM_UNROLL)

        def apply(c, carry):
            rows = pl.ds(pl.multiple_of(c * NORM_ROWS, NORM_ROWS), NORM_ROWS)
            rs = rs_ref[rows, :]
            for lo in range(0, n, LANES):
                cols = pl.ds(lo, LANES)
                o_ref[rows, cols] = x_ref[rows, cols] + (o_ref[rows, cols] * rs) * g_ref[:, cols]
            return carry

        lax.fori_loop(0, tm // NORM_ROWS, apply, 0, unroll=NORM_UNROLL)


def _matmul_norm_res(a, w, x, gain, *, tm=512, tk=1024, name):
    t, k = a.shape
    n = w.shape[1]
    return pl.pallas_call(
        _matmul_norm_res_kernel,
        grid=(t // tm, k // tk),
        in_specs=[
            pl.BlockSpec((tm, tk), lambda i, kk: (i, kk)),
            pl.BlockSpec((tk, n), lambda i, kk: (kk, 0)),
            pl.BlockSpec((tm, n), lambda i, kk: (i, 0)),
            pl.BlockSpec((1, n), lambda i, kk: (0, 0)),
        ],
        out_specs=pl.BlockSpec((tm, n), lambda i, kk: (i, 0)),
        out_shape=jax.ShapeDtypeStruct((t, n), F32),
        scratch_shapes=[pltpu.VMEM((tm, LANES), F32)],
        compiler_params=_params(("parallel", "arbitrary")),
        name=name,
    )(a, w, x, gain.reshape(1, n))


def _retention_kernel(ld_ref, q_ref, k_ref, v_ref, g_ref, cos_ref, sin_ref, gain_ref, o_ref,
                      qd_ref, y_ref, delta_ref, prev_ref, s_ref, kd_ref):
    seq = q_ref.shape[1]
    c = RET_CHUNK
    nc = seq // c
    grp = RET_CHUNK_GROUP
    h = pl.program_id(1)
    ld_f = ld_ref[0, h]
    ld_b = ld_ref[1, h]

    ri = lax.broadcasted_iota(jnp.int32, (c, c), 0)
    ci = lax.broadcasted_iota(jnp.int32, (c, c), 1)
    diff = (ri - ci).astype(F32)
    mask = jnp.where(diff >= 0, jnp.exp(ld_f * jnp.maximum(diff, 0.0)), jnp.exp(ld_b * jnp.maximum(-diff, 0.0)))
    pos = lax.broadcasted_iota(jnp.int32, (c, RET_QK_DIM), 0).astype(F32)
    qdec_f = jnp.exp(ld_f * (pos + 1.0))
    kdec_f = jnp.exp(ld_f * (c - 1.0 - pos))
    qdec_b = jnp.exp(ld_b * (c - pos))
    kdec_b = jnp.exp(ld_b * pos)
    cd_f = jnp.exp(jnp.full((1, RET_V_DIM), ld_f * c, F32))
    cd_b = jnp.exp(jnp.full((1, RET_V_DIM), ld_b * c, F32))
    gain = gain_ref[0]
    k_scale = RET_QK_DIM ** -0.5
    half = RET_QK_DIM // 2

    def chunk_rows(n):
        return pl.ds(pl.multiple_of(n * c, c), c)

    def phase_a(gi, carry):
        chunks = [gi * grp + j for j in range(grp)]
        for j, n in enumerate(chunks):
            rows = chunk_rows(n)
            cos = cos_ref[rows, :]
            sin = sin_ref[rows, :]
            xq = q_ref[0, rows, :].astype(F32)
            xk = k_ref[0, rows, :].astype(F32)
            q = xq * cos + pltpu.roll(xq, half, 1) * sin
            k = (xk * cos + pltpu.roll(xk, half, 1) * sin) * k_scale
            qd_ref[rows, :] = jnp.concatenate([(q * qdec_f).astype(BF16), (q * qdec_b).astype(BF16)], axis=1)
            kd_ref[j] = jnp.concatenate([(k * kdec_f).astype(BF16), (k * kdec_b).astype(BF16)], axis=1)
            s_ref[j] = lax.dot_general(q.astype(BF16), k.astype(BF16), (((1,), (1,)), ((), ())),
                                       preferred_element_type=F32)
        for j, n in enumerate(chunks):
            rows = chunk_rows(n)
            y_ref[rows, :] = jnp.dot((s_ref[j] * mask).astype(BF16), v_ref[0, rows, :],
                                     preferred_element_type=F32)
        for j, n in enumerate(chunks):
            delta_ref[n] = lax.dot_general(kd_ref[j], v_ref[0, chunk_rows(n), :], (((0,), (0,)), ((), ())),
                                           preferred_element_type=F32)
        return carry

    lax.fori_loop(0, nc // grp, phase_a, 0)

    def scan_f(n, st):
        prev_ref[n, pl.ds(0, RET_QK_DIM), :] = st.astype(BF16)
        return st * cd_f + delta_ref[n, pl.ds(0, RET_QK_DIM), :]

    def scan_b(t, st):
        n = nc - 1 - t
        prev_ref[n, pl.ds(RET_QK_DIM, RET_QK_DIM), :] = st.astype(BF16)
        return st * cd_b + delta_ref[n, pl.ds(RET_QK_DIM, RET_QK_DIM), :]

    zero_state = jnp.zeros((RET_QK_DIM, RET_V_DIM), F32)
    lax.fori_loop(0, nc, scan_f, zero_state)
    lax.fori_loop(0, nc, scan_b, zero_state)

    def phase_c(gi, carry):
        chunks = [gi * grp + j for j in range(grp)]
        ys = []
        for n in chunks:
            rows = chunk_rows(n)
            ys.append(y_ref[rows, :] + jnp.dot(qd_ref[rows, :], prev_ref[n], preferred_element_type=F32))
        for n, y in zip(chunks, ys):
            rows = chunk_rows(n)
            ms = jnp.mean(y * y, axis=-1, keepdims=True)
            yn = (y * lax.rsqrt(ms + NORM_EPS)) * gain
            gate = g_ref[0, rows, :].astype(F32)
            o_ref[0, rows, :] = ((gate * jax.nn.sigmoid(gate)) * yn).astype(o_ref.dtype)
        return carry

    lax.fori_loop(0, nc // grp, phase_c, 0)


def _retention(proj, log_decay, gain, cos2, sin2):
    b, seq, _ = proj.shape
    qk_blocks = RET_QK_WIDTH // RET_QK_DIM
    v_block0 = 2 * RET_QK_WIDTH // RET_V_DIM
    g_block0 = v_block0 + RET_HEADS
    nc = seq // RET_CHUNK
    return pl.pallas_call(
        _retention_kernel,
        grid=(b, RET_HEADS),
        in_specs=[
            pl.BlockSpec(memory_space=pltpu.SMEM),
            pl.BlockSpec((1, seq, RET_QK_DIM), lambda i, h: (i, 0, h)),
            pl.BlockSpec((1, seq, RET_QK_DIM), lambda i, h: (i, 0, qk_blocks + h)),
            pl.BlockSpec((1, seq, RET_V_DIM), lambda i, h: (i, 0, v_block0 + h)),
            pl.BlockSpec((1, seq, RET_V_DIM), lambda i, h: (i, 0, g_block0 + h)),
            pl.BlockSpec((seq, RET_QK_DIM), lambda i, h: (0, 0)),
            pl.BlockSpec((seq, RET_QK_DIM), lambda i, h: (0, 0)),
            pl.BlockSpec((1, 1, RET_V_DIM), lambda i, h: (h, 0, 0)),
        ],
        out_specs=pl.BlockSpec((1, seq, RET_V_DIM), lambda i, h: (i, 0, h)),
        out_shape=jax.ShapeDtypeStruct((b, seq, D_MODEL), BF16),
        scratch_shapes=[
            pltpu.VMEM((seq, 2 * RET_QK_DIM), BF16),
            pltpu.VMEM((seq, RET_V_DIM), F32),
            pltpu.VMEM((nc, 2 * RET_QK_DIM, RET_V_DIM), F32),
            pltpu.VMEM((nc, 2 * RET_QK_DIM, RET_V_DIM), BF16),
            pltpu.VMEM((RET_CHUNK_GROUP, RET_CHUNK, RET_CHUNK), F32),
            pltpu.VMEM((RET_CHUNK_GROUP, RET_CHUNK, 2 * RET_QK_DIM), BF16),
        ],
        compiler_params=_params(("parallel", "arbitrary")),
        name="retention",
    )(log_decay, proj, proj, proj, proj, cos2, sin2, gain.reshape(RET_HEADS, 1, RET_V_DIM))


COPY_ROWS = 128


def _attention_kernel(q_ref, k_ref, v_ref, diag_ref, mix_ref, o_ref,
                      t_ref, stage_ref, q1_ref, q4_ref, k4_ref, v4_ref, q16_ref, k16_ref, v16_ref,
                      oa_ref, la_ref, ob_ref, lb_ref, s_ref, p_ref, den_ref, lse_ref):
    del mix_ref
    seq = q_ref.shape[1]
    q_scale = ATT_HEAD_DIM ** -0.5

    @pl.when(pl.program_id(1) == 0)
    def _():
        for bi in range(len(DILATIONS)):
            wide = jnp.broadcast_to(diag_ref[0, bi], (ATT_QB, 2 * ATT_KW))
            for vi, off in enumerate(BIAS_OFFSETS):
                rolled = pltpu.roll(wide, ATT_KW - off, 1, stride=1, stride_axis=0)
                t_ref[bi, vi] = rolled[:, :ATT_KW]

    def deinterleave(dst_ref, d):
        length = seq // d
        ch = min(COPY_ROWS, length)
        for r in range(d):
            for cc in range(length // ch):
                src = stage_ref[pl.ds(r + cc * ch * d, ch, stride=d), :]
                dst_ref[pl.ds(r * length + cc * ch, ch), :] = src.astype(BF16)

    def stage(src_ref, scale):
        def body(cc, carry):
            rows = pl.ds(pl.multiple_of(cc * COPY_ROWS, COPY_ROWS), COPY_ROWS)
            x = src_ref[0, rows, :].astype(F32)
            stage_ref[rows, :] = x * scale if scale is not None else x
            return carry

        lax.fori_loop(0, seq // COPY_ROWS, body, 0)

    stage(q_ref, q_scale)

    def q1_body(cc, carry):
        rows = pl.ds(pl.multiple_of(cc * COPY_ROWS, COPY_ROWS), COPY_ROWS)
        q1_ref[rows, :] = stage_ref[rows, :].astype(BF16)
        return carry

    lax.fori_loop(0, seq // COPY_ROWS, q1_body, 0)
    deinterleave(q4_ref, 4)
    deinterleave(q16_ref, 16)
    stage(k_ref, None)
    deinterleave(k4_ref, 4)
    deinterleave(k16_ref, 16)
    stage(v_ref, None)
    deinterleave(v4_ref, 4)
    deinterleave(v16_ref, 16)

    def run_branch(bi, d, load_q, load_k, load_v, run_o_ref, run_l_ref, first, last):
        length = seq // d
        kw = min(length, ATT_KW)
        nblk = length // ATT_QB

        def tile_rows(t):
            r = t // nblk
            nb = t - r * nblk
            i0 = nb * ATT_QB
            ws = jnp.clip(i0 - ATT_HALF, 0, length - kw)
            var = jnp.where(nb == 0, 0, jnp.where(nb == nblk - 1, 2, 1))
            base = r * length
            qrows = pl.ds(pl.multiple_of(base + i0, ATT_QB), ATT_QB)
            krows = pl.ds(pl.multiple_of(base + ws, ATT_HALF), kw)
            return qrows, krows, var

        def body(g, carry):
            tiles = [tile_rows(g * ATT_TILE_GROUP + j) for j in range(ATT_TILE_GROUP)]
            for j, (qrows, krows, var) in enumerate(tiles):
                s = lax.dot_general(load_q(qrows), load_k(krows), (((1,), (1,)), ((), ())),
                                    preferred_element_type=F32)
                s_ref[j, :, :kw] = s + t_ref[bi, var, :, :kw]
            for j, (qrows, krows, var) in enumerate(tiles):
                s = s_ref[j, :, :kw]
                m = jnp.max(s, axis=-1, keepdims=True)
                p = jnp.exp(s - m)
                den = jnp.sum(p, axis=-1, keepdims=True)
                p_ref[j, :, :kw] = p.astype(BF16)
                den_ref[j] = jnp.broadcast_to(den, (ATT_QB, ATT_HEAD_DIM))
                lse_ref[j] = jnp.broadcast_to(m + jnp.log(den), (ATT_QB, ATT_HEAD_DIM))
            for j, (qrows, krows, var) in enumerate(tiles):
                acc = jnp.dot(p_ref[j, :, :kw], load_v(krows), preferred_element_type=F32)
                o_new = acc / den_ref[j]
                l_new = lse_ref[j]
                if first:
                    run_o_ref[qrows, :] = o_new
                    run_l_ref[qrows, :] = l_new
                    continue
                o_run = run_o_ref[qrows, :]
                l_run = run_l_ref[qrows, :]
                top = jnp.maximum(l_new, l_run)
                e_new = jnp.exp(l_new - top)
                e_run = jnp.exp(l_run - top)
                tot = e_new + e_run
                o_mix = (e_new / tot) * o_new + (e_run / tot) * o_run
                if last:
                    o_ref[0, qrows, :] = o_mix.astype(o_ref.dtype)
                else:
                    run_o_ref[qrows, :] = o_mix
                    run_l_ref[qrows, :] = top + jnp.log(tot)
            return carry

        lax.fori_loop(0, d * nblk // ATT_TILE_GROUP, body, 0)

    def regroup(src_refs, dst_refs, n_src, n_dst):
        len_src = seq // n_src
        len_dst = seq // n_dst
        ch = min(COPY_ROWS, len_src)
        for r_dst in range(n_dst):
            for r_sub in range(4):
                r_src = r_dst + n_dst * r_sub
                for cc in range(len_src // ch):
                    src = pl.ds(r_src * len_src + cc * ch, ch)
                    dst = pl.ds(r_dst * len_dst + r_sub + 4 * cc * ch, ch, stride=4)
                    for src_ref, dst_ref in zip(src_refs, dst_refs):
                        dst_ref[dst, :] = src_ref[src, :]

    run_branch(2, 16, lambda rows: q16_ref[rows, :], lambda rows: k16_ref[rows, :], lambda rows: v16_ref[rows, :],
               oa_ref, la_ref, first=True, last=False)
    regroup((oa_ref, la_ref), (ob_ref, lb_ref), 16, 4)
    run_branch(1, 4, lambda rows: q4_ref[rows, :], lambda rows: k4_ref[rows, :], lambda rows: v4_ref[rows, :],
               ob_ref, lb_ref, first=False, last=False)
    regroup((ob_ref, lb_ref), (oa_ref, la_ref), 4, 1)
    run_branch(0, 1, lambda rows: q1_ref[rows, :], lambda rows: k_ref[0, rows, :], lambda rows: v_ref[0, rows, :],
               oa_ref, la_ref, first=False, last=True)


def _attention(proj, bias_diags, mix):
    b, seq, _ = proj.shape
    q_block0 = (2 * RET_QK_WIDTH + 2 * RET_WIDTH) // ATT_HEAD_DIM
    k_block0 = q_block0 + ATT_HEADS
    v_block0 = k_block0 + ATT_HEADS
    out_block0 = RET_WIDTH // ATT_HEAD_DIM
    nb, nv = len(DILATIONS), len(BIAS_OFFSETS)
    head_rows = pltpu.VMEM((seq, ATT_HEAD_DIM), BF16)
    head_rows_f32 = pltpu.VMEM((seq, ATT_HEAD_DIM), F32)
    return pl.pallas_call(
        _attention_kernel,
        grid=(ATT_HEADS, b),
        in_specs=[
            pl.BlockSpec((1, seq, ATT_HEAD_DIM), lambda h, i: (i, 0, q_block0 + h)),
            pl.BlockSpec((1, seq, ATT_HEAD_DIM), lambda h, i: (i, 0, k_block0 + h)),
            pl.BlockSpec((1, seq, ATT_HEAD_DIM), lambda h, i: (i, 0, v_block0 + h)),
            pl.BlockSpec((1, nb, 1, 2 * ATT_KW), lambda h, i: (h, 0, 0, 0)),
            pl.BlockSpec(memory_space=pl.ANY),
        ],
        out_specs=pl.BlockSpec((1, seq, ATT_HEAD_DIM), lambda h, i: (i, 0, out_block0 + h)),
        out_shape=jax.ShapeDtypeStruct((b, seq, D_MODEL), BF16),
        scratch_shapes=[
            pltpu.VMEM((nb, nv, ATT_QB, ATT_KW), F32),
            head_rows_f32,
            head_rows,
            head_rows, head_rows, head_rows,
            head_rows, head_rows, head_rows,
            head_rows_f32, head_rows_f32,
            head_rows_f32, head_rows_f32,
            pltpu.VMEM((ATT_TILE_GROUP, ATT_QB, ATT_KW), F32),
            pltpu.VMEM((ATT_TILE_GROUP, ATT_QB, ATT_KW), BF16),
            pltpu.VMEM((ATT_TILE_GROUP, ATT_QB, ATT_HEAD_DIM), F32),
            pltpu.VMEM((ATT_TILE_GROUP, ATT_QB, ATT_HEAD_DIM), F32),
        ],
        input_output_aliases={4: 0},
        compiler_params=_params(("arbitrary", "arbitrary")),
        name="dilated_attention",
    )(proj, proj, proj, bias_diags, mix)


def _rotary_tables(seq):
    half = RET_QK_DIM // 2
    inv = ROPE_BASE ** (-jnp.arange(half, dtype=F32) / half)
    ang = jnp.arange(seq, dtype=F32)[:, None] * inv[None, :]
    cos, sin = jnp.cos(ang), jnp.sin(ang)
    return jnp.concatenate([cos, cos], axis=-1), jnp.concatenate([-sin, sin], axis=-1)


def _bucket_of(rel):
    nbk = REL_BUCKETS // 2
    max_exact = nbk // 2
    n = np.abs(rel)
    nf = np.maximum(n, 1).astype(np.float32)
    large = max_exact + (np.log(nf / max_exact) / math.log(REL_MAX_DISTANCE / max_exact)
                         * (nbk - max_exact)).astype(np.int32)
    large = np.minimum(large, nbk - 1)
    return np.where(rel > 0, nbk, 0) + np.where(n < max_exact, n, large)


def _bias_diags(rel_bias_table):
    delta = np.arange(2 * ATT_KW) - ATT_KW
    buckets = np.stack([_bucket_of(delta * d) for d in DILATIONS])
    inside = np.abs(delta) <= ATT_HALF
    bias = jnp.transpose(rel_bias_table.astype(F32))[:, buckets]
    return jnp.where(inside[None, None], bias, NEG_INF)[:, :, None, :]


def _trunk(x, bias_diags, weights):
    b, seq, dm = x.shape
    cos2, sin2 = _rotary_tables(seq)
    xt = x.reshape(b * seq, dm)
    for layer in weights:
        proj = _norm_matmul(xt, layer["norm_mix_pre"], layer["w_in"], relu2=False, name="in_proj")
        proj = proj.reshape(b, seq, IN_WIDTH)
        mix = _retention(proj, layer["log_decay"], layer["ret_norm_gain"], cos2, sin2)
        mix = _attention(proj, bias_diags, mix)
        xt = _matmul_norm_res(mix.reshape(b * seq, dm), layer["w_out"], xt, layer["norm_mix_post"], name="out_proj")
        u = _norm_matmul(xt, layer["norm_mlp_pre"], layer["w_up"], relu2=True, name="up_proj")
        xt = _matmul_norm_res(u, layer["w_down"], xt, layer["norm_mlp_post"], name="down_proj")
    return xt.reshape(b, seq, dm)


def kernel(x_prompt, x_sample, rel_bias_table, w_in, ret_log_decay, ret_norm_gain, w_out, w_up, w_down,
           norm_mix_pre, norm_mix_post, norm_mlp_pre, norm_mlp_post):
    depth = w_in.shape[0]
    weights = []
    for l in range(depth):
        weights.append({
            "w_in": w_in[l].astype(BF16),
            "w_out": w_out[l].astype(BF16),
            "w_up": w_up[l].astype(BF16),
            "w_down": w_down[l].astype(BF16),
            "log_decay": -jnp.exp(ret_log_decay[l].astype(F32)),
            "ret_norm_gain": ret_norm_gain[l].astype(F32),
            "norm_mix_pre": norm_mix_pre[l].astype(F32),
            "norm_mix_post": norm_mix_post[l].astype(F32),
            "norm_mlp_pre": norm_mlp_pre[l].astype(F32),
            "norm_mlp_post": norm_mlp_post[l].astype(F32),
        })
    bias_diags = _bias_diags(rel_bias_table)
    y_prompt = _trunk(x_prompt, bias_diags, weights)
    y_sample = _trunk(x_sample, bias_diags, weights)
    return (y_prompt, y_sample)
```

```python
import functools
import math

import numpy as np
import jax
import jax.numpy as jnp
from jax import lax
from jax.experimental import pallas as pl
from jax.experimental.pallas import tpu as pltpu

BF16 = jnp.bfloat16
F32 = jnp.float32

D_MODEL = 4096
RET_HEADS = 8
RET_QK_DIM = 128
RET_V_DIM = 256
RET_QK_WIDTH = RET_HEADS * RET_QK_DIM
RET_WIDTH = RET_HEADS * RET_V_DIM
ATT_HEADS = 16
ATT_HEAD_DIM = 128
ATT_WIDTH = ATT_HEADS * ATT_HEAD_DIM
IN_WIDTH = 2 * RET_QK_WIDTH + 2 * RET_WIDTH + 3 * ATT_WIDTH
RET_CHUNK = 128
RET_CHUNK_GROUP = 4
ROPE_BASE = 10000.0
DILATIONS = (1, 4, 16)
ATT_HALF = 64
REL_BUCKETS = 32
REL_MAX_DISTANCE = 1024
NORM_EPS = 1e-6
NEG_INF = -1e30

ATT_QB = 128
ATT_KW = 256
BIAS_OFFSETS = (0, -ATT_HALF, -2 * ATT_HALF)
ATT_TILE_GROUP = 8

V7X_VMEM_LIMIT_BYTES = 56 * 1024 * 1024
LANES = 128


def _params(sem, vmem=V7X_VMEM_LIMIT_BYTES):
    return pltpu.CompilerParams(dimension_semantics=sem, vmem_limit_bytes=vmem)


NORM_ROWS = 16
NORM_SLICES = 8


def _norm_matmul_kernel(xs_ref, g_ref, w_ref, o_ref, h_ref, *, relu2):
    i = pl.program_id(0)
    j = pl.program_id(1)
    slice_rows = xs_ref.shape[0]
    gain = g_ref[...]

    def normalise_slice():
        base = jnp.minimum(j, NORM_SLICES - 1) * slice_rows
        for c in range(slice_rows // NORM_ROWS):
            x = xs_ref[pl.ds(c * NORM_ROWS, NORM_ROWS), :]
            ms = jnp.mean(x * x, axis=-1, keepdims=True)
            rows = pl.ds(pl.multiple_of(base + c * NORM_ROWS, NORM_ROWS), NORM_ROWS)
            h_ref[i % 2, rows, :] = ((x * lax.rsqrt(ms + NORM_EPS)) * gain).astype(BF16)

    @pl.when(i == 0)
    def _():
        normalise_slice()

    @pl.when(i > 0)
    def _():
        normalise_slice()
        y = jnp.dot(h_ref[(i - 1) % 2], w_ref[...], preferred_element_type=F32)
        if relu2:
            y = jnp.maximum(y, 0.0)
            y = y * y
        o_ref[...] = y.astype(o_ref.dtype)


def _norm_matmul(x, gain, w, *, relu2, tm=1024, tn=1024, name):
    t, k = x.shape
    n = w.shape[1]
    ni, nj = t // tm, n // tn
    assert nj >= NORM_SLICES and tm % (NORM_SLICES * NORM_ROWS) == 0
    slice_rows = tm // NORM_SLICES
    return pl.pallas_call(
        functools.partial(_norm_matmul_kernel, relu2=relu2),
        grid=(ni + 1, nj),
        in_specs=[
            pl.BlockSpec((slice_rows, k),
                         lambda i, j: (jnp.minimum(i, ni - 1) * NORM_SLICES + jnp.minimum(j, NORM_SLICES - 1), 0)),
            pl.BlockSpec((1, k), lambda i, j: (0, 0)),
            pl.BlockSpec((k, tn), lambda i, j: (0, jnp.where(i == 0, 0, j))),
        ],
        out_specs=pl.BlockSpec((tm, tn), lambda i, j: (jnp.maximum(i - 1, 0), jnp.where(i == 0, 0, j))),
        out_shape=jax.ShapeDtypeStruct((t, n), BF16),
        scratch_shapes=[pltpu.VMEM((2, tm, k), BF16)],
        compiler_params=_params(("arbitrary", "arbitrary")),
        name=name,
    )(x, gain.reshape(1, k), w)


def _matmul_norm_res_kernel(a_ref, w_ref, xs_ref, g_ref, os_ref, acc_ref):
    i = pl.program_id(0)
    kk = pl.program_id(1)
    ni = pl.num_programs(0) - 1
    slice_rows = os_ref.shape[0]

    @pl.when((i == 0) & (kk == 0))
    def _():
        acc_ref[...] = jnp.zeros_like(acc_ref)

    def accumulate():
        part = jnp.dot(a_ref[...], w_ref[...], preferred_element_type=F32)
        acc_ref[i % 2] = jnp.where(kk == 0, part, acc_ref[i % 2] + part)

    def finish_slice():
        gain = g_ref[...]
        for c in range(slice_rows // NORM_ROWS):
            rows = pl.ds(pl.multiple_of(kk * slice_rows + c * NORM_ROWS, NORM_ROWS), NORM_ROWS)
            y = acc_ref[(i + 1) % 2, rows, :]
            ms = jnp.mean(y * y, axis=-1, keepdims=True)
            local = pl.ds(c * NORM_ROWS, NORM_ROWS)
            os_ref[local, :] = xs_ref[local, :] + (y * lax.rsqrt(ms + NORM_EPS)) * gain

    @pl.when(i == 0)
    def _():
        accumulate()

    @pl.when((i > 0) & (i < ni))
    def _():
        accumulate()
        finish_slice()

    @pl.when(i == ni)
    def _():
        finish_slice()


def _matmul_norm_res(a, w, x, gain, *, tm=512, tk=1024, name):
    t, k = a.shape
    n = w.shape[1]
    ni, nk = t // tm, k // tk
    assert tm % (nk * NORM_ROWS) == 0
    slice_rows = tm // nk

    def slice_index(i, kk):
        return (jnp.maximum(i - 1, 0) * nk + jnp.where(i == 0, 0, kk), 0)

    return pl.pallas_call(
        _matmul_norm_res_kernel,
        grid=(ni + 1, nk),
        in_specs=[
            pl.BlockSpec((tm, tk), lambda i, kk: (jnp.minimum(i, ni - 1), jnp.where(i == ni, nk - 1, kk))),
            pl.BlockSpec((tk, n), lambda i, kk: (jnp.where(i == ni, nk - 1, kk), 0)),
            pl.BlockSpec((slice_rows, n), slice_index),
            pl.BlockSpec((1, n), lambda i, kk: (0, 0)),
        ],
        out_specs=pl.BlockSpec((slice_rows, n), slice_index),
        out_shape=jax.ShapeDtypeStruct((t, n), F32),
        scratch_shapes=[pltpu.VMEM((2, tm, n), F32)],
        compiler_params=_params(("arbitrary", "arbitrary")),
        name=name,
    )(a, w, x, gain.reshape(1, n))


def _retention_kernel(ld_ref, q_ref, k_ref, v_ref, g_ref, cos_ref, sin_ref, gain_ref, o_ref,
                      qd_ref, y_ref, delta_ref, prev_ref, s_ref, kd_ref):
    seq = q_ref.shape[1]
    c = RET_CHUNK
    nc = seq // c
    grp = RET_CHUNK_GROUP
    h = pl.program_id(1)
    ld_f = ld_ref[0, h]
    ld_b = ld_ref[1, h]

    ri = lax.broadcasted_iota(jnp.int32, (c, c), 0)
    ci = lax.broadcasted_iota(jnp.int32, (c, c), 1)
    diff = (ri - ci).astype(F32)
    mask = jnp.where(diff >= 0, jnp.exp(ld_f * jnp.maximum(diff, 0.0)), jnp.exp(ld_b * jnp.maximum(-diff, 0.0)))
    pos = lax.broadcasted_iota(jnp.int32, (c, RET_QK_DIM), 0).astype(F32)
    qdec_f = jnp.exp(ld_f * (pos + 1.0))
    kdec_f = jnp.exp(ld_f * (c - 1.0 - pos))
    qdec_b = jnp.exp(ld_b * (c - pos))
    kdec_b = jnp.exp(ld_b * pos)
    cd_f = jnp.exp(jnp.full((1, RET_V_DIM), ld_f * c, F32))
    cd_b = jnp.exp(jnp.full((1, RET_V_DIM), ld_b * c, F32))
    gain = gain_ref[0]
    k_scale = RET_QK_DIM ** -0.5
    half = RET_QK_DIM // 2

    def chunk_rows(n):
        return pl.ds(pl.multiple_of(n * c, c), c)

    def phase_a(gi, carry):
        chunks = [gi * grp + j for j in range(grp)]
        for j, n in enumerate(chunks):
            rows = chunk_rows(n)
            cos = cos_ref[rows, :]
            sin = sin_ref[rows, :]
            xq = q_ref[0, rows, :].astype(F32)
            xk = k_ref[0, rows, :].astype(F32)
            q = xq * cos + pltpu.roll(xq, half, 1) * sin
            k = (xk * cos + pltpu.roll(xk, half, 1) * sin) * k_scale
            qd_ref[rows, :] = jnp.concatenate([(q * qdec_f).astype(BF16), (q * qdec_b).astype(BF16)], axis=1)
            kd_ref[j] = jnp.concatenate([(k * kdec_f).astype(BF16), (k * kdec_b).astype(BF16)], axis=1)
            s_ref[j] = lax.dot_general(q.astype(BF16), k.astype(BF16), (((1,), (1,)), ((), ())),
                                       preferred_element_type=F32)
        for j, n in enumerate(chunks):
            rows = chunk_rows(n)
            y_ref[rows, :] = jnp.dot((s_ref[j] * mask).astype(BF16), v_ref[0, rows, :],
                                     preferred_element_type=F32)
        for j, n in enumerate(chunks):
            delta_ref[n] = lax.dot_general(kd_ref[j], v_ref[0, chunk_rows(n), :], (((0,), (0,)), ((), ())),
                                           preferred_element_type=F32)
        return carry

    lax.fori_loop(0, nc // grp, phase_a, 0)

    def scan_f(n, st):
        prev_ref[n, pl.ds(0, RET_QK_DIM), :] = st.astype(BF16)
        return st * cd_f + delta_ref[n, pl.ds(0, RET_QK_DIM), :]

    def scan_b(t, st):
        n = nc - 1 - t
        prev_ref[n, pl.ds(RET_QK_DIM, RET_QK_DIM), :] = st.astype(BF16)
        return st * cd_b + delta_ref[n, pl.ds(RET_QK_DIM, RET_QK_DIM), :]

    zero_state = jnp.zeros((RET_QK_DIM, RET_V_DIM), F32)
    lax.fori_loop(0, nc, scan_f, zero_state)
    lax.fori_loop(0, nc, scan_b, zero_state)

    def phase_c(gi, carry):
        chunks = [gi * grp + j for j in range(grp)]
        ys = []
        for n in chunks:
            rows = chunk_rows(n)
            ys.append(y_ref[rows, :] + jnp.dot(qd_ref[rows, :], prev_ref[n], preferred_element_type=F32))
        for n, y in zip(chunks, ys):
            rows = chunk_rows(n)
            ms = jnp.mean(y * y, axis=-1, keepdims=True)
            yn = (y * lax.rsqrt(ms + NORM_EPS)) * gain
            gate = g_ref[0, rows, :].astype(F32)
            o_ref[0, rows, :] = ((gate * jax.nn.sigmoid(gate)) * yn).astype(o_ref.dtype)
        return carry

    lax.fori_loop(0, nc // grp, phase_c, 0)


def _retention(proj, log_decay, gain, cos2, sin2):
    b, seq, _ = proj.shape
    qk_blocks = RET_QK_WIDTH // RET_QK_DIM
    v_block0 = 2 * RET_QK_WIDTH // RET_V_DIM
    g_block0 = v_block0 + RET_HEADS
    nc = seq // RET_CHUNK
    return pl.pallas_call(
        _retention_kernel,
        grid=(b, RET_HEADS),
        in_specs=[
            pl.BlockSpec(memory_space=pltpu.SMEM),
            pl.BlockSpec((1, seq, RET_QK_DIM), lambda i, h: (i, 0, h)),
            pl.BlockSpec((1, seq, RET_QK_DIM), lambda i, h: (i, 0, qk_blocks + h)),
            pl.BlockSpec((1, seq, RET_V_DIM), lambda i, h: (i, 0, v_block0 + h)),
            pl.BlockSpec((1, seq, RET_V_DIM), lambda i, h: (i, 0, g_block0 + h)),
            pl.BlockSpec((seq, RET_QK_DIM), lambda i, h: (0, 0)),
            pl.BlockSpec((seq, RET_QK_DIM), lambda i, h: (0, 0)),
            pl.BlockSpec((1, 1, RET_V_DIM), lambda i, h: (h, 0, 0)),
        ],
        out_specs=pl.BlockSpec((1, seq, RET_V_DIM), lambda i, h: (i, 0, h)),
        out_shape=jax.ShapeDtypeStruct((b, seq, D_MODEL), BF16),
        scratch_shapes=[
            pltpu.VMEM((seq, 2 * RET_QK_DIM), BF16),
            pltpu.VMEM((seq, RET_V_DIM), F32),
            pltpu.VMEM((nc, 2 * RET_QK_DIM, RET_V_DIM), F32),
            pltpu.VMEM((nc, 2 * RET_QK_DIM, RET_V_DIM), BF16),
            pltpu.VMEM((RET_CHUNK_GROUP, RET_CHUNK, RET_CHUNK), F32),
            pltpu.VMEM((RET_CHUNK_GROUP, RET_CHUNK, 2 * RET_QK_DIM), BF16),
        ],
        compiler_params=_params(("parallel", "arbitrary")),
        name="retention",
    )(log_decay, proj, proj, proj, proj, cos2, sin2, gain.reshape(RET_HEADS, 1, RET_V_DIM))


COPY_ROWS = 128


def _attention_kernel(q_ref, k_ref, v_ref, diag_ref, mix_ref, o_ref,
                      t_ref, stage_ref, q1_ref, q4_ref, k4_ref, v4_ref, q16_ref, k16_ref, v16_ref,
                      oa_ref, la_ref, ob_ref, lb_ref, s_ref, p_ref, den_ref, lse_ref):
    del mix_ref
    seq = q_ref.shape[1]
    q_scale = ATT_HEAD_DIM ** -0.5

    @pl.when(pl.program_id(1) == 0)
    def _():
        for bi in range(len(DILATIONS)):
            wide = jnp.broadcast_to(diag_ref[0, bi], (ATT_QB, 2 * ATT_KW))
            for vi, off in enumerate(BIAS_OFFSETS):
                rolled = pltpu.roll(wide, ATT_KW - off, 1, stride=1, stride_axis=0)
                t_ref[bi, vi] = rolled[:, :ATT_KW]

    def deinterleave(dst_ref, d):
        length = seq // d
        ch = min(COPY_ROWS, length)
        for r in range(d):
            for cc in range(length // ch):
                src = stage_ref[pl.ds(r + cc * ch * d, ch, stride=d), :]
                dst_ref[pl.ds(r * length + cc * ch, ch), :] = src.astype(BF16)

    def stage(src_ref, scale):
        def body(cc, carry):
            rows = pl.ds(pl.multiple_of(cc * COPY_ROWS, COPY_ROWS), COPY_ROWS)
            x = src_ref[0, rows, :].astype(F32)
            stage_ref[rows, :] = x * scale if scale is not None else x
            return carry

        lax.fori_loop(0, seq // COPY_ROWS, body, 0)

    stage(q_ref, q_scale)

    def q1_body(cc, carry):
        rows = pl.ds(pl.multiple_of(cc * COPY_ROWS, COPY_ROWS), COPY_ROWS)
        q1_ref[rows, :] = stage_ref[rows, :].astype(BF16)
        return carry

    lax.fori_loop(0, seq // COPY_ROWS, q1_body, 0)
    deinterleave(q4_ref, 4)
    deinterleave(q16_ref, 16)
    stage(k_ref, None)
    deinterleave(k4_ref, 4)
    deinterleave(k16_ref, 16)
    stage(v_ref, None)
    deinterleave(v4_ref, 4)
    deinterleave(v16_ref, 16)

    def run_branch(bi, d, load_q, load_k, load_v, run_o_ref, run_l_ref, first, last):
        length = seq // d
        kw = min(length, ATT_KW)
        nblk = length // ATT_QB

        def tile_rows(t):
            r = t // nblk
            nb = t - r * nblk
            i0 = nb * ATT_QB
            ws = jnp.clip(i0 - ATT_HALF, 0, length - kw)
            var = jnp.where(nb == 0, 0, jnp.where(nb == nblk - 1, 2, 1))
            base = r * length
            qrows = pl.ds(pl.multiple_of(base + i0, ATT_QB), ATT_QB)
            krows = pl.ds(pl.multiple_of(base + ws, ATT_HALF), kw)
            return qrows, krows, var

        def body(g, carry):
            tiles = [tile_rows(g * ATT_TILE_GROUP + j) for j in range(ATT_TILE_GROUP)]
            for j, (qrows, krows, var) in enumerate(tiles):
                s = lax.dot_general(load_q(qrows), load_k(krows), (((1,), (1,)), ((), ())),
                                    preferred_element_type=F32)
                s_ref[j, :, :kw] = s + t_ref[bi, var, :, :kw]
            for j, (qrows, krows, var) in enumerate(tiles):
                s = s_ref[j, :, :kw]
                m = jnp.max(s, axis=-1, keepdims=True)
                p = jnp.exp(s - m)
                den = jnp.sum(p, axis=-1, keepdims=True)
                p_ref[j, :, :kw] = p.astype(BF16)
                den_ref[j] = jnp.broadcast_to(den, (ATT_QB, ATT_HEAD_DIM))
                lse_ref[j] = jnp.broadcast_to(m + jnp.log(den), (ATT_QB, ATT_HEAD_DIM))
            for j, (qrows, krows, var) in enumerate(tiles):
                acc = jnp.dot(p_ref[j, :, :kw], load_v(krows), preferred_element_type=F32)
                o_new = acc / den_ref[j]
                l_new = lse_ref[j]
                if first:
                    run_o_ref[qrows, :] = o_new
                    run_l_ref[qrows, :] = l_new
                    continue
                o_run = run_o_ref[qrows, :]
                l_run = run_l_ref[qrows, :]
                top = jnp.maximum(l_new, l_run)
                e_new = jnp.exp(l_new - top)
                e_run = jnp.exp(l_run - top)
                tot = e_new + e_run
                inv = 1.0 / tot
                o_mix = (e_new * inv) * o_new + (e_run * inv) * o_run
                if last:
                    o_ref[0, qrows, :] = o_mix.astype(o_ref.dtype)
                else:
                    run_o_ref[qrows, :] = o_mix
                    run_l_ref[qrows, :] = top + jnp.log(tot)
            return carry

        lax.fori_loop(0, d * nblk // ATT_TILE_GROUP, body, 0)

    def regroup(src_refs, dst_refs, n_src, n_dst):
        len_src = seq // n_src
        len_dst = seq // n_dst
        ch = min(COPY_ROWS, len_src)
        for r_dst in range(n_dst):
            for r_sub in range(4):
                r_src = r_dst + n_dst * r_sub
                for cc in range(len_src // ch):
                    src = pl.ds(r_src * len_src + cc * ch, ch)
                    dst = pl.ds(r_dst * len_dst + r_sub + 4 * cc * ch, ch, stride=4)
                    for src_ref, dst_ref in zip(src_refs, dst_refs):
                        dst_ref[dst, :] = src_ref[src, :]

    run_branch(2, 16, lambda rows: q16_ref[rows, :], lambda rows: k16_ref[rows, :], lambda rows: v16_ref[rows, :],
               oa_ref, la_ref, first=True, last=False)
    regroup((oa_ref, la_ref), (ob_ref, lb_ref), 16, 4)
    run_branch(1, 4, lambda rows: q4_ref[rows, :], lambda rows: k4_ref[rows, :], lambda rows: v4_ref[rows, :],
               ob_ref, lb_ref, first=False, last=False)
    regroup((ob_ref, lb_ref), (oa_ref, la_ref), 4, 1)
    run_branch(0, 1, lambda rows: q1_ref[rows, :], lambda rows: k_ref[0, rows, :], lambda rows: v_ref[0, rows, :],
               oa_ref, la_ref, first=False, last=True)


def _attention(proj, bias_diags, mix):
    b, seq, _ = proj.shape
    q_block0 = (2 * RET_QK_WIDTH + 2 * RET_WIDTH) // ATT_HEAD_DIM
    k_block0 = q_block0 + ATT_HEADS
    v_block0 = k_block0 + ATT_HEADS
    out_block0 = RET_WIDTH // ATT_HEAD_DIM
    nb, nv = len(DILATIONS), len(BIAS_OFFSETS)
    head_rows = pltpu.VMEM((seq, ATT_HEAD_DIM), BF16)
    head_rows_f32 = pltpu.VMEM((seq, ATT_HEAD_DIM), F32)
    return pl.pallas_call(
        _attention_kernel,
        grid=(ATT_HEADS, b),
        in_specs=[
            pl.BlockSpec((1, seq, ATT_HEAD_DIM), lambda h, i: (i, 0, q_block0 + h)),
            pl.BlockSpec((1, seq, ATT_HEAD_DIM), lambda h, i: (i, 0, k_block0 + h)),
            pl.BlockSpec((1, seq, ATT_HEAD_DIM), lambda h, i: (i, 0, v_block0 + h)),
            pl.BlockSpec((1, nb, 1, 2 * ATT_KW), lambda h, i: (h, 0, 0, 0)),
            pl.BlockSpec(memory_space=pl.ANY),
        ],
        out_specs=pl.BlockSpec((1, seq, ATT_HEAD_DIM), lambda h, i: (i, 0, out_block0 + h)),
        out_shape=jax.ShapeDtypeStruct((b, seq, D_MODEL), BF16),
        scratch_shapes=[
            pltpu.VMEM((nb, nv, ATT_QB, ATT_KW), F32),
            head_rows_f32,
            head_rows,
            head_rows, head_rows, head_rows,
            head_rows, head_rows, head_rows,
            head_rows_f32, head_rows_f32,
            head_rows_f32, head_rows_f32,
            pltpu.VMEM((ATT_TILE_GROUP, ATT_QB, ATT_KW), F32),
            pltpu.VMEM((ATT_TILE_GROUP, ATT_QB, ATT_KW), BF16),
            pltpu.VMEM((ATT_TILE_GROUP, ATT_QB, ATT_HEAD_DIM), F32),
            pltpu.VMEM((ATT_TILE_GROUP, ATT_QB, ATT_HEAD_DIM), F32),
        ],
        input_output_aliases={4: 0},
        compiler_params=_params(("arbitrary", "arbitrary")),
        name="dilated_attention",
    )(proj, proj, proj, bias_diags, mix)


def _rotary_tables(seq):
    half = RET_QK_DIM // 2
    inv = ROPE_BASE ** (-jnp.arange(half, dtype=F32) / half)
    ang = jnp.arange(seq, dtype=F32)[:, None] * inv[None, :]
    cos, sin = jnp.cos(ang), jnp.sin(ang)
    return jnp.concatenate([cos, cos], axis=-1), jnp.concatenate([-sin, sin], axis=-1)


def _bucket_of(rel):
    nbk = REL_BUCKETS // 2
    max_exact = nbk // 2
    n = np.abs(rel)
    nf = np.maximum(n, 1).astype(np.float32)
    large = max_exact + (np.log(nf / max_exact) / math.log(REL_MAX_DISTANCE / max_exact)
                         * (nbk - max_exact)).astype(np.int32)
    large = np.minimum(large, nbk - 1)
    return np.where(rel > 0, nbk, 0) + np.where(n < max_exact, n, large)


def _bias_diags(rel_bias_table):
    delta = np.arange(2 * ATT_KW) - ATT_KW
    buckets = np.stack([_bucket_of(delta * d) for d in DILATIONS])
    inside = np.abs(delta) <= ATT_HALF
    bias = jnp.transpose(rel_bias_table.astype(F32))[:, buckets]
    return jnp.where(inside[None, None], bias, NEG_INF)[:, :, None, :]


def _trunk(x, bias_diags, weights):
    b, seq, dm = x.shape
    cos2, sin2 = _rotary_tables(seq)
    xt = x.reshape(b * seq, dm)
    for layer in weights:
        proj = _norm_matmul(xt, layer["norm_mix_pre"], layer["w_in"], relu2=False, name="in_proj")
        proj = proj.reshape(b, seq, IN_WIDTH)
        mix = _retention(proj, layer["log_decay"], layer["ret_norm_gain"], cos2, sin2)
        mix = _attention(proj, bias_diags, mix)
        xt = _matmul_norm_res(mix.reshape(b * seq, dm), layer["w_out"], xt, layer["norm_mix_post"], name="out_proj")
        u = _norm_matmul(xt, layer["norm_mlp_pre"], layer["w_up"], relu2=True, name="up_proj")
        xt = _matmul_norm_res(u, layer["w_down"], xt, layer["norm_mlp_post"], name="down_proj")
    return xt.reshape(b, seq, dm)


def kernel(x_prompt, x_sample, rel_bias_table, w_in, ret_log_decay, ret_norm_gain, w_out, w_up, w_down,
           norm_mix_pre, norm_mix_post, norm_mlp_pre, norm_mlp_post):
    depth = w_in.shape[0]
    weights = []
    for l in range(depth):
        weights.append({
            "w_in": w_in[l].astype(BF16),
            "w_out": w_out[l].astype(BF16),
            "w_up": w_up[l].astype(BF16),
            "w_down": w_down[l].astype(BF16),
            "log_decay": -jnp.exp(ret_log_decay[l].astype(F32)),
            "ret_norm_gain": ret_norm_gain[l].astype(F32),
            "norm_mix_pre": norm_mix_pre[l].astype(F32),
            "norm_mix_post": norm_mix_post[l].astype(F32),
            "norm_mlp_pre": norm_mlp_pre[l].astype(F32),
            "norm_mlp_post": norm_mlp_post[l].astype(F32),
        })
    bias_diags = _bias_diags(rel_bias_table)
    y_prompt = _trunk(x_prompt, bias_diags, weights)
    y_sample = _trunk(x_sample, bias_diags, weights)
    return (y_prompt, y_sample)
```

```python
import functools
import math

import numpy as np
import jax
import jax.numpy as jnp
from jax import lax
from jax.experimental import pallas as pl
from jax.experimental.pallas import tpu as pltpu

BF16 = jnp.bfloat16
F32 = jnp.float32

D_MODEL = 4096
RET_HEADS = 8
RET_QK_DIM = 128
RET_V_DIM = 256
RET_QK_WIDTH = RET_HEADS * RET_QK_DIM
RET_WIDTH = RET_HEADS * RET_V_DIM
ATT_HEADS = 16
ATT_HEAD_DIM = 128
ATT_WIDTH = ATT_HEADS * ATT_HEAD_DIM
IN_WIDTH = 2 * RET_QK_WIDTH + 2 * RET_WIDTH + 3 * ATT_WIDTH
RET_CHUNK = 128
RET_CHUNK_GROUP = 4
ROPE_BASE = 10000.0
DILATIONS = (1, 4, 16)
ATT_HALF = 64
REL_BUCKETS = 32
REL_MAX_DISTANCE = 1024
NORM_EPS = 1e-6
NEG_INF = -1e30

ATT_QB = 128
ATT_KW = 256
BIAS_OFFSETS = (0, -ATT_HALF, -2 * ATT_HALF)
ATT_TILE_GROUP = 8

V7X_VMEM_LIMIT_BYTES = 56 * 1024 * 1024


def _params(sem, vmem=V7X_VMEM_LIMIT_BYTES):
    return pltpu.CompilerParams(dimension_semantics=sem, vmem_limit_bytes=vmem)


NORM_ROWS = 16
NORM_SLICES = 8


def _norm_matmul_kernel(xs_ref, g_ref, w_ref, o_ref, h_even_ref, h_odd_ref, *, relu2):
    i = pl.program_id(0)
    j = pl.program_id(1)
    slice_rows = xs_ref.shape[0]

    def normalise_slice(h_ref):
        gain = g_ref[...]
        base = jnp.minimum(j, NORM_SLICES - 1) * slice_rows
        for c in range(slice_rows // NORM_ROWS):
            x = xs_ref[pl.ds(c * NORM_ROWS, NORM_ROWS), :]
            ms = jnp.mean(x * x, axis=-1, keepdims=True)
            rows = pl.ds(pl.multiple_of(base + c * NORM_ROWS, NORM_ROWS), NORM_ROWS)
            h_ref[rows, :] = ((x * lax.rsqrt(ms + NORM_EPS)) * gain).astype(BF16)

    def multiply(h_ref):
        y = jnp.dot(h_ref[...], w_ref[...], preferred_element_type=F32)
        if relu2:
            y = jnp.maximum(y, 0.0)
            y = y * y
        o_ref[...] = y.astype(o_ref.dtype)

    @pl.when(i == 0)
    def _():
        normalise_slice(h_even_ref)

    @pl.when((i > 0) & (i % 2 == 1))
    def _():
        normalise_slice(h_odd_ref)
        multiply(h_even_ref)

    @pl.when((i > 0) & (i % 2 == 0))
    def _():
        normalise_slice(h_even_ref)
        multiply(h_odd_ref)


def _norm_matmul(x, gain, w, *, relu2, tm=1024, tn=1024, name):
    t, k = x.shape
    n = w.shape[1]
    ni, nj = t // tm, n // tn
    assert nj >= NORM_SLICES and tm % (NORM_SLICES * NORM_ROWS) == 0
    slice_rows = tm // NORM_SLICES
    return pl.pallas_call(
        functools.partial(_norm_matmul_kernel, relu2=relu2),
        grid=(ni + 1, nj),
        in_specs=[
            pl.BlockSpec((slice_rows, k),
                         lambda i, j: (jnp.minimum(i, ni - 1) * NORM_SLICES + jnp.minimum(j, NORM_SLICES - 1), 0)),
            pl.BlockSpec((1, k), lambda i, j: (0, 0)),
            pl.BlockSpec((k, tn), lambda i, j: (0, jnp.where(i == 0, 0, j))),
        ],
        out_specs=pl.BlockSpec((tm, tn), lambda i, j: (jnp.maximum(i - 1, 0), jnp.where(i == 0, 0, j))),
        out_shape=jax.ShapeDtypeStruct((t, n), BF16),
        scratch_shapes=[pltpu.VMEM((tm, k), BF16), pltpu.VMEM((tm, k), BF16)],
        compiler_params=_params(("arbitrary", "arbitrary")),
        name=name,
    )(x, gain.reshape(1, k), w)


def _matmul_norm_res_kernel(a_ref, w_ref, xs_ref, g_ref, os_ref, acc_even_ref, acc_odd_ref, *, ni):
    i = pl.program_id(0)
    kk = pl.program_id(1)
    slice_rows = os_ref.shape[0]

    @pl.when((i == 0) & (kk == 0))
    def _():
        acc_even_ref[...] = jnp.zeros_like(acc_even_ref)
        acc_odd_ref[...] = jnp.zeros_like(acc_odd_ref)

    def accumulate(acc_ref):
        part = jnp.dot(a_ref[...], w_ref[...], preferred_element_type=F32)
        acc_ref[...] = jnp.where(kk == 0, part, acc_ref[...] + part)

    def finish_slice(acc_ref):
        gain = g_ref[...]
        for c in range(slice_rows // NORM_ROWS):
            rows = pl.ds(pl.multiple_of(kk * slice_rows + c * NORM_ROWS, NORM_ROWS), NORM_ROWS)
            y = acc_ref[rows, :]
            ms = jnp.mean(y * y, axis=-1, keepdims=True)
            local = pl.ds(c * NORM_ROWS, NORM_ROWS)
            os_ref[local, :] = xs_ref[local, :] + (y * lax.rsqrt(ms + NORM_EPS)) * gain

    @pl.when(i == 0)
    def _():
        accumulate(acc_even_ref)

    @pl.when((i > 0) & (i < ni) & (i % 2 == 1))
    def _():
        accumulate(acc_odd_ref)
        finish_slice(acc_even_ref)

    @pl.when((i > 0) & (i < ni) & (i % 2 == 0))
    def _():
        accumulate(acc_even_ref)
        finish_slice(acc_odd_ref)

    @pl.when(i == ni)
    def _():
        finish_slice(acc_odd_ref if (ni - 1) % 2 == 1 else acc_even_ref)


def _matmul_norm_res(a, w, x, gain, *, tm=512, tk=1024, name):
    t, k = a.shape
    n = w.shape[1]
    ni, nk = t // tm, k // tk
    assert tm % (nk * NORM_ROWS) == 0
    slice_rows = tm // nk

    def slice_index(i, kk):
        return (jnp.maximum(i - 1, 0) * nk + jnp.where(i == 0, 0, kk), 0)

    return pl.pallas_call(
        functools.partial(_matmul_norm_res_kernel, ni=ni),
        grid=(ni + 1, nk),
        in_specs=[
            pl.BlockSpec((tm, tk), lambda i, kk: (jnp.minimum(i, ni - 1), jnp.where(i == ni, nk - 1, kk))),
            pl.BlockSpec((tk, n), lambda i, kk: (jnp.where(i == ni, nk - 1, kk), 0)),
            pl.BlockSpec((slice_rows, n), slice_index),
            pl.BlockSpec((1, n), lambda i, kk: (0, 0)),
        ],
        out_specs=pl.BlockSpec((slice_rows, n), slice_index),
        out_shape=jax.ShapeDtypeStruct((t, n), F32),
        scratch_shapes=[pltpu.VMEM((tm, n), F32), pltpu.VMEM((tm, n), F32)],
        compiler_params=_params(("arbitrary", "arbitrary")),
        name=name,
    )(a, w, x, gain.reshape(1, n))


def _retention_kernel(ld_ref, q_ref, k_ref, v_ref, g_ref, cos_ref, sin_ref, gain_ref, o_ref,
                      qd_ref, y_ref, delta_ref, prev_ref, s_ref, kd_ref):
    seq = q_ref.shape[1]
    c = RET_CHUNK
    nc = seq // c
    grp = RET_CHUNK_GROUP
    h = pl.program_id(1)
    ld_f = ld_ref[0, h]
    ld_b = ld_ref[1, h]

    ri = lax.broadcasted_iota(jnp.int32, (c, c), 0)
    ci = lax.broadcasted_iota(jnp.int32, (c, c), 1)
    diff = (ri - ci).astype(F32)
    mask = jnp.where(diff >= 0, jnp.exp(ld_f * jnp.maximum(diff, 0.0)), jnp.exp(ld_b * jnp.maximum(-diff, 0.0)))
    pos = lax.broadcasted_iota(jnp.int32, (c, RET_QK_DIM), 0).astype(F32)
    qdec_f = jnp.exp(ld_f * (pos + 1.0))
    kdec_f = jnp.exp(ld_f * (c - 1.0 - pos))
    qdec_b = jnp.exp(ld_b * (c - pos))
    kdec_b = jnp.exp(ld_b * pos)
    cd_f = jnp.exp(jnp.full((1, RET_V_DIM), ld_f * c, F32))
    cd_b = jnp.exp(jnp.full((1, RET_V_DIM), ld_b * c, F32))
    gain = gain_ref[0]
    k_scale = RET_QK_DIM ** -0.5
    half = RET_QK_DIM // 2

    def chunk_rows(n):
        return pl.ds(pl.multiple_of(n * c, c), c)

    def phase_a(gi, carry):
        chunks = [gi * grp + j for j in range(grp)]
        for j, n in enumerate(chunks):
            rows = chunk_rows(n)
            cos = cos_ref[rows, :]
            sin = sin_ref[rows, :]
            xq = q_ref[0, rows, :].astype(F32)
            xk = k_ref[0, rows, :].astype(F32)
            q = xq * cos + pltpu.roll(xq, half, 1) * sin
            k = (xk * cos + pltpu.roll(xk, half, 1) * sin) * k_scale
            qd_ref[rows, :] = jnp.concatenate([(q * qdec_f).astype(BF16), (q * qdec_b).astype(BF16)], axis=1)
            kd_ref[j] = jnp.concatenate([(k * kdec_f).astype(BF16), (k * kdec_b).astype(BF16)], axis=1)
            s_ref[j] = lax.dot_general(q.astype(BF16), k.astype(BF16), (((1,), (1,)), ((), ())),
                                       preferred_element_type=F32)
        for j, n in enumerate(chunks):
            rows = chunk_rows(n)
            y_ref[rows, :] = jnp.dot((s_ref[j] * mask).astype(BF16), v_ref[0, rows, :],
                                     preferred_element_type=F32)
        for j, n in enumerate(chunks):
            delta_ref[n] = lax.dot_general(kd_ref[j], v_ref[0, chunk_rows(n), :], (((0,), (0,)), ((), ())),
                                           preferred_element_type=F32)
        return carry

    lax.fori_loop(0, nc // grp, phase_a, 0)

    def scan_f(n, st):
        prev_ref[n, pl.ds(0, RET_QK_DIM), :] = st.astype(BF16)
        return st * cd_f + delta_ref[n, pl.ds(0, RET_QK_DIM), :]

    def scan_b(t, st):
        n = nc - 1 - t
        prev_ref[n, pl.ds(RET_QK_DIM, RET_QK_DIM), :] = st.astype(BF16)
        return st * cd_b + delta_ref[n, pl.ds(RET_QK_DIM, RET_QK_DIM), :]

    zero_state = jnp.zeros((RET_QK_DIM, RET_V_DIM), F32)
    lax.fori_loop(0, nc, scan_f, zero_state)
    lax.fori_loop(0, nc, scan_b, zero_state)

    def phase_c(gi, carry):
        chunks = [gi * grp + j for j in range(grp)]
        ys = []
        for n in chunks:
            rows = chunk_rows(n)
            ys.append(y_ref[rows, :] + jnp.dot(qd_ref[rows, :], prev_ref[n], preferred_element_type=F32))
        for n, y in zip(chunks, ys):
            rows = chunk_rows(n)
            ms = jnp.mean(y * y, axis=-1, keepdims=True)
            yn = (y * lax.rsqrt(ms + NORM_EPS)) * gain
            gate = g_ref[0, rows, :].astype(F32)
            o_ref[0, rows, :] = ((gate * jax.nn.sigmoid(gate)) * yn).astype(o_ref.dtype)
        return carry

    lax.fori_loop(0, nc // grp, phase_c, 0)


def _retention(proj, log_decay, gain, cos2, sin2):
    b, seq, _ = proj.shape
    qk_blocks = RET_QK_WIDTH // RET_QK_DIM
    v_block0 = 2 * RET_QK_WIDTH // RET_V_DIM
    g_block0 = v_block0 + RET_HEADS
    nc = seq // RET_CHUNK
    return pl.pallas_call(
        _retention_kernel,
        grid=(b, RET_HEADS),
        in_specs=[
            pl.BlockSpec(memory_space=pltpu.SMEM),
            pl.BlockSpec((1, seq, RET_QK_DIM), lambda i, h: (i, 0, h)),
            pl.BlockSpec((1, seq, RET_QK_DIM), lambda i, h: (i, 0, qk_blocks + h)),
            pl.BlockSpec((1, seq, RET_V_DIM), lambda i, h: (i, 0, v_block0 + h)),
            pl.BlockSpec((1, seq, RET_V_DIM), lambda i, h: (i, 0, g_block0 + h)),
            pl.BlockSpec((seq, RET_QK_DIM), lambda i, h: (0, 0)),
            pl.BlockSpec((seq, RET_QK_DIM), lambda i, h: (0, 0)),
            pl.BlockSpec((1, 1, RET_V_DIM), lambda i, h: (h, 0, 0)),
        ],
        out_specs=pl.BlockSpec((1, seq, RET_V_DIM), lambda i, h: (i, 0, h)),
        out_shape=jax.ShapeDtypeStruct((b, seq, D_MODEL), BF16),
        scratch_shapes=[
            pltpu.VMEM((seq, 2 * RET_QK_DIM), BF16),
            pltpu.VMEM((seq, RET_V_DIM), F32),
            pltpu.VMEM((nc, 2 * RET_QK_DIM, RET_V_DIM), F32),
            pltpu.VMEM((nc, 2 * RET_QK_DIM, RET_V_DIM), BF16),
            pltpu.VMEM((RET_CHUNK_GROUP, RET_CHUNK, RET_CHUNK), F32),
            pltpu.VMEM((RET_CHUNK_GROUP, RET_CHUNK, 2 * RET_QK_DIM), BF16),
        ],
        compiler_params=_params(("parallel", "arbitrary")),
        name="retention",
    )(log_decay, proj, proj, proj, proj, cos2, sin2, gain.reshape(RET_HEADS, 1, RET_V_DIM))


COPY_ROWS = 128


def _attention_kernel(q_ref, k_ref, v_ref, diag_ref, mix_ref, o_ref,
                      t_ref, stage_ref, q1_ref, q4_ref, k4_ref, v4_ref, q16_ref, k16_ref, v16_ref,
                      oa_ref, la_ref, ob_ref, lb_ref, s_ref, p_ref, den_ref, lse_ref):
    del mix_ref
    seq = q_ref.shape[1]
    q_scale = ATT_HEAD_DIM ** -0.5

    @pl.when(pl.program_id(1) == 0)
    def _():
        for bi in range(len(DILATIONS)):
            wide = jnp.broadcast_to(diag_ref[0, bi], (ATT_QB, 2 * ATT_KW))
            for vi, off in enumerate(BIAS_OFFSETS):
                rolled = pltpu.roll(wide, ATT_KW - off, 1, stride=1, stride_axis=0)
                t_ref[bi, vi] = rolled[:, :ATT_KW]

    def deinterleave(dst_ref, d):
        length = seq // d
        ch = min(COPY_ROWS, length)
        for r in range(d):
            for cc in range(length // ch):
                src = stage_ref[pl.ds(r + cc * ch * d, ch, stride=d), :]
                dst_ref[pl.ds(r * length + cc * ch, ch), :] = src.astype(BF16)

    def stage(src_ref, scale):
        def body(cc, carry):
            rows = pl.ds(pl.multiple_of(cc * COPY_ROWS, COPY_ROWS), COPY_ROWS)
            x = src_ref[0, rows, :].astype(F32)
            stage_ref[rows, :] = x * scale if scale is not None else x
            return carry

        lax.fori_loop(0, seq // COPY_ROWS, body, 0)

    stage(q_ref, q_scale)

    def q1_body(cc, carry):
        rows = pl.ds(pl.multiple_of(cc * COPY_ROWS, COPY_ROWS), COPY_ROWS)
        q1_ref[rows, :] = stage_ref[rows, :].astype(BF16)
        return carry

    lax.fori_loop(0, seq // COPY_ROWS, q1_body, 0)
    deinterleave(q4_ref, 4)
    deinterleave(q16_ref, 16)
    stage(k_ref, None)
    deinterleave(k4_ref, 4)
    deinterleave(k16_ref, 16)
    stage(v_ref, None)
    deinterleave(v4_ref, 4)
    deinterleave(v16_ref, 16)

    def run_branch(bi, d, load_q, load_k, load_v, run_o_ref, run_l_ref, first, last):
        length = seq // d
        kw = min(length, ATT_KW)
        nblk = length // ATT_QB

        def tile_rows(t):
            r = t // nblk
            nb = t - r * nblk
            i0 = nb * ATT_QB
            ws = jnp.clip(i0 - ATT_HALF, 0, length - kw)
            var = jnp.where(nb == 0, 0, jnp.where(nb == nblk - 1, 2, 1))
            base = r * length
            qrows = pl.ds(pl.multiple_of(base + i0, ATT_QB), ATT_QB)
            krows = pl.ds(pl.multiple_of(base + ws, ATT_HALF), kw)
            return qrows, krows, var

        def body(g, carry):
            tiles = [tile_rows(g * ATT_TILE_GROUP + j) for j in range(ATT_TILE_GROUP)]
            for j, (qrows, krows, var) in enumerate(tiles):
                s = lax.dot_general(load_q(qrows), load_k(krows), (((1,), (1,)), ((), ())),
                                    preferred_element_type=F32)
                s_ref[j, :, :kw] = s + t_ref[bi, var, :, :kw]
            for j, (qrows, krows, var) in enumerate(tiles):
                s = s_ref[j, :, :kw]
                m = jnp.max(s, axis=-1, keepdims=True)
                p = jnp.exp(s - m)
                den = jnp.sum(p, axis=-1, keepdims=True)
                p_ref[j, :, :kw] = p.astype(BF16)
                den_ref[j] = jnp.broadcast_to(den, (ATT_QB, ATT_HEAD_DIM))
                lse_ref[j] = jnp.broadcast_to(m + jnp.log(den), (ATT_QB, ATT_HEAD_DIM))
            for j, (qrows, krows, var) in enumerate(tiles):
                acc = jnp.dot(p_ref[j, :, :kw], load_v(krows), preferred_element_type=F32)
                o_new = acc / den_ref[j]
                l_new = lse_ref[j]
                if first:
                    run_o_ref[qrows, :] = o_new
                    run_l_ref[qrows, :] = l_new
                    continue
                o_run = run_o_ref[qrows, :]
                l_run = run_l_ref[qrows, :]
                top = jnp.maximum(l_new, l_run)
                e_new = jnp.exp(l_new - top)
                e_run = jnp.exp(l_run - top)
                tot = e_new + e_run
                inv = 1.0 / tot
                o_mix = (e_new * inv) * o_new + (e_run * inv) * o_run
                if last:
                    o_ref[0, qrows, :] = o_mix.astype(o_ref.dtype)
                else:
                    run_o_ref[qrows, :] = o_mix
                    run_l_ref[qrows, :] = top + jnp.log(tot)
            return carry

        lax.fori_loop(0, d * nblk // ATT_TILE_GROUP, body, 0)

    def regroup(src_refs, dst_refs, n_src, n_dst):
        len_src = seq // n_src
        len_dst = seq // n_dst
        ch = min(COPY_ROWS, len_src)
        for r_dst in range(n_dst):
            for r_sub in range(4):
                r_src = r_dst + n_dst * r_sub
                for cc in range(len_src // ch):
                    src = pl.ds(r_src * len_src + cc * ch, ch)
                    dst = pl.ds(r_dst * len_dst + r_sub + 4 * cc * ch, ch, stride=4)
                    for src_ref, dst_ref in zip(src_refs, dst_refs):
                        dst_ref[dst, :] = src_ref[src, :]

    run_branch(2, 16, lambda rows: q16_ref[rows, :], lambda rows: k16_ref[rows, :], lambda rows: v16_ref[rows, :],
               oa_ref, la_ref, first=True, last=False)
    regroup((oa_ref, la_ref), (ob_ref, lb_ref), 16, 4)
    run_branch(1, 4, lambda rows: q4_ref[rows, :], lambda rows: k4_ref[rows, :], lambda rows: v4_ref[rows, :],
               ob_ref, lb_ref, first=False, last=False)
    regroup((ob_ref, lb_ref), (oa_ref, la_ref), 4, 1)
    run_branch(0, 1, lambda rows: q1_ref[rows, :], lambda rows: k_ref[0, rows, :], lambda rows: v_ref[0, rows, :],
               oa_ref, la_ref, first=False, last=True)


def _attention(proj, bias_diags, mix):
    b, seq, _ = proj.shape
    q_block0 = (2 * RET_QK_WIDTH + 2 * RET_WIDTH) // ATT_HEAD_DIM
    k_block0 = q_block0 + ATT_HEADS
    v_block0 = k_block0 + ATT_HEADS
    out_block0 = RET_WIDTH // ATT_HEAD_DIM
    nb, nv = len(DILATIONS), len(BIAS_OFFSETS)
    head_rows = pltpu.VMEM((seq, ATT_HEAD_DIM), BF16)
    head_rows_f32 = pltpu.VMEM((seq, ATT_HEAD_DIM), F32)
    return pl.pallas_call(
        _attention_kernel,
        grid=(ATT_HEADS, b),
        in_specs=[
            pl.BlockSpec((1, seq, ATT_HEAD_DIM), lambda h, i: (i, 0, q_block0 + h)),
            pl.BlockSpec((1, seq, ATT_HEAD_DIM), lambda h, i: (i, 0, k_block0 + h)),
            pl.BlockSpec((1, seq, ATT_HEAD_DIM), lambda h, i: (i, 0, v_block0 + h)),
            pl.BlockSpec((1, nb, 1, 2 * ATT_KW), lambda h, i: (h, 0, 0, 0)),
            pl.BlockSpec(memory_space=pl.ANY),
        ],
        out_specs=pl.BlockSpec((1, seq, ATT_HEAD_DIM), lambda h, i: (i, 0, out_block0 + h)),
        out_shape=jax.ShapeDtypeStruct((b, seq, D_MODEL), BF16),
        scratch_shapes=[
            pltpu.VMEM((nb, nv, ATT_QB, ATT_KW), F32),
            head_rows_f32,
            head_rows,
            head_rows, head_rows, head_rows,
            head_rows, head_rows, head_rows,
            head_rows_f32, head_rows_f32,
            head_rows_f32, head_rows_f32,
            pltpu.VMEM((ATT_TILE_GROUP, ATT_QB, ATT_KW), F32),
            pltpu.VMEM((ATT_TILE_GROUP, ATT_QB, ATT_KW), BF16),
            pltpu.VMEM((ATT_TILE_GROUP, ATT_QB, ATT_HEAD_DIM), F32),
            pltpu.VMEM((ATT_TILE_GROUP, ATT_QB, ATT_HEAD_DIM), F32),
        ],
        input_output_aliases={4: 0},
        compiler_params=_params(("arbitrary", "arbitrary")),
        name="dilated_attention",
    )(proj, proj, proj, bias_diags, mix)


def _rotary_tables(seq):
    half = RET_QK_DIM // 2
    inv = ROPE_BASE ** (-jnp.arange(half, dtype=F32) / half)
    ang = jnp.arange(seq, dtype=F32)[:, None] * inv[None, :]
    cos, sin = jnp.cos(ang), jnp.sin(ang)
    return jnp.concatenate([cos, cos], axis=-1), jnp.concatenate([-sin, sin], axis=-1)


def _bucket_of(rel):
    nbk = REL_BUCKETS // 2
    max_exact = nbk // 2
    n = np.abs(rel)
    nf = np.maximum(n, 1).astype(np.float32)
    large = max_exact + (np.log(nf / max_exact) / math.log(REL_MAX_DISTANCE / max_exact)
                         * (nbk - max_exact)).astype(np.int32)
    large = np.minimum(large, nbk - 1)
    return np.where(rel > 0, nbk, 0) + np.where(n < max_exact, n, large)


def _bias_diags(rel_bias_table):
    delta = np.arange(2 * ATT_KW) - ATT_KW
    buckets = np.stack([_bucket_of(delta * d) for d in DILATIONS])
    inside = np.abs(delta) <= ATT_HALF
    bias = jnp.transpose(rel_bias_table.astype(F32))[:, buckets]
    return jnp.where(inside[None, None], bias, NEG_INF)[:, :, None, :]


def _trunk(x, bias_diags, weights):
    b, seq, dm = x.shape
    cos2, sin2 = _rotary_tables(seq)
    xt = x.reshape(b * seq, dm)
    for layer in weights:
        proj = _norm_matmul(xt, layer["norm_mix_pre"], layer["w_in"], relu2=False, name="in_proj")
        proj = proj.reshape(b, seq, IN_WIDTH)
        mix = _retention(proj, layer["log_decay"], layer["ret_norm_gain"], cos2, sin2)
        mix = _attention(proj, bias_diags, mix)
        xt = _matmul_norm_res(mix.reshape(b * seq, dm), layer["w_out"], xt, layer["norm_mix_post"], name="out_proj")
        u = _norm_matmul(xt, layer["norm_mlp_pre"], layer["w_up"], relu2=True, name="up_proj")
        xt = _matmul_norm_res(u, layer["w_down"], xt, layer["norm_mlp_post"], name="down_proj")
    return xt.reshape(b, seq, dm)


def kernel(x_prompt, x_sample, rel_bias_table, w_in, ret_log_decay, ret_norm_gain, w_out, w_up, w_down,
           norm_mix_pre, norm_mix_post, norm_mlp_pre, norm_mlp_post):
    depth = w_in.shape[0]
    weights = []
    for l in range(depth):
        weights.append({
            "w_in": w_in[l].astype(BF16),
            "w_out": w_out[l].astype(BF16),
            "w_up": w_up[l].astype(BF16),
            "w_down": w_down[l].astype(BF16),
            "log_decay": -jnp.exp(ret_log_decay[l].astype(F32)),
            "ret_norm_gain": ret_norm_gain[l].astype(F32),
            "norm_mix_pre": norm_mix_pre[l].astype(F32),
            "norm_mix_post": norm_mix_post[l].astype(F32),
            "norm_mlp_pre": norm_mlp_pre[l].astype(F32),
            "norm_mlp_post": norm_mlp_post[l].astype(F32),
        })
    bias_diags = _bias_diags(rel_bias_table)
    y_prompt = _trunk(x_prompt, bias_diags, weights)
    y_sample = _trunk(x_sample, bias_diags, weights)
    return (y_prompt, y_sample)
```

```python
import functools
import math

import numpy as np
import jax
import jax.numpy as jnp
from jax import lax
from jax.experimental import pallas as pl
from jax.experimental.pallas import tpu as pltpu

BF16 = jnp.bfloat16
F32 = jnp.float32

D_MODEL = 4096
RET_HEADS = 8
RET_QK_DIM = 128
RET_V_DIM = 256
RET_QK_WIDTH = RET_HEADS * RET_QK_DIM
RET_WIDTH = RET_HEADS * RET_V_DIM
ATT_HEADS = 16
ATT_HEAD_DIM = 128
ATT_WIDTH = ATT_HEADS * ATT_HEAD_DIM
IN_WIDTH = 2 * RET_QK_WIDTH + 2 * RET_WIDTH + 3 * ATT_WIDTH
RET_CHUNK = 128
RET_CHUNK_GROUP = 4
ROPE_BASE = 10000.0
DILATIONS = (1, 4, 16)
ATT_HALF = 64
REL_BUCKETS = 32
REL_MAX_DISTANCE = 1024
NORM_EPS = 1e-6
NEG_INF = -1e30

ATT_QB = 128
ATT_KW = 256
BIAS_OFFSETS = (0, -ATT_HALF, -2 * ATT_HALF)
ATT_TILE_GROUP = 8

V7X_VMEM_LIMIT_BYTES = 56 * 1024 * 1024


def _params(sem, vmem=V7X_VMEM_LIMIT_BYTES):
    return pltpu.CompilerParams(dimension_semantics=sem, vmem_limit_bytes=vmem)


NORM_ROWS = 16
NORM_SLICES = 8


def _norm_matmul_kernel(xs_ref, g_ref, w_ref, o_ref, h_even_ref, h_odd_ref, *, relu2):
    i = pl.program_id(0)
    j = pl.program_id(1)
    slice_rows = xs_ref.shape[0]

    def normalise_slice(h_ref):
        gain = g_ref[...]
        base = jnp.minimum(j, NORM_SLICES - 1) * slice_rows
        for c in range(slice_rows // NORM_ROWS):
            x = xs_ref[pl.ds(c * NORM_ROWS, NORM_ROWS), :]
            ms = jnp.mean(x * x, axis=-1, keepdims=True)
            rows = pl.ds(pl.multiple_of(base + c * NORM_ROWS, NORM_ROWS), NORM_ROWS)
            h_ref[rows, :] = ((x * lax.rsqrt(ms + NORM_EPS)) * gain).astype(BF16)

    def multiply(h_ref):
        y = jnp.dot(h_ref[...], w_ref[...], preferred_element_type=F32)
        if relu2:
            y = jnp.maximum(y, 0.0)
            y = y * y
        o_ref[...] = y.astype(o_ref.dtype)

    @pl.when(i == 0)
    def _():
        normalise_slice(h_even_ref)

    @pl.when((i > 0) & (i % 2 == 1))
    def _():
        normalise_slice(h_odd_ref)
        multiply(h_even_ref)

    @pl.when((i > 0) & (i % 2 == 0))
    def _():
        normalise_slice(h_even_ref)
        multiply(h_odd_ref)


def _norm_matmul(x, gain, w, *, relu2, tm=1024, tn=1024, name):
    t, k = x.shape
    n = w.shape[1]
    ni, nj = t // tm, n // tn
    assert nj >= NORM_SLICES and tm % (NORM_SLICES * NORM_ROWS) == 0
    slice_rows = tm // NORM_SLICES
    return pl.pallas_call(
        functools.partial(_norm_matmul_kernel, relu2=relu2),
        grid=(ni + 1, nj),
        in_specs=[
            pl.BlockSpec((slice_rows, k),
                         lambda i, j: (jnp.minimum(i, ni - 1) * NORM_SLICES + jnp.minimum(j, NORM_SLICES - 1), 0)),
            pl.BlockSpec((1, k), lambda i, j: (0, 0)),
            pl.BlockSpec((k, tn), lambda i, j: (0, jnp.where(i == 0, 0, j))),
        ],
        out_specs=pl.BlockSpec((tm, tn), lambda i, j: (jnp.maximum(i - 1, 0), jnp.where(i == 0, 0, j))),
        out_shape=jax.ShapeDtypeStruct((t, n), BF16),
        scratch_shapes=[pltpu.VMEM((tm, k), BF16), pltpu.VMEM((tm, k), BF16)],
        compiler_params=_params(("arbitrary", "arbitrary")),
        name=name,
    )(x, gain.reshape(1, k), w)


def _matmul_norm_res_kernel(a_ref, w_ref, xs_ref, g_ref, os_ref, acc_even_ref, acc_odd_ref, *, ni):
    i = pl.program_id(0)
    kk = pl.program_id(1)
    slice_rows = os_ref.shape[0]

    @pl.when((i == 0) & (kk == 0))
    def _():
        acc_even_ref[...] = jnp.zeros_like(acc_even_ref)
        acc_odd_ref[...] = jnp.zeros_like(acc_odd_ref)

    def accumulate(acc_ref):
        part = jnp.dot(a_ref[...], w_ref[...], preferred_element_type=F32)
        acc_ref[...] = jnp.where(kk == 0, part, acc_ref[...] + part)

    def finish_slice(acc_ref):
        gain = g_ref[...]
        for c in range(slice_rows // NORM_ROWS):
            rows = pl.ds(pl.multiple_of(kk * slice_rows + c * NORM_ROWS, NORM_ROWS), NORM_ROWS)
            y = acc_ref[rows, :]
            ms = jnp.mean(y * y, axis=-1, keepdims=True)
            local = pl.ds(c * NORM_ROWS, NORM_ROWS)
            os_ref[local, :] = xs_ref[local, :] + (y * lax.rsqrt(ms + NORM_EPS)) * gain

    @pl.when(i == 0)
    def _():
        accumulate(acc_even_ref)

    @pl.when((i > 0) & (i < ni) & (i % 2 == 1))
    def _():
        accumulate(acc_odd_ref)
        finish_slice(acc_even_ref)

    @pl.when((i > 0) & (i < ni) & (i % 2 == 0))
    def _():
        accumulate(acc_even_ref)
        finish_slice(acc_odd_ref)

    @pl.when(i == ni)
    def _():
        finish_slice(acc_odd_ref if (ni - 1) % 2 == 1 else acc_even_ref)


def _matmul_norm_res(a, w, x, gain, *, tm=1024, tk=512, name):
    t, k = a.shape
    n = w.shape[1]
    ni, nk = t // tm, k // tk
    assert tm % (nk * NORM_ROWS) == 0
    slice_rows = tm // nk

    def slice_index(i, kk):
        return (jnp.maximum(i - 1, 0) * nk + jnp.where(i == 0, 0, kk), 0)

    return pl.pallas_call(
        functools.partial(_matmul_norm_res_kernel, ni=ni),
        grid=(ni + 1, nk),
        in_specs=[
            pl.BlockSpec((tm, tk), lambda i, kk: (jnp.minimum(i, ni - 1), jnp.where(i == ni, nk - 1, kk))),
            pl.BlockSpec((tk, n), lambda i, kk: (jnp.where(i == ni, nk - 1, kk), 0)),
            pl.BlockSpec((slice_rows, n), slice_index),
            pl.BlockSpec((1, n), lambda i, kk: (0, 0)),
        ],
        out_specs=pl.BlockSpec((slice_rows, n), slice_index),
        out_shape=jax.ShapeDtypeStruct((t, n), F32),
        scratch_shapes=[pltpu.VMEM((tm, n), F32), pltpu.VMEM((tm, n), F32)],
        compiler_params=_params(("arbitrary", "arbitrary")),
        name=name,
    )(a, w, x, gain.reshape(1, n))


def _retention_kernel(ld_ref, q_ref, k_ref, v_ref, g_ref, cos_ref, sin_ref, gain_ref, o_ref,
                      qd_ref, y_ref, delta_ref, prev_ref, s_ref, kd_ref):
    seq = q_ref.shape[1]
    c = RET_CHUNK
    nc = seq // c
    grp = RET_CHUNK_GROUP
    h = pl.program_id(1)
    ld_f = ld_ref[0, h]
    ld_b = ld_ref[1, h]

    ri = lax.broadcasted_iota(jnp.int32, (c, c), 0)
    ci = lax.broadcasted_iota(jnp.int32, (c, c), 1)
    diff = (ri - ci).astype(F32)
    mask = jnp.where(diff >= 0, jnp.exp(ld_f * jnp.maximum(diff, 0.0)), jnp.exp(ld_b * jnp.maximum(-diff, 0.0)))
    pos = lax.broadcasted_iota(jnp.int32, (c, RET_QK_DIM), 0).astype(F32)
    qdec_f = jnp.exp(ld_f * (pos + 1.0))
    kdec_f = jnp.exp(ld_f * (c - 1.0 - pos))
    qdec_b = jnp.exp(ld_b * (c - pos))
    kdec_b = jnp.exp(ld_b * pos)
    cd_f = jnp.exp(jnp.full((1, RET_V_DIM), ld_f * c, F32))
    cd_b = jnp.exp(jnp.full((1, RET_V_DIM), ld_b * c, F32))
    gain = gain_ref[0]
    k_scale = RET_QK_DIM ** -0.5
    half = RET_QK_DIM // 2

    def chunk_rows(n):
        return pl.ds(pl.multiple_of(n * c, c), c)

    def phase_a(gi, carry):
        chunks = [gi * grp + j for j in range(grp)]
        for j, n in enumerate(chunks):
            rows = chunk_rows(n)
            cos = cos_ref[rows, :]
            sin = sin_ref[rows, :]
            xq = q_ref[0, rows, :].astype(F32)
            xk = k_ref[0, rows, :].astype(F32)
            q = xq * cos + pltpu.roll(xq, half, 1) * sin
            k = (xk * cos + pltpu.roll(xk, half, 1) * sin) * k_scale
            qd_ref[rows, :] = jnp.concatenate([(q * qdec_f).astype(BF16), (q * qdec_b).astype(BF16)], axis=1)
            kd_ref[j] = jnp.concatenate([(k * kdec_f).astype(BF16), (k * kdec_b).astype(BF16)], axis=1)
            s_ref[j] = lax.dot_general(q.astype(BF16), k.astype(BF16), (((1,), (1,)), ((), ())),
                                       preferred_element_type=F32)
        for j, n in enumerate(chunks):
            rows = chunk_rows(n)
            y_ref[rows, :] = jnp.dot((s_ref[j] * mask).astype(BF16), v_ref[0, rows, :],
                                     preferred_element_type=F32)
        for j, n in enumerate(chunks):
            delta_ref[n] = lax.dot_general(kd_ref[j], v_ref[0, chunk_rows(n), :], (((0,), (0,)), ((), ())),
                                           preferred_element_type=F32)
        return carry

    lax.fori_loop(0, nc // grp, phase_a, 0)

    def scan_f(n, st):
        prev_ref[n, pl.ds(0, RET_QK_DIM), :] = st.astype(BF16)
        return st * cd_f + delta_ref[n, pl.ds(0, RET_QK_DIM), :]

    def scan_b(t, st):
        n = nc - 1 - t
        prev_ref[n, pl.ds(RET_QK_DIM, RET_QK_DIM), :] = st.astype(BF16)
        return st * cd_b + delta_ref[n, pl.ds(RET_QK_DIM, RET_QK_DIM), :]

    zero_state = jnp.zeros((RET_QK_DIM, RET_V_DIM), F32)
    lax.fori_loop(0, nc, scan_f, zero_state)
    lax.fori_loop(0, nc, scan_b, zero_state)

    def phase_c(gi, carry):
        chunks = [gi * grp + j for j in range(grp)]
        ys = []
        for n in chunks:
            rows = chunk_rows(n)
            ys.append(y_ref[rows, :] + jnp.dot(qd_ref[rows, :], prev_ref[n], preferred_element_type=F32))
        for n, y in zip(chunks, ys):
            rows = chunk_rows(n)
            ms = jnp.mean(y * y, axis=-1, keepdims=True)
            yn = (y * lax.rsqrt(ms + NORM_EPS)) * gain
            gate = g_ref[0, rows, :].astype(F32)
            o_ref[0, rows, :] = ((gate * jax.nn.sigmoid(gate)) * yn).astype(o_ref.dtype)
        return carry

    lax.fori_loop(0, nc // grp, phase_c, 0)


def _retention(proj, log_decay, gain, cos2, sin2):
    b, seq, _ = proj.shape
    qk_blocks = RET_QK_WIDTH // RET_QK_DIM
    v_block0 = 2 * RET_QK_WIDTH // RET_V_DIM
    g_block0 = v_block0 + RET_HEADS
    nc = seq // RET_CHUNK
    return pl.pallas_call(
        _retention_kernel,
        grid=(b, RET_HEADS),
        in_specs=[
            pl.BlockSpec(memory_space=pltpu.SMEM),
            pl.BlockSpec((1, seq, RET_QK_DIM), lambda i, h: (i, 0, h)),
            pl.BlockSpec((1, seq, RET_QK_DIM), lambda i, h: (i, 0, qk_blocks + h)),
            pl.BlockSpec((1, seq, RET_V_DIM), lambda i, h: (i, 0, v_block0 + h)),
            pl.BlockSpec((1, seq, RET_V_DIM), lambda i, h: (i, 0, g_block0 + h)),
            pl.BlockSpec((seq, RET_QK_DIM), lambda i, h: (0, 0)),
            pl.BlockSpec((seq, RET_QK_DIM), lambda i, h: (0, 0)),
            pl.BlockSpec((1, 1, RET_V_DIM), lambda i, h: (h, 0, 0)),
        ],
        out_specs=pl.BlockSpec((1, seq, RET_V_DIM), lambda i, h: (i, 0, h)),
        out_shape=jax.ShapeDtypeStruct((b, seq, D_MODEL), BF16),
        scratch_shapes=[
            pltpu.VMEM((seq, 2 * RET_QK_DIM), BF16),
            pltpu.VMEM((seq, RET_V_DIM), F32),
            pltpu.VMEM((nc, 2 * RET_QK_DIM, RET_V_DIM), F32),
            pltpu.VMEM((nc, 2 * RET_QK_DIM, RET_V_DIM), BF16),
            pltpu.VMEM((RET_CHUNK_GROUP, RET_CHUNK, RET_CHUNK), F32),
            pltpu.VMEM((RET_CHUNK_GROUP, RET_CHUNK, 2 * RET_QK_DIM), BF16),
        ],
        compiler_params=_params(("parallel", "arbitrary")),
        name="retention",
    )(log_decay, proj, proj, proj, proj, cos2, sin2, gain.reshape(RET_HEADS, 1, RET_V_DIM))


COPY_ROWS = 128


def _attention_kernel(q_ref, k_ref, v_ref, diag_ref, mix_ref, o_ref,
                      t_ref, stage_ref, q1_ref, q4_ref, k4_ref, v4_ref, q16_ref, k16_ref, v16_ref,
                      oa_ref, la_ref, ob_ref, lb_ref, s_ref, p_ref, den_ref, lse_ref):
    del mix_ref
    seq = q_ref.shape[1]
    q_scale = ATT_HEAD_DIM ** -0.5

    @pl.when(pl.program_id(1) == 0)
    def _():
        for bi in range(len(DILATIONS)):
            wide = jnp.broadcast_to(diag_ref[0, bi], (ATT_QB, 2 * ATT_KW))
            for vi, off in enumerate(BIAS_OFFSETS):
                rolled = pltpu.roll(wide, ATT_KW - off, 1, stride=1, stride_axis=0)
                t_ref[bi, vi] = rolled[:, :ATT_KW]

    def deinterleave(dst_ref, d):
        length = seq // d
        ch = min(COPY_ROWS, length)
        for r in range(d):
            for cc in range(length // ch):
                src = stage_ref[pl.ds(r + cc * ch * d, ch, stride=d), :]
                dst_ref[pl.ds(r * length + cc * ch, ch), :] = src.astype(BF16)

    def stage(src_ref, scale):
        def body(cc, carry):
            rows = pl.ds(pl.multiple_of(cc * COPY_ROWS, COPY_ROWS), COPY_ROWS)
            x = src_ref[0, rows, :].astype(F32)
            stage_ref[rows, :] = x * scale if scale is not None else x
            return carry

        lax.fori_loop(0, seq // COPY_ROWS, body, 0)

    stage(q_ref, q_scale)

    def q1_body(cc, carry):
        rows = pl.ds(pl.multiple_of(cc * COPY_ROWS, COPY_ROWS), COPY_ROWS)
        q1_ref[rows, :] = stage_ref[rows, :].astype(BF16)
        return carry

    lax.fori_loop(0, seq // COPY_ROWS, q1_body, 0)
    deinterleave(q4_ref, 4)
    deinterleave(q16_ref, 16)
    stage(k_ref, None)
    deinterleave(k4_ref, 4)
    deinterleave(k16_ref, 16)
    stage(v_ref, None)
    deinterleave(v4_ref, 4)
    deinterleave(v16_ref, 16)

    def run_branch(bi, d, load_q, load_k, load_v, run_o_ref, run_l_ref, first, last):
        length = seq // d
        kw = min(length, ATT_KW)
        nblk = length // ATT_QB

        def tile_rows(t):
            r = t // nblk
            nb = t - r * nblk
            i0 = nb * ATT_QB
            ws = jnp.clip(i0 - ATT_HALF, 0, length - kw)
            var = jnp.where(nb == 0, 0, jnp.where(nb == nblk - 1, 2, 1))
            base = r * length
            qrows = pl.ds(pl.multiple_of(base + i0, ATT_QB), ATT_QB)
            krows = pl.ds(pl.multiple_of(base + ws, ATT_HALF), kw)
            return qrows, krows, var

        def body(g, carry):
            tiles = [tile_rows(g * ATT_TILE_GROUP + j) for j in range(ATT_TILE_GROUP)]
            for j, (qrows, krows, var) in enumerate(tiles):
                s = lax.dot_general(load_q(qrows), load_k(krows), (((1,), (1,)), ((), ())),
                                    preferred_element_type=F32)
                s_ref[j, :, :kw] = s + t_ref[bi, var, :, :kw]
            for j, (qrows, krows, var) in enumerate(tiles):
                s = s_ref[j, :, :kw]
                m = jnp.max(s, axis=-1, keepdims=True)
                p = jnp.exp(s - m)
                den = jnp.sum(p, axis=-1, keepdims=True)
                p_ref[j, :, :kw] = p.astype(BF16)
                den_ref[j] = jnp.broadcast_to(den, (ATT_QB, ATT_HEAD_DIM))
                lse_ref[j] = jnp.broadcast_to(m + jnp.log(den), (ATT_QB, ATT_HEAD_DIM))
            for j, (qrows, krows, var) in enumerate(tiles):
                acc = jnp.dot(p_ref[j, :, :kw], load_v(krows), preferred_element_type=F32)
                o_new = acc / den_ref[j]
                l_new = lse_ref[j]
                if first:
                    run_o_ref[qrows, :] = o_new
                    run_l_ref[qrows, :] = l_new
                    continue
                o_run = run_o_ref[qrows, :]
                l_run = run_l_ref[qrows, :]
                top = jnp.maximum(l_new, l_run)
                e_new = jnp.exp(l_new - top)
                e_run = jnp.exp(l_run - top)
                tot = e_new + e_run
                inv = 1.0 / tot
                o_mix = (e_new * inv) * o_new + (e_run * inv) * o_run
                if last:
                    o_ref[0, qrows, :] = o_mix.astype(o_ref.dtype)
                else:
                    run_o_ref[qrows, :] = o_mix
                    run_l_ref[qrows, :] = top + jnp.log(tot)
            return carry

        lax.fori_loop(0, d * nblk // ATT_TILE_GROUP, body, 0)

    def regroup(src_refs, dst_refs, n_src, n_dst):
        len_src = seq // n_src
        len_dst = seq // n_dst
        ch = min(COPY_ROWS, len_src)
        for r_dst in range(n_dst):
            for r_sub in range(4):
                r_src = r_dst + n_dst * r_sub
                for cc in range(len_src // ch):
                    src = pl.ds(r_src * len_src + cc * ch, ch)
                    dst = pl.ds(r_dst * len_dst + r_sub + 4 * cc * ch, ch, stride=4)
                    for src_ref, dst_ref in zip(src_refs, dst_refs):
                        dst_ref[dst, :] = src_ref[src, :]

    run_branch(2, 16, lambda rows: q16_ref[rows, :], lambda rows: k16_ref[rows, :], lambda rows: v16_ref[rows, :],
               oa_ref, la_ref, first=True, last=False)
    regroup((oa_ref, la_ref), (ob_ref, lb_ref), 16, 4)
    run_branch(1, 4, lambda rows: q4_ref[rows, :], lambda rows: k4_ref[rows, :], lambda rows: v4_ref[rows, :],
               ob_ref, lb_ref, first=False, last=False)
    regroup((ob_ref, lb_ref), (oa_ref, la_ref), 4, 1)
    run_branch(0, 1, lambda rows: q1_ref[rows, :], lambda rows: k_ref[0, rows, :], lambda rows: v_ref[0, rows, :],
               oa_ref, la_ref, first=False, last=True)


def _attention(proj, bias_diags, mix):
    b, seq, _ = proj.shape
    q_block0 = (2 * RET_QK_WIDTH + 2 * RET_WIDTH) // ATT_HEAD_DIM
    k_block0 = q_block0 + ATT_HEADS
    v_block0 = k_block0 + ATT_HEADS
    out_block0 = RET_WIDTH // ATT_HEAD_DIM
    nb, nv = len(DILATIONS), len(BIAS_OFFSETS)
    head_rows = pltpu.VMEM((seq, ATT_HEAD_DIM), BF16)
    head_rows_f32 = pltpu.VMEM((seq, ATT_HEAD_DIM), F32)
    return pl.pallas_call(
        _attention_kernel,
        grid=(ATT_HEADS, b),
        in_specs=[
            pl.BlockSpec((1, seq, ATT_HEAD_DIM), lambda h, i: (i, 0, q_block0 + h)),
            pl.BlockSpec((1, seq, ATT_HEAD_DIM), lambda h, i: (i, 0, k_block0 + h)),
            pl.BlockSpec((1, seq, ATT_HEAD_DIM), lambda h, i: (i, 0, v_block0 + h)),
            pl.BlockSpec((1, nb, 1, 2 * ATT_KW), lambda h, i: (h, 0, 0, 0)),
            pl.BlockSpec(memory_space=pl.ANY),
        ],
        out_specs=pl.BlockSpec((1, seq, ATT_HEAD_DIM), lambda h, i: (i, 0, out_block0 + h)),
        out_shape=jax.ShapeDtypeStruct((b, seq, D_MODEL), BF16),
        scratch_shapes=[
            pltpu.VMEM((nb, nv, ATT_QB, ATT_KW), F32),
            head_rows_f32,
            head_rows,
            head_rows, head_rows, head_rows,
            head_rows, head_rows, head_rows,
            head_rows_f32, head_rows_f32,
            head_rows_f32, head_rows_f32,
            pltpu.VMEM((ATT_TILE_GROUP, ATT_QB, ATT_KW), F32),
            pltpu.VMEM((ATT_TILE_GROUP, ATT_QB, ATT_KW), BF16),
            pltpu.VMEM((ATT_TILE_GROUP, ATT_QB, ATT_HEAD_DIM), F32),
            pltpu.VMEM((ATT_TILE_GROUP, ATT_QB, ATT_HEAD_DIM), F32),
        ],
        input_output_aliases={4: 0},
        compiler_params=_params(("arbitrary", "arbitrary")),
        name="dilated_attention",
    )(proj, proj, proj, bias_diags, mix)


def _rotary_tables(seq):
    half = RET_QK_DIM // 2
    inv = ROPE_BASE ** (-jnp.arange(half, dtype=F32) / half)
    ang = jnp.arange(seq, dtype=F32)[:, None] * inv[None, :]
    cos, sin = jnp.cos(ang), jnp.sin(ang)
    return jnp.concatenate([cos, cos], axis=-1), jnp.concatenate([-sin, sin], axis=-1)


def _bucket_of(rel):
    nbk = REL_BUCKETS // 2
    max_exact = nbk // 2
    n = np.abs(rel)
    nf = np.maximum(n, 1).astype(np.float32)
    large = max_exact + (np.log(nf / max_exact) / math.log(REL_MAX_DISTANCE / max_exact)
                         * (nbk - max_exact)).astype(np.int32)
    large = np.minimum(large, nbk - 1)
    return np.where(rel > 0, nbk, 0) + np.where(n < max_exact, n, large)


def _bias_diags(rel_bias_table):
    delta = np.arange(2 * ATT_KW) - ATT_KW
    buckets = np.stack([_bucket_of(delta * d) for d in DILATIONS])
    inside = np.abs(delta) <= ATT_HALF
    bias = jnp.transpose(rel_bias_table.astype(F32))[:, buckets]
    return jnp.where(inside[None, None], bias, NEG_INF)[:, :, None, :]


def _trunk(x, bias_diags, weights):
    b, seq, dm = x.shape
    cos2, sin2 = _rotary_tables(seq)
    xt = x.reshape(b * seq, dm)
    for layer in weights:
        proj = _norm_matmul(xt, layer["norm_mix_pre"], layer["w_in"], relu2=False, name="in_proj")
        proj = proj.reshape(b, seq, IN_WIDTH)
        mix = _retention(proj, layer["log_decay"], layer["ret_norm_gain"], cos2, sin2)
        mix = _attention(proj, bias_diags, mix)
        xt = _matmul_norm_res(mix.reshape(b * seq, dm), layer["w_out"], xt, layer["norm_mix_post"], name="out_proj")
        u = _norm_matmul(xt, layer["norm_mlp_pre"], layer["w_up"], relu2=True, name="up_proj")
        xt = _matmul_norm_res(u, layer["w_down"], xt, layer["norm_mlp_post"], name="down_proj")
    return xt.reshape(b, seq, dm)


def kernel(x_prompt, x_sample, rel_bias_table, w_in, ret_log_decay, ret_norm_gain, w_out, w_up, w_down,
           norm_mix_pre, norm_mix_post, norm_mlp_pre, norm_mlp_post):
    depth = w_in.shape[0]
    weights = []
    for l in range(depth):
        weights.append({
            "w_in": w_in[l].astype(BF16),
            "w_out": w_out[l].astype(BF16),
            "w_up": w_up[l].astype(BF16),
            "w_down": w_down[l].astype(BF16),
            "log_decay": -jnp.exp(ret_log_decay[l].astype(F32)),
            "ret_norm_gain": ret_norm_gain[l].astype(F32),
            "norm_mix_pre": norm_mix_pre[l].astype(F32),
            "norm_mix_post": norm_mix_post[l].astype(F32),
            "norm_mlp_pre": norm_mlp_pre[l].astype(F32),
            "norm_mlp_post": norm_mlp_post[l].astype(F32),
        })
    bias_diags = _bias_diags(rel_bias_table)
    y_prompt = _trunk(x_prompt, bias_diags, weights)
    y_sample = _trunk(x_sample, bias_diags, weights)
    return (y_prompt, y_sample)
```

```python
import functools
import math

import numpy as np
import jax
import jax.numpy as jnp
from jax import lax
from jax.experimental import pallas as pl
from jax.experimental.pallas import tpu as pltpu

BF16 = jnp.bfloat16
F32 = jnp.float32

D_MODEL = 4096
RET_HEADS = 8
RET_QK_DIM = 128
RET_V_DIM = 256
RET_QK_WIDTH = RET_HEADS * RET_QK_DIM
RET_WIDTH = RET_HEADS * RET_V_DIM
ATT_HEADS = 16
ATT_HEAD_DIM = 128
ATT_WIDTH = ATT_HEADS * ATT_HEAD_DIM
IN_WIDTH = 2 * RET_QK_WIDTH + 2 * RET_WIDTH + 3 * ATT_WIDTH
RET_CHUNK = 128
RET_CHUNK_GROUP = 4
ROPE_BASE = 10000.0
DILATIONS = (1, 4, 16)
ATT_HALF = 64
REL_BUCKETS = 32
REL_MAX_DISTANCE = 1024
NORM_EPS = 1e-6
NEG_INF = -1e30

ATT_QB = 128
ATT_KW = 256
BIAS_OFFSETS = (0, -ATT_HALF, -2 * ATT_HALF)
ATT_TILE_GROUP = 8

V7X_VMEM_LIMIT_BYTES = 56 * 1024 * 1024


def _params(sem, vmem=V7X_VMEM_LIMIT_BYTES):
    return pltpu.CompilerParams(dimension_semantics=sem, vmem_limit_bytes=vmem)


NORM_ROWS = 16
NORM_SLICES = 8


def _norm_matmul_kernel(xs_ref, g_ref, w_ref, o_ref, h_even_ref, h_odd_ref, *, relu2):
    i = pl.program_id(0)
    j = pl.program_id(1)
    slice_rows = xs_ref.shape[0]

    def normalise_slice(h_ref):
        gain = g_ref[...]
        base = jnp.minimum(j, NORM_SLICES - 1) * slice_rows
        for c in range(slice_rows // NORM_ROWS):
            x = xs_ref[pl.ds(c * NORM_ROWS, NORM_ROWS), :]
            ms = jnp.mean(x * x, axis=-1, keepdims=True)
            rows = pl.ds(pl.multiple_of(base + c * NORM_ROWS, NORM_ROWS), NORM_ROWS)
            h_ref[rows, :] = ((x * lax.rsqrt(ms + NORM_EPS)) * gain).astype(BF16)

    def multiply(h_ref):
        y = jnp.dot(h_ref[...], w_ref[...], preferred_element_type=F32)
        if relu2:
            y = jnp.maximum(y, 0.0)
            y = y * y
        o_ref[...] = y.astype(o_ref.dtype)

    @pl.when(i == 0)
    def _():
        normalise_slice(h_even_ref)

    @pl.when((i > 0) & (i % 2 == 1))
    def _():
        normalise_slice(h_odd_ref)
        multiply(h_even_ref)

    @pl.when((i > 0) & (i % 2 == 0))
    def _():
        normalise_slice(h_even_ref)
        multiply(h_odd_ref)


def _norm_matmul(x, gain, w, *, relu2, tm=1024, tn=1024, name):
    t, k = x.shape
    n = w.shape[1]
    ni, nj = t // tm, n // tn
    assert nj >= NORM_SLICES and tm % (NORM_SLICES * NORM_ROWS) == 0
    slice_rows = tm // NORM_SLICES
    return pl.pallas_call(
        functools.partial(_norm_matmul_kernel, relu2=relu2),
        grid=(ni + 1, nj),
        in_specs=[
            pl.BlockSpec((slice_rows, k),
                         lambda i, j: (jnp.minimum(i, ni - 1) * NORM_SLICES + jnp.minimum(j, NORM_SLICES - 1), 0)),
            pl.BlockSpec((1, k), lambda i, j: (0, 0)),
            pl.BlockSpec((k, tn), lambda i, j: (0, jnp.where(i == 0, 0, j))),
        ],
        out_specs=pl.BlockSpec((tm, tn), lambda i, j: (jnp.maximum(i - 1, 0), jnp.where(i == 0, 0, j))),
        out_shape=jax.ShapeDtypeStruct((t, n), BF16),
        scratch_shapes=[pltpu.VMEM((tm, k), BF16), pltpu.VMEM((tm, k), BF16)],
        compiler_params=_params(("arbitrary", "arbitrary")),
        name=name,
    )(x, gain.reshape(1, k), w)


def _matmul_norm_res_kernel(a_ref, w_ref, xs_ref, g_ref, os_ref, acc_even_ref, acc_odd_ref, *, ni):
    i = pl.program_id(0)
    kk = pl.program_id(1)
    slice_rows = os_ref.shape[0]

    @pl.when((i == 0) & (kk == 0))
    def _():
        acc_even_ref[...] = jnp.zeros_like(acc_even_ref)
        acc_odd_ref[...] = jnp.zeros_like(acc_odd_ref)

    def accumulate(acc_ref):
        part = jnp.dot(a_ref[...], w_ref[...], preferred_element_type=F32)
        acc_ref[...] = jnp.where(kk == 0, part, acc_ref[...] + part)

    def finish_slice(acc_ref):
        gain = g_ref[...]
        for c in range(slice_rows // NORM_ROWS):
            rows = pl.ds(pl.multiple_of(kk * slice_rows + c * NORM_ROWS, NORM_ROWS), NORM_ROWS)
            y = acc_ref[rows, :]
            ms = jnp.mean(y * y, axis=-1, keepdims=True)
            local = pl.ds(c * NORM_ROWS, NORM_ROWS)
            os_ref[local, :] = xs_ref[local, :] + (y * lax.rsqrt(ms + NORM_EPS)) * gain

    @pl.when(i == 0)
    def _():
        accumulate(acc_even_ref)

    @pl.when((i > 0) & (i < ni) & (i % 2 == 1))
    def _():
        accumulate(acc_odd_ref)
        finish_slice(acc_even_ref)

    @pl.when((i > 0) & (i < ni) & (i % 2 == 0))
    def _():
        accumulate(acc_even_ref)
        finish_slice(acc_odd_ref)

    @pl.when(i == ni)
    def _():
        finish_slice(acc_odd_ref if (ni - 1) % 2 == 1 else acc_even_ref)


def _matmul_norm_res(a, w, x, gain, *, tm=512, tk=1024, name):
    t, k = a.shape
    n = w.shape[1]
    ni, nk = t // tm, k // tk
    assert tm % (nk * NORM_ROWS) == 0
    slice_rows = tm // nk

    def slice_index(i, kk):
        return (jnp.maximum(i - 1, 0) * nk + jnp.where(i == 0, 0, kk), 0)

    return pl.pallas_call(
        functools.partial(_matmul_norm_res_kernel, ni=ni),
        grid=(ni + 1, nk),
        in_specs=[
            pl.BlockSpec((tm, tk), lambda i, kk: (jnp.minimum(i, ni - 1), jnp.where(i == ni, nk - 1, kk))),
            pl.BlockSpec((tk, n), lambda i, kk: (jnp.where(i == ni, nk - 1, kk), 0)),
            pl.BlockSpec((slice_rows, n), slice_index),
            pl.BlockSpec((1, n), lambda i, kk: (0, 0)),
        ],
        out_specs=pl.BlockSpec((slice_rows, n), slice_index),
        out_shape=jax.ShapeDtypeStruct((t, n), F32),
        scratch_shapes=[pltpu.VMEM((tm, n), F32), pltpu.VMEM((tm, n), F32)],
        compiler_params=_params(("arbitrary", "arbitrary")),
        name=name,
    )(a, w, x, gain.reshape(1, n))


def _retention_kernel(ld_ref, q_ref, k_ref, v_ref, g_ref, cos_ref, sin_ref, gain_ref, o_ref,
                      qd_ref, y_ref, delta_ref, prev_ref, s_ref, kd_ref):
    seq = q_ref.shape[1]
    c = RET_CHUNK
    nc = seq // c
    grp = RET_CHUNK_GROUP
    h = pl.program_id(1)
    ld_f = ld_ref[0, h]
    ld_b = ld_ref[1, h]

    ri = lax.broadcasted_iota(jnp.int32, (c, c), 0)
    ci = lax.broadcasted_iota(jnp.int32, (c, c), 1)
    diff = (ri - ci).astype(F32)
    mask = jnp.where(diff >= 0, jnp.exp(ld_f * jnp.maximum(diff, 0.0)), jnp.exp(ld_b * jnp.maximum(-diff, 0.0)))
    pos = lax.broadcasted_iota(jnp.int32, (c, RET_QK_DIM), 0).astype(F32)
    qdec_f = jnp.exp(ld_f * (pos + 1.0))
    kdec_f = jnp.exp(ld_f * (c - 1.0 - pos))
    qdec_b = jnp.exp(ld_b * (c - pos))
    kdec_b = jnp.exp(ld_b * pos)
    cd_f = jnp.exp(jnp.full((1, RET_V_DIM), ld_f * c, F32))
    cd_b = jnp.exp(jnp.full((1, RET_V_DIM), ld_b * c, F32))
    gain = gain_ref[0]
    k_scale = RET_QK_DIM ** -0.5
    half = RET_QK_DIM // 2

    def chunk_rows(n):
        return pl.ds(pl.multiple_of(n * c, c), c)

    def phase_a(gi, carry):
        chunks = [gi * grp + j for j in range(grp)]
        for j, n in enumerate(chunks):
            rows = chunk_rows(n)
            cos = cos_ref[rows, :]
            sin = sin_ref[rows, :]
            xq = q_ref[0, rows, :].astype(F32)
            xk = k_ref[0, rows, :].astype(F32)
            q = xq * cos + pltpu.roll(xq, half, 1) * sin
            k = (xk * cos + pltpu.roll(xk, half, 1) * sin) * k_scale
            qd_ref[rows, :] = jnp.concatenate([(q * qdec_f).astype(BF16), (q * qdec_b).astype(BF16)], axis=1)
            kd_ref[j] = jnp.concatenate([(k * kdec_f).astype(BF16), (k * kdec_b).astype(BF16)], axis=1)
            s_ref[j] = lax.dot_general(q.astype(BF16), k.astype(BF16), (((1,), (1,)), ((), ())),
                                       preferred_element_type=F32)
        for j, n in enumerate(chunks):
            rows = chunk_rows(n)
            y_ref[rows, :] = jnp.dot((s_ref[j] * mask).astype(BF16), v_ref[0, rows, :],
                                     preferred_element_type=F32)
        for j, n in enumerate(chunks):
            delta_ref[n] = lax.dot_general(kd_ref[j], v_ref[0, chunk_rows(n), :], (((0,), (0,)), ((), ())),
                                           preferred_element_type=F32)
        return carry

    lax.fori_loop(0, nc // grp, phase_a, 0)

    def scan_f(n, st):
        prev_ref[n, pl.ds(0, RET_QK_DIM), :] = st.astype(BF16)
        return st * cd_f + delta_ref[n, pl.ds(0, RET_QK_DIM), :]

    def scan_b(t, st):
        n = nc - 1 - t
        prev_ref[n, pl.ds(RET_QK_DIM, RET_QK_DIM), :] = st.astype(BF16)
        return st * cd_b + delta_ref[n, pl.ds(RET_QK_DIM, RET_QK_DIM), :]

    zero_state = jnp.zeros((RET_QK_DIM, RET_V_DIM), F32)
    lax.fori_loop(0, nc, scan_f, zero_state)
    lax.fori_loop(0, nc, scan_b, zero_state)

    def phase_c(gi, carry):
        chunks = [gi * grp + j for j in range(grp)]
        ys = []
        for n in chunks:
            rows = chunk_rows(n)
            ys.append(y_ref[rows, :] + jnp.dot(qd_ref[rows, :], prev_ref[n], preferred_element_type=F32))
        for n, y in zip(chunks, ys):
            rows = chunk_rows(n)
            ms = jnp.mean(y * y, axis=-1, keepdims=True)
            yn = (y * lax.rsqrt(ms + NORM_EPS)) * gain
            gate = g_ref[0, rows, :].astype(F32)
            o_ref[0, rows, :] = ((gate * jax.nn.sigmoid(gate)) * yn).astype(o_ref.dtype)
        return carry

    lax.fori_loop(0, nc // grp, phase_c, 0)


def _retention(proj, log_decay, gain, cos2, sin2):
    b, seq, _ = proj.shape
    qk_blocks = RET_QK_WIDTH // RET_QK_DIM
    v_block0 = 2 * RET_QK_WIDTH // RET_V_DIM
    g_block0 = v_block0 + RET_HEADS
    nc = seq // RET_CHUNK
    return pl.pallas_call(
        _retention_kernel,
        grid=(b, RET_HEADS),
        in_specs=[
            pl.BlockSpec(memory_space=pltpu.SMEM),
            pl.BlockSpec((1, seq, RET_QK_DIM), lambda i, h: (i, 0, h)),
            pl.BlockSpec((1, seq, RET_QK_DIM), lambda i, h: (i, 0, qk_blocks + h)),
            pl.BlockSpec((1, seq, RET_V_DIM), lambda i, h: (i, 0, v_block0 + h)),
            pl.BlockSpec((1, seq, RET_V_DIM), lambda i, h: (i, 0, g_block0 + h)),
            pl.BlockSpec((seq, RET_QK_DIM), lambda i, h: (0, 0)),
            pl.BlockSpec((seq, RET_QK_DIM), lambda i, h: (0, 0)),
            pl.BlockSpec((1, 1, RET_V_DIM), lambda i, h: (h, 0, 0)),
        ],
        out_specs=pl.BlockSpec((1, seq, RET_V_DIM), lambda i, h: (i, 0, h)),
        out_shape=jax.ShapeDtypeStruct((b, seq, D_MODEL), BF16),
        scratch_shapes=[
            pltpu.VMEM((seq, 2 * RET_QK_DIM), BF16),
            pltpu.VMEM((seq, RET_V_DIM), F32),
            pltpu.VMEM((nc, 2 * RET_QK_DIM, RET_V_DIM), F32),
            pltpu.VMEM((nc, 2 * RET_QK_DIM, RET_V_DIM), BF16),
            pltpu.VMEM((RET_CHUNK_GROUP, RET_CHUNK, RET_CHUNK), F32),
            pltpu.VMEM((RET_CHUNK_GROUP, RET_CHUNK, 2 * RET_QK_DIM), BF16),
        ],
        compiler_params=_params(("parallel", "arbitrary")),
        name="retention",
    )(log_decay, proj, proj, proj, proj, cos2, sin2, gain.reshape(RET_HEADS, 1, RET_V_DIM))


COPY_ROWS = 128
RUN_NUM, RUN_MAX, RUN_DEN = 0, 1, 2
RUN_PARTS = 3


def _attention_kernel(q_ref, k_ref, v_ref, diag_ref, mix_ref, o_ref,
                      t_ref, stage_ref, stage4_ref, q1_ref, q4_ref, k4_ref, v4_ref, q16_ref, k16_ref, v16_ref,
                      run_a, run_b, s_ref, p_ref, den_ref, max_ref):
    del mix_ref
    seq = q_ref.shape[1]
    q_scale = ATT_HEAD_DIM ** -0.5
    len4 = seq // 4
    len16 = seq // 16

    @pl.when(pl.program_id(1) == 0)
    def _():
        for bi in range(len(DILATIONS)):
            wide = jnp.broadcast_to(diag_ref[0, bi], (ATT_QB, 2 * ATT_KW))
            for vi, off in enumerate(BIAS_OFFSETS):
                rolled = pltpu.roll(wide, ATT_KW - off, 1, stride=1, stride_axis=0)
                t_ref[bi, vi] = rolled[:, :ATT_KW]

    def stage(src_ref, scale, natural_ref):
        def body(cc, carry):
            rows = pl.ds(pl.multiple_of(cc * COPY_ROWS, COPY_ROWS), COPY_ROWS)
            x = src_ref[0, rows, :].astype(F32)
            if scale is not None:
                x = x * scale
                natural_ref[rows, :] = x.astype(BF16)
            stage_ref[rows, :] = x
            return carry

        lax.fori_loop(0, seq // COPY_ROWS, body, 0)

    def deinterleave(dst4_ref, dst16_ref):
        ch = min(COPY_ROWS, len4)
        for r in range(4):
            for cc in range(len4 // ch):
                x = stage_ref[pl.ds(r + 4 * cc * ch, ch, stride=4), :]
                rows = pl.ds(r * len4 + cc * ch, ch)
                stage4_ref[rows, :] = x
                dst4_ref[rows, :] = x.astype(BF16)
        ch = min(COPY_ROWS, len16)
        for r in range(4):
            for r_sub in range(4):
                for cc in range(len16 // ch):
                    x = stage4_ref[pl.ds(r * len4 + r_sub + 4 * cc * ch, ch, stride=4), :]
                    dst16_ref[pl.ds((r + 4 * r_sub) * len16 + cc * ch, ch), :] = x.astype(BF16)

    stage(q_ref, q_scale, q1_ref)
    deinterleave(q4_ref, q16_ref)
    stage(k_ref, None, None)
    deinterleave(k4_ref, k16_ref)
    stage(v_ref, None, None)
    deinterleave(v4_ref, v16_ref)

    def run_branch(bi, d, load_q, load_k, load_v, run_ref, first, last):
        length = seq // d
        kw = min(length, ATT_KW)
        nblk = length // ATT_QB

        def tile_rows(t):
            r = t // nblk
            nb = t - r * nblk
            i0 = nb * ATT_QB
            ws = jnp.clip(i0 - ATT_HALF, 0, length - kw)
            var = jnp.where(nb == 0, 0, jnp.where(nb == nblk - 1, 2, 1))
            base = r * length
            qrows = pl.ds(pl.multiple_of(base + i0, ATT_QB), ATT_QB)
            krows = pl.ds(pl.multiple_of(base + ws, ATT_HALF), kw)
            return qrows, krows, var

        def body(g, carry):
            tiles = [tile_rows(g * ATT_TILE_GROUP + j) for j in range(ATT_TILE_GROUP)]
            for j, (qrows, krows, var) in enumerate(tiles):
                s = lax.dot_general(load_q(qrows), load_k(krows), (((1,), (1,)), ((), ())),
                                    preferred_element_type=F32)
                s_ref[j, :, :kw] = s + t_ref[bi, var, :, :kw]
            for j, (qrows, krows, var) in enumerate(tiles):
                s = s_ref[j, :, :kw]
                m = jnp.max(s, axis=-1, keepdims=True)
                p = jnp.exp(s - m)
                den = jnp.sum(p, axis=-1, keepdims=True)
                p_ref[j, :, :kw] = p.astype(BF16)
                den_ref[j] = jnp.broadcast_to(den, (ATT_QB, ATT_HEAD_DIM))
                max_ref[j] = jnp.broadcast_to(m, (ATT_QB, ATT_HEAD_DIM))
            for j, (qrows, krows, var) in enumerate(tiles):
                num = jnp.dot(p_ref[j, :, :kw], load_v(krows), preferred_element_type=F32)
                den = den_ref[j]
                top = max_ref[j]
                if not first:
                    m_run = run_ref[RUN_MAX, qrows, :]
                    top = jnp.maximum(max_ref[j], m_run)
                    w_new = jnp.exp(max_ref[j] - top)
                    w_run = jnp.exp(m_run - top)
                    num = w_new * num + w_run * run_ref[RUN_NUM, qrows, :]
                    den = w_new * den + w_run * run_ref[RUN_DEN, qrows, :]
                if last:
                    o_ref[0, qrows, :] = (num / den).astype(o_ref.dtype)
                else:
                    run_ref[RUN_NUM, qrows, :] = num
                    run_ref[RUN_MAX, qrows, :] = top
                    run_ref[RUN_DEN, qrows, :] = den
            return carry

        lax.fori_loop(0, d * nblk // ATT_TILE_GROUP, body, 0)

    def regroup(src_ref, dst_ref, n_src, n_dst):
        len_src = seq // n_src
        len_dst = seq // n_dst
        ch = min(COPY_ROWS, len_src)
        for r_dst in range(n_dst):
            for r_sub in range(4):
                r_src = r_dst + n_dst * r_sub
                for cc in range(len_src // ch):
                    src = pl.ds(r_src * len_src + cc * ch, ch)
                    dst = pl.ds(r_dst * len_dst + r_sub + 4 * cc * ch, ch, stride=4)
                    for part in range(RUN_PARTS):
                        dst_ref[part, dst, :] = src_ref[part, src, :]

    run_branch(2, 16, lambda rows: q16_ref[rows, :], lambda rows: k16_ref[rows, :], lambda rows: v16_ref[rows, :],
               run_a, first=True, last=False)
    regroup(run_a, run_b, 16, 4)
    run_branch(1, 4, lambda rows: q4_ref[rows, :], lambda rows: k4_ref[rows, :], lambda rows: v4_ref[rows, :],
               run_b, first=False, last=False)
    regroup(run_b, run_a, 4, 1)
    run_branch(0, 1, lambda rows: q1_ref[rows, :], lambda rows: k_ref[0, rows, :], lambda rows: v_ref[0, rows, :],
               run_a, first=False, last=True)


def _attention(proj, bias_diags, mix):
    b, seq, _ = proj.shape
    q_block0 = (2 * RET_QK_WIDTH + 2 * RET_WIDTH) // ATT_HEAD_DIM
    k_block0 = q_block0 + ATT_HEADS
    v_block0 = k_block0 + ATT_HEADS
    out_block0 = RET_WIDTH // ATT_HEAD_DIM
    nb, nv = len(DILATIONS), len(BIAS_OFFSETS)
    head_rows = pltpu.VMEM((seq, ATT_HEAD_DIM), BF16)
    head_rows_f32 = pltpu.VMEM((seq, ATT_HEAD_DIM), F32)
    return pl.pallas_call(
        _attention_kernel,
        grid=(ATT_HEADS, b),
        in_specs=[
            pl.BlockSpec((1, seq, ATT_HEAD_DIM), lambda h, i: (i, 0, q_block0 + h)),
            pl.BlockSpec((1, seq, ATT_HEAD_DIM), lambda h, i: (i, 0, k_block0 + h)),
            pl.BlockSpec((1, seq, ATT_HEAD_DIM), lambda h, i: (i, 0, v_block0 + h)),
            pl.BlockSpec((1, nb, 1, 2 * ATT_KW), lambda h, i: (h, 0, 0, 0)),
            pl.BlockSpec(memory_space=pl.ANY),
        ],
        out_specs=pl.BlockSpec((1, seq, ATT_HEAD_DIM), lambda h, i: (i, 0, out_block0 + h)),
        out_shape=jax.ShapeDtypeStruct((b, seq, D_MODEL), BF16),
        scratch_shapes=[
            pltpu.VMEM((nb, nv, ATT_QB, ATT_KW), F32),
            head_rows_f32,
            head_rows_f32,
            head_rows,
            head_rows, head_rows, head_rows,
            head_rows, head_rows, head_rows,
            pltpu.VMEM((RUN_PARTS, seq, ATT_HEAD_DIM), F32),
            pltpu.VMEM((RUN_PARTS, seq, ATT_HEAD_DIM), F32),
            pltpu.VMEM((ATT_TILE_GROUP, ATT_QB, ATT_KW), F32),
            pltpu.VMEM((ATT_TILE_GROUP, ATT_QB, ATT_KW), BF16),
            pltpu.VMEM((ATT_TILE_GROUP, ATT_QB, ATT_HEAD_DIM), F32),
            pltpu.VMEM((ATT_TILE_GROUP, ATT_QB, ATT_HEAD_DIM), F32),
        ],
        input_output_aliases={4: 0},
        compiler_params=_params(("arbitrary", "arbitrary")),
        name="dilated_attention",
    )(proj, proj, proj, bias_diags, mix)


def _rotary_tables(seq):
    half = RET_QK_DIM // 2
    inv = ROPE_BASE ** (-jnp.arange(half, dtype=F32) / half)
    ang = jnp.arange(seq, dtype=F32)[:, None] * inv[None, :]
    cos, sin = jnp.cos(ang), jnp.sin(ang)
    return jnp.concatenate([cos, cos], axis=-1), jnp.concatenate([-sin, sin], axis=-1)


def _bucket_of(rel):
    nbk = REL_BUCKETS // 2
    max_exact = nbk // 2
    n = np.abs(rel)
    nf = np.maximum(n, 1).astype(np.float32)
    large = max_exact + (np.log(nf / max_exact) / math.log(REL_MAX_DISTANCE / max_exact)
                         * (nbk - max_exact)).astype(np.int32)
    large = np.minimum(large, nbk - 1)
    return np.where(rel > 0, nbk, 0) + np.where(n < max_exact, n, large)


def _bias_diags(rel_bias_table):
    delta = np.arange(2 * ATT_KW) - ATT_KW
    buckets = np.stack([_bucket_of(delta * d) for d in DILATIONS])
    inside = np.abs(delta) <= ATT_HALF
    bias = jnp.transpose(rel_bias_table.astype(F32))[:, buckets]
    return jnp.where(inside[None, None], bias, NEG_INF)[:, :, None, :]


def _trunk(x, bias_diags, weights):
    b, seq, dm = x.shape
    cos2, sin2 = _rotary_tables(seq)
    xt = x.reshape(b * seq, dm)
    for layer in weights:
        proj = _norm_matmul(xt, layer["norm_mix_pre"], layer["w_in"], relu2=False, name="in_proj")
        proj = proj.reshape(b, seq, IN_WIDTH)
        mix = _retention(proj, layer["log_decay"], layer["ret_norm_gain"], cos2, sin2)
        mix = _attention(proj, bias_diags, mix)
        xt = _matmul_norm_res(mix.reshape(b * seq, dm), layer["w_out"], xt, layer["norm_mix_post"], name="out_proj")
        u = _norm_matmul(xt, layer["norm_mlp_pre"], layer["w_up"], relu2=True, name="up_proj")
        xt = _matmul_norm_res(u, layer["w_down"], xt, layer["norm_mlp_post"], name="down_proj")
    return xt.reshape(b, seq, dm)


def kernel(x_prompt, x_sample, rel_bias_table, w_in, ret_log_decay, ret_norm_gain, w_out, w_up, w_down,
           norm_mix_pre, norm_mix_post, norm_mlp_pre, norm_mlp_post):
    depth = w_in.shape[0]
    weights = []
    for l in range(depth):
        weights.append({
            "w_in": w_in[l].astype(BF16),
            "w_out": w_out[l].astype(BF16),
            "w_up": w_up[l].astype(BF16),
            "w_down": w_down[l].astype(BF16),
            "log_decay": -jnp.exp(ret_log_decay[l].astype(F32)),
            "ret_norm_gain": ret_norm_gain[l].astype(F32),
            "norm_mix_pre": norm_mix_pre[l].astype(F32),
            "norm_mix_post": norm_mix_post[l].astype(F32),
            "norm_mlp_pre": norm_mlp_pre[l].astype(F32),
            "norm_mlp_post": norm_mlp_post[l].astype(F32),
        })
    bias_diags = _bias_diags(rel_bias_table)
    y_prompt = _trunk(x_prompt, bias_diags, weights)
    y_sample = _trunk(x_sample, bias_diags, weights)
    return (y_prompt, y_sample)
```

```python
import functools
import math

import numpy as np
import jax
import jax.numpy as jnp
from jax import lax
from jax.experimental import pallas as pl
from jax.experimental.pallas import tpu as pltpu

BF16 = jnp.bfloat16
F32 = jnp.float32

D_MODEL = 4096
RET_HEADS = 8
RET_QK_DIM = 128
RET_V_DIM = 256
RET_QK_WIDTH = RET_HEADS * RET_QK_DIM
RET_WIDTH = RET_HEADS * RET_V_DIM
ATT_HEADS = 16
ATT_HEAD_DIM = 128
ATT_WIDTH = ATT_HEADS * ATT_HEAD_DIM
IN_WIDTH = 2 * RET_QK_WIDTH + 2 * RET_WIDTH + 3 * ATT_WIDTH
RET_CHUNK = 128
RET_CHUNK_GROUP = 4
ROPE_BASE = 10000.0
DILATIONS = (1, 4, 16)
ATT_HALF = 64
REL_BUCKETS = 32
REL_MAX_DISTANCE = 1024
NORM_EPS = 1e-6
NEG_INF = -1e30

ATT_QB = 128
ATT_KW = 256
BIAS_OFFSETS = (0, -ATT_HALF, -2 * ATT_HALF)
ATT_TILE_GROUP = 8

V7X_VMEM_LIMIT_BYTES = 56 * 1024 * 1024
V7X_VMEM_LIMIT_LARGE_BYTES = 60 * 1024 * 1024


def _params(sem, vmem=V7X_VMEM_LIMIT_BYTES):
    return pltpu.CompilerParams(dimension_semantics=sem, vmem_limit_bytes=vmem)


NORM_ROWS = 16
NORM_SLICES = 8


def _norm_matmul_kernel(xs_ref, g_ref, w_ref, o_ref, h_even_ref, h_odd_ref, *, relu2):
    i = pl.program_id(0)
    j = pl.program_id(1)
    slice_rows = xs_ref.shape[0]

    def normalise_slice(h_ref):
        gain = g_ref[...]
        base = jnp.minimum(j, NORM_SLICES - 1) * slice_rows
        for c in range(slice_rows // NORM_ROWS):
            x = xs_ref[pl.ds(c * NORM_ROWS, NORM_ROWS), :]
            ms = jnp.mean(x * x, axis=-1, keepdims=True)
            rows = pl.ds(pl.multiple_of(base + c * NORM_ROWS, NORM_ROWS), NORM_ROWS)
            h_ref[rows, :] = ((x * lax.rsqrt(ms + NORM_EPS)) * gain).astype(BF16)

    def multiply(h_ref):
        y = jnp.dot(h_ref[...], w_ref[...], preferred_element_type=F32)
        if relu2:
            y = jnp.maximum(y, 0.0)
            y = y * y
        o_ref[...] = y.astype(o_ref.dtype)

    @pl.when(i == 0)
    def _():
        normalise_slice(h_even_ref)

    @pl.when((i > 0) & (i % 2 == 1))
    def _():
        normalise_slice(h_odd_ref)
        multiply(h_even_ref)

    @pl.when((i > 0) & (i % 2 == 0))
    def _():
        normalise_slice(h_even_ref)
        multiply(h_odd_ref)


def _norm_matmul(x, gain, w, *, relu2, tm=1024, tn=1024, name):
    t, k = x.shape
    n = w.shape[1]
    ni, nj = t // tm, n // tn
    assert nj >= NORM_SLICES and tm % (NORM_SLICES * NORM_ROWS) == 0
    slice_rows = tm // NORM_SLICES
    return pl.pallas_call(
        functools.partial(_norm_matmul_kernel, relu2=relu2),
        grid=(ni + 1, nj),
        in_specs=[
            pl.BlockSpec((slice_rows, k),
                         lambda i, j: (jnp.minimum(i, ni - 1) * NORM_SLICES + jnp.minimum(j, NORM_SLICES - 1), 0)),
            pl.BlockSpec((1, k), lambda i, j: (0, 0)),
            pl.BlockSpec((k, tn), lambda i, j: (0, jnp.where(i == 0, 0, j))),
        ],
        out_specs=pl.BlockSpec((tm, tn), lambda i, j: (jnp.maximum(i - 1, 0), jnp.where(i == 0, 0, j))),
        out_shape=jax.ShapeDtypeStruct((t, n), BF16),
        scratch_shapes=[pltpu.VMEM((tm, k), BF16), pltpu.VMEM((tm, k), BF16)],
        compiler_params=_params(("arbitrary", "arbitrary")),
        name=name,
    )(x, gain.reshape(1, k), w)


def _matmul_norm_res_kernel(a_ref, w_ref, xs_ref, g_ref, os_ref, acc_even_ref, acc_odd_ref, *, ni):
    i = pl.program_id(0)
    kk = pl.program_id(1)
    slice_rows = os_ref.shape[0]

    @pl.when((i == 0) & (kk == 0))
    def _():
        acc_even_ref[...] = jnp.zeros_like(acc_even_ref)
        acc_odd_ref[...] = jnp.zeros_like(acc_odd_ref)

    def accumulate(acc_ref):
        part = jnp.dot(a_ref[...], w_ref[...], preferred_element_type=F32)
        acc_ref[...] = jnp.where(kk == 0, part, acc_ref[...] + part)

    def finish_slice(acc_ref):
        gain = g_ref[...]
        for c in range(slice_rows // NORM_ROWS):
            rows = pl.ds(pl.multiple_of(kk * slice_rows + c * NORM_ROWS, NORM_ROWS), NORM_ROWS)
            y = acc_ref[rows, :]
            ms = jnp.mean(y * y, axis=-1, keepdims=True)
            local = pl.ds(c * NORM_ROWS, NORM_ROWS)
            os_ref[local, :] = xs_ref[local, :] + (y * lax.rsqrt(ms + NORM_EPS)) * gain

    @pl.when(i == 0)
    def _():
        accumulate(acc_even_ref)

    @pl.when((i > 0) & (i < ni) & (i % 2 == 1))
    def _():
        accumulate(acc_odd_ref)
        finish_slice(acc_even_ref)

    @pl.when((i > 0) & (i < ni) & (i % 2 == 0))
    def _():
        accumulate(acc_even_ref)
        finish_slice(acc_odd_ref)

    @pl.when(i == ni)
    def _():
        finish_slice(acc_odd_ref if (ni - 1) % 2 == 1 else acc_even_ref)


def _matmul_norm_res(a, w, x, gain, *, tm=512, tk, name):
    t, k = a.shape
    n = w.shape[1]
    ni, nk = t // tm, k // tk
    assert tm % (nk * NORM_ROWS) == 0
    slice_rows = tm // nk

    def slice_index(i, kk):
        return (jnp.maximum(i - 1, 0) * nk + jnp.where(i == 0, 0, kk), 0)

    return pl.pallas_call(
        functools.partial(_matmul_norm_res_kernel, ni=ni),
        grid=(ni + 1, nk),
        in_specs=[
            pl.BlockSpec((tm, tk), lambda i, kk: (jnp.minimum(i, ni - 1), jnp.where(i == ni, nk - 1, kk))),
            pl.BlockSpec((tk, n), lambda i, kk: (jnp.where(i == ni, nk - 1, kk), 0)),
            pl.BlockSpec((slice_rows, n), slice_index),
            pl.BlockSpec((1, n), lambda i, kk: (0, 0)),
        ],
        out_specs=pl.BlockSpec((slice_rows, n), slice_index),
        out_shape=jax.ShapeDtypeStruct((t, n), F32),
        scratch_shapes=[pltpu.VMEM((tm, n), F32), pltpu.VMEM((tm, n), F32)],
        compiler_params=_params(("arbitrary", "arbitrary"), V7X_VMEM_LIMIT_LARGE_BYTES),
        name=name,
    )(a, w, x, gain.reshape(1, n))


def _retention_kernel(ld_ref, q_ref, k_ref, v_ref, g_ref, cos_ref, sin_ref, gain_ref, o_ref,
                      qd_ref, y_ref, delta_ref, prev_ref, s_ref, kd_ref):
    seq = q_ref.shape[1]
    c = RET_CHUNK
    nc = seq // c
    grp = RET_CHUNK_GROUP
    h = pl.program_id(1)
    ld_f = ld_ref[0, h]
    ld_b = ld_ref[1, h]

    ri = lax.broadcasted_iota(jnp.int32, (c, c), 0)
    ci = lax.broadcasted_iota(jnp.int32, (c, c), 1)
    diff = (ri - ci).astype(F32)
    mask = jnp.where(diff >= 0, jnp.exp(ld_f * jnp.maximum(diff, 0.0)), jnp.exp(ld_b * jnp.maximum(-diff, 0.0)))
    pos = lax.broadcasted_iota(jnp.int32, (c, RET_QK_DIM), 0).astype(F32)
    qdec_f = jnp.exp(ld_f * (pos + 1.0))
    kdec_f = jnp.exp(ld_f * (c - 1.0 - pos))
    qdec_b = jnp.exp(ld_b * (c - pos))
    kdec_b = jnp.exp(ld_b * pos)
    cd_f = jnp.exp(jnp.full((1, RET_V_DIM), ld_f * c, F32))
    cd_b = jnp.exp(jnp.full((1, RET_V_DIM), ld_b * c, F32))
    gain = gain_ref[0]
    k_scale = RET_QK_DIM ** -0.5
    half = RET_QK_DIM // 2

    def chunk_rows(n):
        return pl.ds(pl.multiple_of(n * c, c), c)

    def phase_a(gi, carry):
        chunks = [gi * grp + j for j in range(grp)]
        for j, n in enumerate(chunks):
            rows = chunk_rows(n)
            cos = cos_ref[rows, :]
            sin = sin_ref[rows, :]
            xq = q_ref[0, rows, :].astype(F32)
            xk = k_ref[0, rows, :].astype(F32)
            q = xq * cos + pltpu.roll(xq, half, 1) * sin
            k = (xk * cos + pltpu.roll(xk, half, 1) * sin) * k_scale
            qd_ref[rows, :] = jnp.concatenate([(q * qdec_f).astype(BF16), (q * qdec_b).astype(BF16)], axis=1)
            kd_ref[j] = jnp.concatenate([(k * kdec_f).astype(BF16), (k * kdec_b).astype(BF16)], axis=1)
            s_ref[j] = lax.dot_general(q.astype(BF16), k.astype(BF16), (((1,), (1,)), ((), ())),
                                       preferred_element_type=F32)
        for j, n in enumerate(chunks):
            rows = chunk_rows(n)
            y_ref[rows, :] = jnp.dot((s_ref[j] * mask).astype(BF16), v_ref[0, rows, :],
                                     preferred_element_type=F32)
        for j, n in enumerate(chunks):
            delta_ref[n] = lax.dot_general(kd_ref[j], v_ref[0, chunk_rows(n), :], (((0,), (0,)), ((), ())),
                                           preferred_element_type=F32)
        return carry

    lax.fori_loop(0, nc // grp, phase_a, 0)

    def scan_f(n, st):
        prev_ref[n, pl.ds(0, RET_QK_DIM), :] = st.astype(BF16)
        return st * cd_f + delta_ref[n, pl.ds(0, RET_QK_DIM), :]

    def scan_b(t, st):
        n = nc - 1 - t
        prev_ref[n, pl.ds(RET_QK_DIM, RET_QK_DIM), :] = st.astype(BF16)
        return st * cd_b + delta_ref[n, pl.ds(RET_QK_DIM, RET_QK_DIM), :]

    zero_state = jnp.zeros((RET_QK_DIM, RET_V_DIM), F32)
    lax.fori_loop(0, nc, scan_f, zero_state)
    lax.fori_loop(0, nc, scan_b, zero_state)

    def phase_c(gi, carry):
        chunks = [gi * grp + j for j in range(grp)]
        ys = []
        for n in chunks:
            rows = chunk_rows(n)
            ys.append(y_ref[rows, :] + jnp.dot(qd_ref[rows, :], prev_ref[n], preferred_element_type=F32))
        for n, y in zip(chunks, ys):
            rows = chunk_rows(n)
            ms = jnp.mean(y * y, axis=-1, keepdims=True)
            yn = (y * lax.rsqrt(ms + NORM_EPS)) * gain
            gate = g_ref[0, rows, :].astype(F32)
            o_ref[0, rows, :] = ((gate * jax.nn.sigmoid(gate)) * yn).astype(o_ref.dtype)
        return carry

    lax.fori_loop(0, nc // grp, phase_c, 0)


def _retention(proj, log_decay, gain, cos2, sin2):
    b, seq, _ = proj.shape
    qk_blocks = RET_QK_WIDTH // RET_QK_DIM
    v_block0 = 2 * RET_QK_WIDTH // RET_V_DIM
    g_block0 = v_block0 + RET_HEADS
    nc = seq // RET_CHUNK
    return pl.pallas_call(
        _retention_kernel,
        grid=(b, RET_HEADS),
        in_specs=[
            pl.BlockSpec(memory_space=pltpu.SMEM),
            pl.BlockSpec((1, seq, RET_QK_DIM), lambda i, h: (i, 0, h)),
            pl.BlockSpec((1, seq, RET_QK_DIM), lambda i, h: (i, 0, qk_blocks + h)),
            pl.BlockSpec((1, seq, RET_V_DIM), lambda i, h: (i, 0, v_block0 + h)),
            pl.BlockSpec((1, seq, RET_V_DIM), lambda i, h: (i, 0, g_block0 + h)),
            pl.BlockSpec((seq, RET_QK_DIM), lambda i, h: (0, 0)),
            pl.BlockSpec((seq, RET_QK_DIM), lambda i, h: (0, 0)),
            pl.BlockSpec((1, 1, RET_V_DIM), lambda i, h: (h, 0, 0)),
        ],
        out_specs=pl.BlockSpec((1, seq, RET_V_DIM), lambda i, h: (i, 0, h)),
        out_shape=jax.ShapeDtypeStruct((b, seq, D_MODEL), BF16),
        scratch_shapes=[
            pltpu.VMEM((seq, 2 * RET_QK_DIM), BF16),
            pltpu.VMEM((seq, RET_V_DIM), F32),
            pltpu.VMEM((nc, 2 * RET_QK_DIM, RET_V_DIM), F32),
            pltpu.VMEM((nc, 2 * RET_QK_DIM, RET_V_DIM), BF16),
            pltpu.VMEM((RET_CHUNK_GROUP, RET_CHUNK, RET_CHUNK), F32),
            pltpu.VMEM((RET_CHUNK_GROUP, RET_CHUNK, 2 * RET_QK_DIM), BF16),
        ],
        compiler_params=_params(("parallel", "arbitrary")),
        name="retention",
    )(log_decay, proj, proj, proj, proj, cos2, sin2, gain.reshape(RET_HEADS, 1, RET_V_DIM))


COPY_ROWS = 128
RUN_NUM, RUN_MAX, RUN_DEN = 0, 1, 2
RUN_PARTS = 3


def _attention_kernel(q_ref, k_ref, v_ref, diag_ref, mix_ref, o_ref,
                      t_ref, stage_ref, stage4_ref, q1_ref, q4_ref, k4_ref, v4_ref, q16_ref, k16_ref, v16_ref,
                      run_a, run_b, s_ref, p_ref, den_ref, max_ref):
    del mix_ref
    seq = q_ref.shape[1]
    q_scale = ATT_HEAD_DIM ** -0.5
    len4 = seq // 4
    len16 = seq // 16

    @pl.when(pl.program_id(1) == 0)
    def _():
        for bi in range(len(DILATIONS)):
            wide = jnp.broadcast_to(diag_ref[0, bi], (ATT_QB, 2 * ATT_KW))
            for vi, off in enumerate(BIAS_OFFSETS):
                rolled = pltpu.roll(wide, ATT_KW - off, 1, stride=1, stride_axis=0)
                t_ref[bi, vi] = rolled[:, :ATT_KW]

    def stage(src_ref, scale, natural_ref):
        def body(cc, carry):
            rows = pl.ds(pl.multiple_of(cc * COPY_ROWS, COPY_ROWS), COPY_ROWS)
            x = src_ref[0, rows, :].astype(F32)
            if scale is not None:
                x = x * scale
                natural_ref[rows, :] = x.astype(BF16)
            stage_ref[rows, :] = x
            return carry

        lax.fori_loop(0, seq // COPY_ROWS, body, 0)

    def deinterleave(dst4_ref, dst16_ref):
        ch = min(COPY_ROWS, len4)
        for r in range(4):
            for cc in range(len4 // ch):
                x = stage_ref[pl.ds(r + 4 * cc * ch, ch, stride=4), :]
                rows = pl.ds(r * len4 + cc * ch, ch)
                stage4_ref[rows, :] = x
                dst4_ref[rows, :] = x.astype(BF16)
        ch = min(COPY_ROWS, len16)
        for r in range(4):
            for r_sub in range(4):
                for cc in range(len16 // ch):
                    x = stage4_ref[pl.ds(r * len4 + r_sub + 4 * cc * ch, ch, stride=4), :]
                    dst16_ref[pl.ds((r + 4 * r_sub) * len16 + cc * ch, ch), :] = x.astype(BF16)

    stage(q_ref, q_scale, q1_ref)
    deinterleave(q4_ref, q16_ref)
    stage(k_ref, None, None)
    deinterleave(k4_ref, k16_ref)
    stage(v_ref, None, None)
    deinterleave(v4_ref, v16_ref)

    def run_branch(bi, d, load_q, load_k, load_v, run_ref, first, last):
        length = seq // d
        kw = min(length, ATT_KW)
        nblk = length // ATT_QB

        def tile_rows(t):
            r = t // nblk
            nb = t - r * nblk
            i0 = nb * ATT_QB
            ws = jnp.clip(i0 - ATT_HALF, 0, length - kw)
            var = jnp.where(nb == 0, 0, jnp.where(nb == nblk - 1, 2, 1))
            base = r * length
            qrows = pl.ds(pl.multiple_of(base + i0, ATT_QB), ATT_QB)
            krows = pl.ds(pl.multiple_of(base + ws, ATT_HALF), kw)
            return qrows, krows, var

        def body(g, carry):
            tiles = [tile_rows(g * ATT_TILE_GROUP + j) for j in range(ATT_TILE_GROUP)]
            for j, (qrows, krows, var) in enumerate(tiles):
                s = lax.dot_general(load_q(qrows), load_k(krows), (((1,), (1,)), ((), ())),
                                    preferred_element_type=F32)
                s_ref[j, :, :kw] = s + t_ref[bi, var, :, :kw]
            for j, (qrows, krows, var) in enumerate(tiles):
                s = s_ref[j, :, :kw]
                m = jnp.max(s, axis=-1, keepdims=True)
                p = jnp.exp(s - m)
                den = jnp.sum(p, axis=-1, keepdims=True)
                p_ref[j, :, :kw] = p.astype(BF16)
                den_ref[j] = jnp.broadcast_to(den, (ATT_QB, ATT_HEAD_DIM))
                max_ref[j] = jnp.broadcast_to(m, (ATT_QB, ATT_HEAD_DIM))
            for j, (qrows, krows, var) in enumerate(tiles):
                num = jnp.dot(p_ref[j, :, :kw], load_v(krows), preferred_element_type=F32)
                den = den_ref[j]
                top = max_ref[j]
                if not first:
                    m_run = run_ref[RUN_MAX, qrows, :]
                    top = jnp.maximum(max_ref[j], m_run)
                    w_new = jnp.exp(max_ref[j] - top)
                    w_run = jnp.exp(m_run - top)
                    num = w_new * num + w_run * run_ref[RUN_NUM, qrows, :]
                    den = w_new * den + w_run * run_ref[RUN_DEN, qrows, :]
                if last:
                    o_ref[0, qrows, :] = (num / den).astype(o_ref.dtype)
                else:
                    run_ref[RUN_NUM, qrows, :] = num
                    run_ref[RUN_MAX, qrows, :] = top
                    run_ref[RUN_DEN, qrows, :] = den
            return carry

        lax.fori_loop(0, d * nblk // ATT_TILE_GROUP, body, 0)

    def regroup(src_ref, dst_ref, n_src, n_dst):
        len_src = seq // n_src
        len_dst = seq // n_dst
        ch = min(COPY_ROWS, len_src)
        for r_dst in range(n_dst):
            for r_sub in range(4):
                r_src = r_dst + n_dst * r_sub
                for cc in range(len_src // ch):
                    src = pl.ds(r_src * len_src + cc * ch, ch)
                    dst = pl.ds(r_dst * len_dst + r_sub + 4 * cc * ch, ch, stride=4)
                    for part in range(RUN_PARTS):
                        dst_ref[part, dst, :] = src_ref[part, src, :]

    run_branch(2, 16, lambda rows: q16_ref[rows, :], lambda rows: k16_ref[rows, :], lambda rows: v16_ref[rows, :],
               run_a, first=True, last=False)
    regroup(run_a, run_b, 16, 4)
    run_branch(1, 4, lambda rows: q4_ref[rows, :], lambda rows: k4_ref[rows, :], lambda rows: v4_ref[rows, :],
               run_b, first=False, last=False)
    regroup(run_b, run_a, 4, 1)
    run_branch(0, 1, lambda rows: q1_ref[rows, :], lambda rows: k_ref[0, rows, :], lambda rows: v_ref[0, rows, :],
               run_a, first=False, last=True)


def _attention(proj, bias_diags, mix):
    b, seq, _ = proj.shape
    q_block0 = (2 * RET_QK_WIDTH + 2 * RET_WIDTH) // ATT_HEAD_DIM
    k_block0 = q_block0 + ATT_HEADS
    v_block0 = k_block0 + ATT_HEADS
    out_block0 = RET_WIDTH // ATT_HEAD_DIM
    nb, nv = len(DILATIONS), len(BIAS_OFFSETS)
    head_rows = pltpu.VMEM((seq, ATT_HEAD_DIM), BF16)
    head_rows_f32 = pltpu.VMEM((seq, ATT_HEAD_DIM), F32)
    return pl.pallas_call(
        _attention_kernel,
        grid=(ATT_HEADS, b),
        in_specs=[
            pl.BlockSpec((1, seq, ATT_HEAD_DIM), lambda h, i: (i, 0, q_block0 + h)),
            pl.BlockSpec((1, seq, ATT_HEAD_DIM), lambda h, i: (i, 0, k_block0 + h)),
            pl.BlockSpec((1, seq, ATT_HEAD_DIM), lambda h, i: (i, 0, v_block0 + h)),
            pl.BlockSpec((1, nb, 1, 2 * ATT_KW), lambda h, i: (h, 0, 0, 0)),
            pl.BlockSpec(memory_space=pl.ANY),
        ],
        out_specs=pl.BlockSpec((1, seq, ATT_HEAD_DIM), lambda h, i: (i, 0, out_block0 + h)),
        out_shape=jax.ShapeDtypeStruct((b, seq, D_MODEL), BF16),
        scratch_shapes=[
            pltpu.VMEM((nb, nv, ATT_QB, ATT_KW), F32),
            head_rows_f32,
            head_rows_f32,
            head_rows,
            head_rows, head_rows, head_rows,
            head_rows, head_rows, head_rows,
            pltpu.VMEM((RUN_PARTS, seq, ATT_HEAD_DIM), F32),
            pltpu.VMEM((RUN_PARTS, seq, ATT_HEAD_DIM), F32),
            pltpu.VMEM((ATT_TILE_GROUP, ATT_QB, ATT_KW), F32),
            pltpu.VMEM((ATT_TILE_GROUP, ATT_QB, ATT_KW), BF16),
            pltpu.VMEM((ATT_TILE_GROUP, ATT_QB, ATT_HEAD_DIM), F32),
            pltpu.VMEM((ATT_TILE_GROUP, ATT_QB, ATT_HEAD_DIM), F32),
        ],
        input_output_aliases={4: 0},
        compiler_params=_params(("arbitrary", "arbitrary")),
        name="dilated_attention",
    )(proj, proj, proj, bias_diags, mix)


def _rotary_tables(seq):
    half = RET_QK_DIM // 2
    inv = ROPE_BASE ** (-jnp.arange(half, dtype=F32) / half)
    ang = jnp.arange(seq, dtype=F32)[:, None] * inv[None, :]
    cos, sin = jnp.cos(ang), jnp.sin(ang)
    return jnp.concatenate([cos, cos], axis=-1), jnp.concatenate([-sin, sin], axis=-1)


def _bucket_of(rel):
    nbk = REL_BUCKETS // 2
    max_exact = nbk // 2
    n = np.abs(rel)
    nf = np.maximum(n, 1).astype(np.float32)
    large = max_exact + (np.log(nf / max_exact) / math.log(REL_MAX_DISTANCE / max_exact)
                         * (nbk - max_exact)).astype(np.int32)
    large = np.minimum(large, nbk - 1)
    return np.where(rel > 0, nbk, 0) + np.where(n < max_exact, n, large)


def _bias_diags(rel_bias_table):
    delta = np.arange(2 * ATT_KW) - ATT_KW
    buckets = np.stack([_bucket_of(delta * d) for d in DILATIONS])
    inside = np.abs(delta) <= ATT_HALF
    bias = jnp.transpose(rel_bias_table.astype(F32))[:, buckets]
    return jnp.where(inside[None, None], bias, NEG_INF)[:, :, None, :]


def _trunk(x, bias_diags, weights):
    b, seq, dm = x.shape
    cos2, sin2 = _rotary_tables(seq)
    xt = x.reshape(b * seq, dm)
    for layer in weights:
        proj = _norm_matmul(xt, layer["norm_mix_pre"], layer["w_in"], relu2=False, name="in_proj")
        proj = proj.reshape(b, seq, IN_WIDTH)
        mix = _retention(proj, layer["log_decay"], layer["ret_norm_gain"], cos2, sin2)
        mix = _attention(proj, bias_diags, mix)
        xt = _matmul_norm_res(mix.reshape(b * seq, dm), layer["w_out"], xt, layer["norm_mix_post"], tk=1024,
                              name="out_proj")
        u = _norm_matmul(xt, layer["norm_mlp_pre"], layer["w_up"], relu2=True, name="up_proj")
        xt = _matmul_norm_res(u, layer["w_down"], xt, layer["norm_mlp_post"], tk=2048, name="down_proj")
    return xt.reshape(b, seq, dm)


def kernel(x_prompt, x_sample, rel_bias_table, w_in, ret_log_decay, ret_norm_gain, w_out, w_up, w_down,
           norm_mix_pre, norm_mix_post, norm_mlp_pre, norm_mlp_post):
    depth = w_in.shape[0]
    weights = []
    for l in range(depth):
        weights.append({
            "w_in": w_in[l].astype(BF16),
            "w_out": w_out[l].astype(BF16),
            "w_up": w_up[l].astype(BF16),
            "w_down": w_down[l].astype(BF16),
            "log_decay": -jnp.exp(ret_log_decay[l].astype(F32)),
            "ret_norm_gain": ret_norm_gain[l].astype(F32),
            "norm_mix_pre": norm_mix_pre[l].astype(F32),
            "norm_mix_post": norm_mix_post[l].astype(F32),
            "norm_mlp_pre": norm_mlp_pre[l].astype(F32),
            "norm_mlp_post": norm_mlp_post[l].astype(F32),
        })
    bias_diags = _bias_diags(rel_bias_table)
    y_prompt = _trunk(x_prompt, bias_diags, weights)
    y_sample = _trunk(x_sample, bias_diags, weights)
    return (y_prompt, y_sample)
```

```python
import functools
import math

import numpy as np
import jax
import jax.numpy as jnp
from jax import lax
from jax.experimental import pallas as pl
from jax.experimental.pallas import tpu as pltpu

BF16 = jnp.bfloat16
F32 = jnp.float32

D_MODEL = 4096
RET_HEADS = 8
RET_QK_DIM = 128
RET_V_DIM = 256
RET_QK_WIDTH = RET_HEADS * RET_QK_DIM
RET_WIDTH = RET_HEADS * RET_V_DIM
ATT_HEADS = 16
ATT_HEAD_DIM = 128
ATT_WIDTH = ATT_HEADS * ATT_HEAD_DIM
IN_WIDTH = 2 * RET_QK_WIDTH + 2 * RET_WIDTH + 3 * ATT_WIDTH
RET_CHUNK = 128
RET_CHUNK_GROUP = 4
ROPE_BASE = 10000.0
DILATIONS = (1, 4, 16)
ATT_HALF = 64
REL_BUCKETS = 32
REL_MAX_DISTANCE = 1024
NORM_EPS = 1e-6
NEG_INF = -1e30

ATT_QB = 128
ATT_KW = 256
BIAS_OFFSETS = (0, -ATT_HALF, -2 * ATT_HALF)
ATT_TILE_GROUP = 8

V7X_VMEM_LIMIT_BYTES = 56 * 1024 * 1024
V7X_VMEM_LIMIT_LARGE_BYTES = 60 * 1024 * 1024


def _params(sem, vmem=V7X_VMEM_LIMIT_BYTES):
    return pltpu.CompilerParams(dimension_semantics=sem, vmem_limit_bytes=vmem)


NORM_ROWS = 16
NORM_SLICES = 8


def _norm_matmul_kernel(xs_ref, g_ref, w_ref, *refs, relu2, n_casts):
    cast_in_refs = refs[:n_casts]
    o_ref = refs[n_casts]
    cast_out_refs = refs[n_casts + 1:2 * n_casts + 1]
    h_even_ref, h_odd_ref = refs[2 * n_casts + 1:]
    i = pl.program_id(0)
    j = pl.program_id(1)
    slice_rows = xs_ref.shape[0]

    def cast_blocks():
        for src_ref, dst_ref in zip(cast_in_refs, cast_out_refs):
            dst_ref[...] = src_ref[...].astype(BF16)

    def normalise_slice(h_ref):
        gain = g_ref[...]
        base = jnp.minimum(j, NORM_SLICES - 1) * slice_rows
        for c in range(slice_rows // NORM_ROWS):
            x = xs_ref[pl.ds(c * NORM_ROWS, NORM_ROWS), :]
            ms = jnp.mean(x * x, axis=-1, keepdims=True)
            rows = pl.ds(pl.multiple_of(base + c * NORM_ROWS, NORM_ROWS), NORM_ROWS)
            h_ref[rows, :] = ((x * lax.rsqrt(ms + NORM_EPS)) * gain).astype(BF16)

    def multiply(h_ref):
        y = jnp.dot(h_ref[...], w_ref[...], preferred_element_type=F32)
        if relu2:
            y = jnp.maximum(y, 0.0)
            y = y * y
        o_ref[...] = y.astype(o_ref.dtype)

    @pl.when(i == 0)
    def _():
        cast_blocks()
        normalise_slice(h_even_ref)

    @pl.when((i > 0) & (i % 2 == 1))
    def _():
        cast_blocks()
        normalise_slice(h_odd_ref)
        multiply(h_even_ref)

    @pl.when((i > 0) & (i % 2 == 0))
    def _():
        cast_blocks()
        normalise_slice(h_even_ref)
        multiply(h_odd_ref)


def _norm_matmul(x, gain, w, *, relu2, casts=(), tm=1024, tn=1024, name):
    t, k = x.shape
    n = w.shape[1]
    ni, nj = t // tm, n // tn
    assert nj >= NORM_SLICES and tm % (NORM_SLICES * NORM_ROWS) == 0
    slice_rows = tm // NORM_SLICES
    steps = (ni + 1) * nj
    cast_specs = []
    for m in casts:
        rows, cols = m.shape
        n_blocks = max(nb for nb in (8, 16, 32, 64, 128, 256, 512) if nb <= steps and rows % (nb * NORM_ROWS) == 0)
        block_rows = rows // n_blocks
        cast_specs.append(pl.BlockSpec((block_rows, cols),
                                       lambda i, j, last=n_blocks - 1: (jnp.minimum(i * nj + j, last), 0)))
    outs = pl.pallas_call(
        functools.partial(_norm_matmul_kernel, relu2=relu2, n_casts=len(casts)),
        grid=(ni + 1, nj),
        in_specs=[
            pl.BlockSpec((slice_rows, k),
                         lambda i, j: (jnp.minimum(i, ni - 1) * NORM_SLICES + jnp.minimum(j, NORM_SLICES - 1), 0)),
            pl.BlockSpec((1, k), lambda i, j: (0, 0)),
            pl.BlockSpec((k, tn), lambda i, j: (0, jnp.where(i == 0, 0, j))),
            *cast_specs,
        ],
        out_specs=[pl.BlockSpec((tm, tn), lambda i, j: (jnp.maximum(i - 1, 0), jnp.where(i == 0, 0, j))),
                   *cast_specs],
        out_shape=[jax.ShapeDtypeStruct((t, n), BF16), *(jax.ShapeDtypeStruct(m.shape, BF16) for m in casts)],
        scratch_shapes=[pltpu.VMEM((tm, k), BF16), pltpu.VMEM((tm, k), BF16)],
        compiler_params=_params(("arbitrary", "arbitrary")),
        name=name,
    )(x, gain.reshape(1, k), w, *casts)
    return outs[0], tuple(outs[1:])


def _matmul_norm_res_kernel(a_ref, w_ref, xs_ref, g_ref, os_ref, acc_even_ref, acc_odd_ref, *, ni):
    i = pl.program_id(0)
    kk = pl.program_id(1)
    slice_rows = os_ref.shape[0]

    @pl.when((i == 0) & (kk == 0))
    def _():
        acc_even_ref[...] = jnp.zeros_like(acc_even_ref)
        acc_odd_ref[...] = jnp.zeros_like(acc_odd_ref)

    def accumulate(acc_ref):
        part = jnp.dot(a_ref[...], w_ref[...], preferred_element_type=F32)
        acc_ref[...] = jnp.where(kk == 0, part, acc_ref[...] + part)

    def finish_slice(acc_ref):
        gain = g_ref[...]
        for c in range(slice_rows // NORM_ROWS):
            rows = pl.ds(pl.multiple_of(kk * slice_rows + c * NORM_ROWS, NORM_ROWS), NORM_ROWS)
            y = acc_ref[rows, :]
            ms = jnp.mean(y * y, axis=-1, keepdims=True)
            local = pl.ds(c * NORM_ROWS, NORM_ROWS)
            os_ref[local, :] = xs_ref[local, :] + (y * lax.rsqrt(ms + NORM_EPS)) * gain

    @pl.when(i == 0)
    def _():
        accumulate(acc_even_ref)

    @pl.when((i > 0) & (i < ni) & (i % 2 == 1))
    def _():
        accumulate(acc_odd_ref)
        finish_slice(acc_even_ref)

    @pl.when((i > 0) & (i < ni) & (i % 2 == 0))
    def _():
        accumulate(acc_even_ref)
        finish_slice(acc_odd_ref)

    @pl.when(i == ni)
    def _():
        finish_slice(acc_odd_ref if (ni - 1) % 2 == 1 else acc_even_ref)


def _matmul_norm_res(a, w, x, gain, *, tm=512, tk, name):
    t, k = a.shape
    n = w.shape[1]
    ni, nk = t // tm, k // tk
    assert tm % (nk * NORM_ROWS) == 0
    slice_rows = tm // nk

    def slice_index(i, kk):
        return (jnp.maximum(i - 1, 0) * nk + jnp.where(i == 0, 0, kk), 0)

    return pl.pallas_call(
        functools.partial(_matmul_norm_res_kernel, ni=ni),
        grid=(ni + 1, nk),
        in_specs=[
            pl.BlockSpec((tm, tk), lambda i, kk: (jnp.minimum(i, ni - 1), jnp.where(i == ni, nk - 1, kk))),
            pl.BlockSpec((tk, n), lambda i, kk: (jnp.where(i == ni, nk - 1, kk), 0)),
            pl.BlockSpec((slice_rows, n), slice_index),
            pl.BlockSpec((1, n), lambda i, kk: (0, 0)),
        ],
        out_specs=pl.BlockSpec((slice_rows, n), slice_index),
        out_shape=jax.ShapeDtypeStruct((t, n), F32),
        scratch_shapes=[pltpu.VMEM((tm, n), F32), pltpu.VMEM((tm, n), F32)],
        compiler_params=_params(("arbitrary", "arbitrary"), V7X_VMEM_LIMIT_LARGE_BYTES),
        name=name,
    )(a, w, x, gain.reshape(1, n))


def _retention_kernel(ld_ref, q_ref, k_ref, v_ref, g_ref, cos_ref, sin_ref, gain_ref, o_ref,
                      qd_ref, y_ref, delta_ref, prev_ref, s_ref, kd_ref):
    seq = q_ref.shape[1]
    c = RET_CHUNK
    nc = seq // c
    grp = RET_CHUNK_GROUP
    h = pl.program_id(1)
    ld_f = ld_ref[0, h]
    ld_b = ld_ref[1, h]

    ri = lax.broadcasted_iota(jnp.int32, (c, c), 0)
    ci = lax.broadcasted_iota(jnp.int32, (c, c), 1)
    diff = (ri - ci).astype(F32)
    mask = jnp.where(diff >= 0, jnp.exp(ld_f * jnp.maximum(diff, 0.0)), jnp.exp(ld_b * jnp.maximum(-diff, 0.0)))
    pos = lax.broadcasted_iota(jnp.int32, (c, RET_QK_DIM), 0).astype(F32)
    qdec_f = jnp.exp(ld_f * (pos + 1.0))
    kdec_f = jnp.exp(ld_f * (c - 1.0 - pos))
    qdec_b = jnp.exp(ld_b * (c - pos))
    kdec_b = jnp.exp(ld_b * pos)
    cd_f = jnp.exp(jnp.full((1, RET_V_DIM), ld_f * c, F32))
    cd_b = jnp.exp(jnp.full((1, RET_V_DIM), ld_b * c, F32))
    gain = gain_ref[0]
    k_scale = RET_QK_DIM ** -0.5
    half = RET_QK_DIM // 2

    def chunk_rows(n):
        return pl.ds(pl.multiple_of(n * c, c), c)

    def phase_a(gi, carry):
        chunks = [gi * grp + j for j in range(grp)]
        for j, n in enumerate(chunks):
            rows = chunk_rows(n)
            cos = cos_ref[rows, :]
            sin = sin_ref[rows, :]
            xq = q_ref[0, rows, :].astype(F32)
            xk = k_ref[0, rows, :].astype(F32)
            q = xq * cos + pltpu.roll(xq, half, 1) * sin
            k = (xk * cos + pltpu.roll(xk, half, 1) * sin) * k_scale
            qd_ref[rows, :] = jnp.concatenate([(q * qdec_f).astype(BF16), (q * qdec_b).astype(BF16)], axis=1)
            kd_ref[j] = jnp.concatenate([(k * kdec_f).astype(BF16), (k * kdec_b).astype(BF16)], axis=1)
            s_ref[j] = lax.dot_general(q.astype(BF16), k.astype(BF16), (((1,), (1,)), ((), ())),
                                       preferred_element_type=F32)
        for j, n in enumerate(chunks):
            rows = chunk_rows(n)
            y_ref[rows, :] = jnp.dot((s_ref[j] * mask).astype(BF16), v_ref[0, rows, :],
                                     preferred_element_type=F32)
        for j, n in enumerate(chunks):
            delta_ref[n] = lax.dot_general(kd_ref[j], v_ref[0, chunk_rows(n), :], (((0,), (0,)), ((), ())),
                                           preferred_element_type=F32)
        return carry

    lax.fori_loop(0, nc // grp, phase_a, 0)

    def scan_f(n, st):
        prev_ref[n, pl.ds(0, RET_QK_DIM), :] = st.astype(BF16)
        return st * cd_f + delta_ref[n, pl.ds(0, RET_QK_DIM), :]

    def scan_b(t, st):
        n = nc - 1 - t
        prev_ref[n, pl.ds(RET_QK_DIM, RET_QK_DIM), :] = st.astype(BF16)
        return st * cd_b + delta_ref[n, pl.ds(RET_QK_DIM, RET_QK_DIM), :]

    zero_state = jnp.zeros((RET_QK_DIM, RET_V_DIM), F32)
    lax.fori_loop(0, nc, scan_f, zero_state)
    lax.fori_loop(0, nc, scan_b, zero_state)

    def phase_c(gi, carry):
        chunks = [gi * grp + j for j in range(grp)]
        ys = []
        for n in chunks:
            rows = chunk_rows(n)
            ys.append(y_ref[rows, :] + jnp.dot(qd_ref[rows, :], prev_ref[n], preferred_element_type=F32))
        for n, y in zip(chunks, ys):
            rows = chunk_rows(n)
            ms = jnp.mean(y * y, axis=-1, keepdims=True)
            yn = (y * lax.rsqrt(ms + NORM_EPS)) * gain
            gate = g_ref[0, rows, :].astype(F32)
            o_ref[0, rows, :] = ((gate * jax.nn.sigmoid(gate)) * yn).astype(o_ref.dtype)
        return carry

    lax.fori_loop(0, nc // grp, phase_c, 0)


def _retention(proj, log_decay, gain, cos2, sin2):
    b, seq, _ = proj.shape
    qk_blocks = RET_QK_WIDTH // RET_QK_DIM
    v_block0 = 2 * RET_QK_WIDTH // RET_V_DIM
    g_block0 = v_block0 + RET_HEADS
    nc = seq // RET_CHUNK
    return pl.pallas_call(
        _retention_kernel,
        grid=(b, RET_HEADS),
        in_specs=[
            pl.BlockSpec(memory_space=pltpu.SMEM),
            pl.BlockSpec((1, seq, RET_QK_DIM), lambda i, h: (i, 0, h)),
            pl.BlockSpec((1, seq, RET_QK_DIM), lambda i, h: (i, 0, qk_blocks + h)),
            pl.BlockSpec((1, seq, RET_V_DIM), lambda i, h: (i, 0, v_block0 + h)),
            pl.BlockSpec((1, seq, RET_V_DIM), lambda i, h: (i, 0, g_block0 + h)),
            pl.BlockSpec((seq, RET_QK_DIM), lambda i, h: (0, 0)),
            pl.BlockSpec((seq, RET_QK_DIM), lambda i, h: (0, 0)),
            pl.BlockSpec((1, 1, RET_V_DIM), lambda i, h: (h, 0, 0)),
        ],
        out_specs=pl.BlockSpec((1, seq, RET_V_DIM), lambda i, h: (i, 0, h)),
        out_shape=jax.ShapeDtypeStruct((b, seq, D_MODEL), BF16),
        scratch_shapes=[
            pltpu.VMEM((seq, 2 * RET_QK_DIM), BF16),
            pltpu.VMEM((seq, RET_V_DIM), F32),
            pltpu.VMEM((nc, 2 * RET_QK_DIM, RET_V_DIM), F32),
            pltpu.VMEM((nc, 2 * RET_QK_DIM, RET_V_DIM), BF16),
            pltpu.VMEM((RET_CHUNK_GROUP, RET_CHUNK, RET_CHUNK), F32),
            pltpu.VMEM((RET_CHUNK_GROUP, RET_CHUNK, 2 * RET_QK_DIM), BF16),
        ],
        compiler_params=_params(("parallel", "arbitrary")),
        name="retention",
    )(log_decay, proj, proj, proj, proj, cos2, sin2, gain.reshape(RET_HEADS, 1, RET_V_DIM))


COPY_ROWS = 128
RUN_NUM, RUN_MAX, RUN_DEN = 0, 1, 2
RUN_PARTS = 3


def _attention_kernel(q_ref, k_ref, v_ref, diag_ref, mix_ref, o_ref,
                      t_ref, stage_ref, stage4_ref, q1_ref, q4_ref, k4_ref, v4_ref, q16_ref, k16_ref, v16_ref,
                      run_a, run_b, s_ref, p_ref, den_ref, max_ref):
    del mix_ref
    seq = q_ref.shape[1]
    q_scale = ATT_HEAD_DIM ** -0.5
    len4 = seq // 4
    len16 = seq // 16

    @pl.when(pl.program_id(1) == 0)
    def _():
        for bi in range(len(DILATIONS)):
            wide = jnp.broadcast_to(diag_ref[0, bi], (ATT_QB, 2 * ATT_KW))
            for vi, off in enumerate(BIAS_OFFSETS):
                rolled = pltpu.roll(wide, ATT_KW - off, 1, stride=1, stride_axis=0)
                t_ref[bi, vi] = rolled[:, :ATT_KW]

    def stage(src_ref, scale, natural_ref):
        def body(cc, carry):
            rows = pl.ds(pl.multiple_of(cc * COPY_ROWS, COPY_ROWS), COPY_ROWS)
            x = src_ref[0, rows, :].astype(F32)
            if scale is not None:
                x = x * scale
                natural_ref[rows, :] = x.astype(BF16)
            stage_ref[rows, :] = x
            return carry

        lax.fori_loop(0, seq // COPY_ROWS, body, 0)

    def deinterleave(dst4_ref, dst16_ref):
        ch = min(COPY_ROWS, len4)
        for r in range(4):
            for cc in range(len4 // ch):
                x = stage_ref[pl.ds(r + 4 * cc * ch, ch, stride=4), :]
                rows = pl.ds(r * len4 + cc * ch, ch)
                stage4_ref[rows, :] = x
                dst4_ref[rows, :] = x.astype(BF16)
        ch = min(COPY_ROWS, len16)
        for r in range(4):
            for r_sub in range(4):
                for cc in range(len16 // ch):
                    x = stage4_ref[pl.ds(r * len4 + r_sub + 4 * cc * ch, ch, stride=4), :]
                    dst16_ref[pl.ds((r + 4 * r_sub) * len16 + cc * ch, ch), :] = x.astype(BF16)

    stage(q_ref, q_scale, q1_ref)
    deinterleave(q4_ref, q16_ref)
    stage(k_ref, None, None)
    deinterleave(k4_ref, k16_ref)
    stage(v_ref, None, None)
    deinterleave(v4_ref, v16_ref)

    def run_branch(bi, d, load_q, load_k, load_v, run_ref, first, last):
        length = seq // d
        kw = min(length, ATT_KW)
        nblk = length // ATT_QB

        def tile_rows(t):
            r = t // nblk
            nb = t - r * nblk
            i0 = nb * ATT_QB
            ws = jnp.clip(i0 - ATT_HALF, 0, length - kw)
            var = jnp.where(nb == 0, 0, jnp.where(nb == nblk - 1, 2, 1))
            base = r * length
            qrows = pl.ds(pl.multiple_of(base + i0, ATT_QB), ATT_QB)
            krows = pl.ds(pl.multiple_of(base + ws, ATT_HALF), kw)
            return qrows, krows, var

        def body(g, carry):
            tiles = [tile_rows(g * ATT_TILE_GROUP + j) for j in range(ATT_TILE_GROUP)]
            for j, (qrows, krows, var) in enumerate(tiles):
                s = lax.dot_general(load_q(qrows), load_k(krows), (((1,), (1,)), ((), ())),
                                    preferred_element_type=F32)
                s_ref[j, :, :kw] = s + t_ref[bi, var, :, :kw]
            for j, (qrows, krows, var) in enumerate(tiles):
                s = s_ref[j, :, :kw]
                m = jnp.max(s, axis=-1, keepdims=True)
                p = jnp.exp(s - m)
                den = jnp.sum(p, axis=-1, keepdims=True)
                p_ref[j, :, :kw] = p.astype(BF16)
                den_ref[j] = jnp.broadcast_to(den, (ATT_QB, ATT_HEAD_DIM))
                max_ref[j] = jnp.broadcast_to(m, (ATT_QB, ATT_HEAD_DIM))
            for j, (qrows, krows, var) in enumerate(tiles):
                num = jnp.dot(p_ref[j, :, :kw], load_v(krows), preferred_element_type=F32)
                den = den_ref[j]
                top = max_ref[j]
                if not first:
                    m_run = run_ref[RUN_MAX, qrows, :]
                    top = jnp.maximum(max_ref[j], m_run)
                    w_new = jnp.exp(max_ref[j] - top)
                    w_run = jnp.exp(m_run - top)
                    num = w_new * num + w_run * run_ref[RUN_NUM, qrows, :]
                    den = w_new * den + w_run * run_ref[RUN_DEN, qrows, :]
                if last:
                    o_ref[0, qrows, :] = (num / den).astype(o_ref.dtype)
                else:
                    run_ref[RUN_NUM, qrows, :] = num
                    run_ref[RUN_MAX, qrows, :] = top
                    run_ref[RUN_DEN, qrows, :] = den
            return carry

        lax.fori_loop(0, d * nblk // ATT_TILE_GROUP, body, 0)

    def regroup(src_ref, dst_ref, n_src, n_dst):
        len_src = seq // n_src
        len_dst = seq // n_dst
        ch = min(COPY_ROWS, len_src)
        for r_dst in range(n_dst):
            for r_sub in range(4):
                r_src = r_dst + n_dst * r_sub
                for cc in range(len_src // ch):
                    src = pl.ds(r_src * len_src + cc * ch, ch)
                    dst = pl.ds(r_dst * len_dst + r_sub + 4 * cc * ch, ch, stride=4)
                    for part in range(RUN_PARTS):
                        dst_ref[part, dst, :] = src_ref[part, src, :]

    run_branch(2, 16, lambda rows: q16_ref[rows, :], lambda rows: k16_ref[rows, :], lambda rows: v16_ref[rows, :],
               run_a, first=True, last=False)
    regroup(run_a, run_b, 16, 4)
    run_branch(1, 4, lambda rows: q4_ref[rows, :], lambda rows: k4_ref[rows, :], lambda rows: v4_ref[rows, :],
               run_b, first=False, last=False)
    regroup(run_b, run_a, 4, 1)
    run_branch(0, 1, lambda rows: q1_ref[rows, :], lambda rows: k_ref[0, rows, :], lambda rows: v_ref[0, rows, :],
               run_a, first=False, last=True)


def _attention(proj, bias_diags, mix):
    b, seq, _ = proj.shape
    q_block0 = (2 * RET_QK_WIDTH + 2 * RET_WIDTH) // ATT_HEAD_DIM
    k_block0 = q_block0 + ATT_HEADS
    v_block0 = k_block0 + ATT_HEADS
    out_block0 = RET_WIDTH // ATT_HEAD_DIM
    nb, nv = len(DILATIONS), len(BIAS_OFFSETS)
    head_rows = pltpu.VMEM((seq, ATT_HEAD_DIM), BF16)
    head_rows_f32 = pltpu.VMEM((seq, ATT_HEAD_DIM), F32)
    return pl.pallas_call(
        _attention_kernel,
        grid=(ATT_HEADS, b),
        in_specs=[
            pl.BlockSpec((1, seq, ATT_HEAD_DIM), lambda h, i: (i, 0, q_block0 + h)),
            pl.BlockSpec((1, seq, ATT_HEAD_DIM), lambda h, i: (i, 0, k_block0 + h)),
            pl.BlockSpec((1, seq, ATT_HEAD_DIM), lambda h, i: (i, 0, v_block0 + h)),
            pl.BlockSpec((1, nb, 1, 2 * ATT_KW), lambda h, i: (h, 0, 0, 0)),
            pl.BlockSpec(memory_space=pl.ANY),
        ],
        out_specs=pl.BlockSpec((1, seq, ATT_HEAD_DIM), lambda h, i: (i, 0, out_block0 + h)),
        out_shape=jax.ShapeDtypeStruct((b, seq, D_MODEL), BF16),
        scratch_shapes=[
            pltpu.VMEM((nb, nv, ATT_QB, ATT_KW), F32),
            head_rows_f32,
            head_rows_f32,
            head_rows,
            head_rows, head_rows, head_rows,
            head_rows, head_rows, head_rows,
            pltpu.VMEM((RUN_PARTS, seq, ATT_HEAD_DIM), F32),
            pltpu.VMEM((RUN_PARTS, seq, ATT_HEAD_DIM), F32),
            pltpu.VMEM((ATT_TILE_GROUP, ATT_QB, ATT_KW), F32),
            pltpu.VMEM((ATT_TILE_GROUP, ATT_QB, ATT_KW), BF16),
            pltpu.VMEM((ATT_TILE_GROUP, ATT_QB, ATT_HEAD_DIM), F32),
            pltpu.VMEM((ATT_TILE_GROUP, ATT_QB, ATT_HEAD_DIM), F32),
        ],
        input_output_aliases={4: 0},
        compiler_params=_params(("arbitrary", "arbitrary")),
        name="dilated_attention",
    )(proj, proj, proj, bias_diags, mix)


def _rotary_tables(seq):
    half = RET_QK_DIM // 2
    inv = ROPE_BASE ** (-jnp.arange(half, dtype=F32) / half)
    ang = jnp.arange(seq, dtype=F32)[:, None] * inv[None, :]
    cos, sin = jnp.cos(ang), jnp.sin(ang)
    return jnp.concatenate([cos, cos], axis=-1), jnp.concatenate([-sin, sin], axis=-1)


def _bucket_of(rel):
    nbk = REL_BUCKETS // 2
    max_exact = nbk // 2
    n = np.abs(rel)
    nf = np.maximum(n, 1).astype(np.float32)
    large = max_exact + (np.log(nf / max_exact) / math.log(REL_MAX_DISTANCE / max_exact)
                         * (nbk - max_exact)).astype(np.int32)
    large = np.minimum(large, nbk - 1)
    return np.where(rel > 0, nbk, 0) + np.where(n < max_exact, n, large)


def _bias_diags(rel_bias_table):
    delta = np.arange(2 * ATT_KW) - ATT_KW
    buckets = np.stack([_bucket_of(delta * d) for d in DILATIONS])
    inside = np.abs(delta) <= ATT_HALF
    bias = jnp.transpose(rel_bias_table.astype(F32))[:, buckets]
    return jnp.where(inside[None, None], bias, NEG_INF)[:, :, None, :]


def _trunk(x, bias_diags, layers, matrices, cast_ahead):
    b, seq, dm = x.shape
    cos2, sin2 = _rotary_tables(seq)
    xt = x.reshape(b * seq, dm)
    for l, layer in enumerate(layers):
        have = matrices[l]
        todo = [name for name in ("w_out", "w_up") if name not in have] if cast_ahead else []
        proj, done = _norm_matmul(xt, layer["norm_mix_pre"], have["w_in"], relu2=False,
                                  casts=[layer[name] for name in todo], name="in_proj")
        have.update(zip(todo, done))
        proj = proj.reshape(b, seq, IN_WIDTH)
        mix = _retention(proj, layer["log_decay"], layer["ret_norm_gain"], cos2, sin2)
        mix = _attention(proj, bias_diags, mix)
        xt = _matmul_norm_res(mix.reshape(b * seq, dm), have["w_out"], xt, layer["norm_mix_post"], tk=1024,
                              name="out_proj")
        todo = [(l, "w_down")] if cast_ahead and "w_down" not in have else []
        if cast_ahead and l + 1 < len(layers) and "w_in" not in matrices[l + 1]:
            todo.append((l + 1, "w_in"))
        u, done = _norm_matmul(xt, layer["norm_mlp_pre"], have["w_up"], relu2=True,
                               casts=[layers[m][name] for m, name in todo], name="up_proj")
        for (m, name), mat in zip(todo, done):
            matrices[m][name] = mat
        xt = _matmul_norm_res(u, have["w_down"], xt, layer["norm_mlp_post"], tk=2048, name="down_proj")
    return xt.reshape(b, seq, dm)


def kernel(x_prompt, x_sample, rel_bias_table, w_in, ret_log_decay, ret_norm_gain, w_out, w_up, w_down,
           norm_mix_pre, norm_mix_post, norm_mlp_pre, norm_mlp_post):
    depth = w_in.shape[0]
    layers = []
    for l in range(depth):
        layers.append({
            "w_in": w_in[l].astype(F32),
            "w_out": w_out[l].astype(F32),
            "w_up": w_up[l].astype(F32),
            "w_down": w_down[l].astype(F32),
            "log_decay": -jnp.exp(ret_log_decay[l].astype(F32)),
            "ret_norm_gain": ret_norm_gain[l].astype(F32),
            "norm_mix_pre": norm_mix_pre[l].astype(F32),
            "norm_mix_post": norm_mix_post[l].astype(F32),
            "norm_mlp_pre": norm_mlp_pre[l].astype(F32),
            "norm_mlp_post": norm_mlp_post[l].astype(F32),
        })
    matrices = [{} for _ in range(depth)]
    matrices[0]["w_in"] = layers[0]["w_in"].astype(BF16)
    bias_diags = _bias_diags(rel_bias_table)
    y_prompt = _trunk(x_prompt, bias_diags, layers, matrices, cast_ahead=True)
    y_sample = _trunk(x_sample, bias_diags, layers, matrices, cast_ahead=False)
    return (y_prompt, y_sample)
```

```python
import functools
import math

import numpy as np
import jax
import jax.numpy as jnp
from jax import lax
from jax.experimental import pallas as pl
from jax.experimental.pallas import tpu as pltpu

BF16 = jnp.bfloat16
F32 = jnp.float32

D_MODEL = 4096
RET_HEADS = 8
RET_QK_DIM = 128
RET_V_DIM = 256
RET_QK_WIDTH = RET_HEADS * RET_QK_DIM
RET_WIDTH = RET_HEADS * RET_V_DIM
ATT_HEADS = 16
ATT_HEAD_DIM = 128
ATT_WIDTH = ATT_HEADS * ATT_HEAD_DIM
IN_WIDTH = 2 * RET_QK_WIDTH + 2 * RET_WIDTH + 3 * ATT_WIDTH
RET_CHUNK = 128
RET_CHUNK_GROUP = 4
ROPE_BASE = 10000.0
DILATIONS = (1, 4, 16)
ATT_HALF = 64
REL_BUCKETS = 32
REL_MAX_DISTANCE = 1024
NORM_EPS = 1e-6
NEG_INF = -1e30

ATT_QB = 128
ATT_KW = 256
BIAS_OFFSETS = (0, -ATT_HALF, -2 * ATT_HALF)
ATT_TILE_GROUP = 8

V7X_VMEM_LIMIT_BYTES = 56 * 1024 * 1024
V7X_VMEM_LIMIT_LARGE_BYTES = 60 * 1024 * 1024


def _params(sem, vmem=V7X_VMEM_LIMIT_BYTES):
    return pltpu.CompilerParams(dimension_semantics=sem, vmem_limit_bytes=vmem)


NORM_ROWS = 16
NORM_SLICES = 8


def _norm_matmul_kernel(xs_ref, g_ref, w_ref, *refs, relu2, n_casts):
    cast_in_refs = refs[:n_casts]
    o_ref = refs[n_casts]
    cast_out_refs = refs[n_casts + 1:2 * n_casts + 1]
    h_even_ref, h_odd_ref = refs[2 * n_casts + 1:]
    i = pl.program_id(0)
    j = pl.program_id(1)
    slice_rows = xs_ref.shape[0]

    def cast_blocks():
        for src_ref, dst_ref in zip(cast_in_refs, cast_out_refs):
            dst_ref[...] = src_ref[...].astype(BF16)

    def normalise_slice(h_ref):
        gain = g_ref[...]
        base = jnp.minimum(j, NORM_SLICES - 1) * slice_rows
        for c in range(slice_rows // NORM_ROWS):
            x = xs_ref[pl.ds(c * NORM_ROWS, NORM_ROWS), :]
            ms = jnp.mean(x * x, axis=-1, keepdims=True)
            rows = pl.ds(pl.multiple_of(base + c * NORM_ROWS, NORM_ROWS), NORM_ROWS)
            h_ref[rows, :] = ((x * lax.rsqrt(ms + NORM_EPS)) * gain).astype(BF16)

    def multiply(h_ref):
        y = jnp.dot(h_ref[...], w_ref[...], preferred_element_type=F32)
        if relu2:
            y = jnp.maximum(y, 0.0)
            y = y * y
        o_ref[...] = y.astype(o_ref.dtype)

    @pl.when(i == 0)
    def _():
        cast_blocks()
        normalise_slice(h_even_ref)

    @pl.when((i > 0) & (i % 2 == 1))
    def _():
        cast_blocks()
        normalise_slice(h_odd_ref)
        multiply(h_even_ref)

    @pl.when((i > 0) & (i % 2 == 0))
    def _():
        cast_blocks()
        normalise_slice(h_even_ref)
        multiply(h_odd_ref)


def _norm_matmul(x, gain, w, *, relu2, casts=(), tm=1024, tn=1024, name):
    t, k = x.shape
    n = w.shape[1]
    ni, nj = t // tm, n // tn
    assert nj >= NORM_SLICES and tm % (NORM_SLICES * NORM_ROWS) == 0
    slice_rows = tm // NORM_SLICES
    steps = (ni + 1) * nj
    cast_in_specs, cast_out_specs = [], []
    for stack, layer in casts:
        _, rows, cols = stack.shape
        n_blocks = max(nb for nb in (8, 16, 32, 64, 128, 256, 512) if nb <= steps and rows % (nb * NORM_ROWS) == 0)
        block_rows = rows // n_blocks
        cast_in_specs.append(pl.BlockSpec(
            (None, block_rows, cols),
            lambda i, j, layer=layer, last=n_blocks - 1: (layer, jnp.minimum(i * nj + j, last), 0)))
        cast_out_specs.append(pl.BlockSpec(
            (block_rows, cols), lambda i, j, last=n_blocks - 1: (jnp.minimum(i * nj + j, last), 0)))
    outs = pl.pallas_call(
        functools.partial(_norm_matmul_kernel, relu2=relu2, n_casts=len(casts)),
        grid=(ni + 1, nj),
        in_specs=[
            pl.BlockSpec((slice_rows, k),
                         lambda i, j: (jnp.minimum(i, ni - 1) * NORM_SLICES + jnp.minimum(j, NORM_SLICES - 1), 0)),
            pl.BlockSpec((1, k), lambda i, j: (0, 0)),
            pl.BlockSpec((k, tn), lambda i, j: (0, jnp.where(i == 0, 0, j))),
            *cast_in_specs,
        ],
        out_specs=[pl.BlockSpec((tm, tn), lambda i, j: (jnp.maximum(i - 1, 0), jnp.where(i == 0, 0, j))),
                   *cast_out_specs],
        out_shape=[jax.ShapeDtypeStruct((t, n), BF16),
                   *(jax.ShapeDtypeStruct(stack.shape[1:], BF16) for stack, _ in casts)],
        scratch_shapes=[pltpu.VMEM((tm, k), BF16), pltpu.VMEM((tm, k), BF16)],
        compiler_params=_params(("arbitrary", "arbitrary")),
        name=name,
    )(x, gain.reshape(1, k), w, *(stack for stack, _ in casts))
    return outs[0], tuple(outs[1:])


def _matmul_norm_res_kernel(a_ref, w_ref, xs_ref, g_ref, os_ref, acc_even_ref, acc_odd_ref, *, ni):
    i = pl.program_id(0)
    kk = pl.program_id(1)
    slice_rows = os_ref.shape[0]

    @pl.when((i == 0) & (kk == 0))
    def _():
        acc_even_ref[...] = jnp.zeros_like(acc_even_ref)
        acc_odd_ref[...] = jnp.zeros_like(acc_odd_ref)

    def accumulate(acc_ref):
        part = jnp.dot(a_ref[...], w_ref[...], preferred_element_type=F32)
        acc_ref[...] = jnp.where(kk == 0, part, acc_ref[...] + part)

    def finish_slice(acc_ref):
        gain = g_ref[...]
        for c in range(slice_rows // NORM_ROWS):
            rows = pl.ds(pl.multiple_of(kk * slice_rows + c * NORM_ROWS, NORM_ROWS), NORM_ROWS)
            y = acc_ref[rows, :]
            ms = jnp.mean(y * y, axis=-1, keepdims=True)
            local = pl.ds(c * NORM_ROWS, NORM_ROWS)
            os_ref[local, :] = xs_ref[local, :] + (y * lax.rsqrt(ms + NORM_EPS)) * gain

    @pl.when(i == 0)
    def _():
        accumulate(acc_even_ref)

    @pl.when((i > 0) & (i < ni) & (i % 2 == 1))
    def _():
        accumulate(acc_odd_ref)
        finish_slice(acc_even_ref)

    @pl.when((i > 0) & (i < ni) & (i % 2 == 0))
    def _():
        accumulate(acc_even_ref)
        finish_slice(acc_odd_ref)

    @pl.when(i == ni)
    def _():
        finish_slice(acc_odd_ref if (ni - 1) % 2 == 1 else acc_even_ref)


def _matmul_norm_res(a, w, x, gain, *, tm=512, tk, name):
    t, k = a.shape
    n = w.shape[1]
    ni, nk = t // tm, k // tk
    assert tm % (nk * NORM_ROWS) == 0
    slice_rows = tm // nk

    def slice_index(i, kk):
        return (jnp.maximum(i - 1, 0) * nk + jnp.where(i == 0, 0, kk), 0)

    return pl.pallas_call(
        functools.partial(_matmul_norm_res_kernel, ni=ni),
        grid=(ni + 1, nk),
        in_specs=[
            pl.BlockSpec((tm, tk), lambda i, kk: (jnp.minimum(i, ni - 1), jnp.where(i == ni, nk - 1, kk))),
            pl.BlockSpec((tk, n), lambda i, kk: (jnp.where(i == ni, nk - 1, kk), 0)),
            pl.BlockSpec((slice_rows, n), slice_index),
            pl.BlockSpec((1, n), lambda i, kk: (0, 0)),
        ],
        out_specs=pl.BlockSpec((slice_rows, n), slice_index),
        out_shape=jax.ShapeDtypeStruct((t, n), F32),
        scratch_shapes=[pltpu.VMEM((tm, n), F32), pltpu.VMEM((tm, n), F32)],
        compiler_params=_params(("arbitrary", "arbitrary"), V7X_VMEM_LIMIT_LARGE_BYTES),
        name=name,
    )(a, w, x, gain.reshape(1, n))


def _retention_kernel(ld_ref, q_ref, k_ref, v_ref, g_ref, cos_ref, sin_ref, gain_ref, o_ref,
                      qd_ref, y_ref, delta_ref, prev_ref, s_ref, kd_ref):
    seq = q_ref.shape[1]
    c = RET_CHUNK
    nc = seq // c
    grp = RET_CHUNK_GROUP
    h = pl.program_id(1)
    ld_f = ld_ref[0, h]
    ld_b = ld_ref[1, h]

    ri = lax.broadcasted_iota(jnp.int32, (c, c), 0)
    ci = lax.broadcasted_iota(jnp.int32, (c, c), 1)
    diff = (ri - ci).astype(F32)
    mask = jnp.where(diff >= 0, jnp.exp(ld_f * jnp.maximum(diff, 0.0)), jnp.exp(ld_b * jnp.maximum(-diff, 0.0)))
    pos = lax.broadcasted_iota(jnp.int32, (c, RET_QK_DIM), 0).astype(F32)
    qdec_f = jnp.exp(ld_f * (pos + 1.0))
    kdec_f = jnp.exp(ld_f * (c - 1.0 - pos))
    qdec_b = jnp.exp(ld_b * (c - pos))
    kdec_b = jnp.exp(ld_b * pos)
    cd_f = jnp.exp(jnp.full((1, RET_V_DIM), ld_f * c, F32))
    cd_b = jnp.exp(jnp.full((1, RET_V_DIM), ld_b * c, F32))
    gain = gain_ref[0]
    k_scale = RET_QK_DIM ** -0.5
    half = RET_QK_DIM // 2

    def chunk_rows(n):
        return pl.ds(pl.multiple_of(n * c, c), c)

    def phase_a(gi, carry):
        chunks = [gi * grp + j for j in range(grp)]
        for j, n in enumerate(chunks):
            rows = chunk_rows(n)
            cos = cos_ref[rows, :]
            sin = sin_ref[rows, :]
            xq = q_ref[0, rows, :].astype(F32)
            xk = k_ref[0, rows, :].astype(F32)
            q = xq * cos + pltpu.roll(xq, half, 1) * sin
            k = (xk * cos + pltpu.roll(xk, half, 1) * sin) * k_scale
            qd_ref[rows, :] = jnp.concatenate([(q * qdec_f).astype(BF16), (q * qdec_b).astype(BF16)], axis=1)
            kd_ref[j] = jnp.concatenate([(k * kdec_f).astype(BF16), (k * kdec_b).astype(BF16)], axis=1)
            s_ref[j] = lax.dot_general(q.astype(BF16), k.astype(BF16), (((1,), (1,)), ((), ())),
                                       preferred_element_type=F32)
        for j, n in enumerate(chunks):
            rows = chunk_rows(n)
            y_ref[rows, :] = jnp.dot((s_ref[j] * mask).astype(BF16), v_ref[0, rows, :],
                                     preferred_element_type=F32)
        for j, n in enumerate(chunks):
            delta_ref[n] = lax.dot_general(kd_ref[j], v_ref[0, chunk_rows(n), :], (((0,), (0,)), ((), ())),
                                           preferred_element_type=F32)
        return carry

    lax.fori_loop(0, nc // grp, phase_a, 0)

    def scan_f(n, st):
        prev_ref[n, pl.ds(0, RET_QK_DIM), :] = st.astype(BF16)
        return st * cd_f + delta_ref[n, pl.ds(0, RET_QK_DIM), :]

    def scan_b(t, st):
        n = nc - 1 - t
        prev_ref[n, pl.ds(RET_QK_DIM, RET_QK_DIM), :] = st.astype(BF16)
        return st * cd_b + delta_ref[n, pl.ds(RET_QK_DIM, RET_QK_DIM), :]

    zero_state = jnp.zeros((RET_QK_DIM, RET_V_DIM), F32)
    lax.fori_loop(0, nc, scan_f, zero_state)
    lax.fori_loop(0, nc, scan_b, zero_state)

    def phase_c(gi, carry):
        chunks = [gi * grp + j for j in range(grp)]
        ys = []
        for n in chunks:
            rows = chunk_rows(n)
            ys.append(y_ref[rows, :] + jnp.dot(qd_ref[rows, :], prev_ref[n], preferred_element_type=F32))
        for n, y in zip(chunks, ys):
            rows = chunk_rows(n)
            ms = jnp.mean(y * y, axis=-1, keepdims=True)
            yn = (y * lax.rsqrt(ms + NORM_EPS)) * gain
            gate = g_ref[0, rows, :].astype(F32)
            o_ref[0, rows, :] = ((gate * jax.nn.sigmoid(gate)) * yn).astype(o_ref.dtype)
        return carry

    lax.fori_loop(0, nc // grp, phase_c, 0)


def _retention(proj, log_decay, gain, cos2, sin2):
    b, seq, _ = proj.shape
    qk_blocks = RET_QK_WIDTH // RET_QK_DIM
    v_block0 = 2 * RET_QK_WIDTH // RET_V_DIM
    g_block0 = v_block0 + RET_HEADS
    nc = seq // RET_CHUNK
    return pl.pallas_call(
        _retention_kernel,
        grid=(b, RET_HEADS),
        in_specs=[
            pl.BlockSpec(memory_space=pltpu.SMEM),
            pl.BlockSpec((1, seq, RET_QK_DIM), lambda i, h: (i, 0, h)),
            pl.BlockSpec((1, seq, RET_QK_DIM), lambda i, h: (i, 0, qk_blocks + h)),
            pl.BlockSpec((1, seq, RET_V_DIM), lambda i, h: (i, 0, v_block0 + h)),
            pl.BlockSpec((1, seq, RET_V_DIM), lambda i, h: (i, 0, g_block0 + h)),
            pl.BlockSpec((seq, RET_QK_DIM), lambda i, h: (0, 0)),
            pl.BlockSpec((seq, RET_QK_DIM), lambda i, h: (0, 0)),
            pl.BlockSpec((1, 1, RET_V_DIM), lambda i, h: (h, 0, 0)),
        ],
        out_specs=pl.BlockSpec((1, seq, RET_V_DIM), lambda i, h: (i, 0, h)),
        out_shape=jax.ShapeDtypeStruct((b, seq, D_MODEL), BF16),
        scratch_shapes=[
            pltpu.VMEM((seq, 2 * RET_QK_DIM), BF16),
            pltpu.VMEM((seq, RET_V_DIM), F32),
            pltpu.VMEM((nc, 2 * RET_QK_DIM, RET_V_DIM), F32),
            pltpu.VMEM((nc, 2 * RET_QK_DIM, RET_V_DIM), BF16),
            pltpu.VMEM((RET_CHUNK_GROUP, RET_CHUNK, RET_CHUNK), F32),
            pltpu.VMEM((RET_CHUNK_GROUP, RET_CHUNK, 2 * RET_QK_DIM), BF16),
        ],
        compiler_params=_params(("parallel", "arbitrary")),
        name="retention",
    )(log_decay, proj, proj, proj, proj, cos2, sin2, gain.reshape(RET_HEADS, 1, RET_V_DIM))


COPY_ROWS = 128
RUN_NUM, RUN_MAX, RUN_DEN = 0, 1, 2
RUN_PARTS = 3


def _attention_kernel(q_ref, k_ref, v_ref, diag_ref, mix_ref, o_ref,
                      t_ref, stage_ref, stage4_ref, q1_ref, q4_ref, k4_ref, v4_ref, q16_ref, k16_ref, v16_ref,
                      run_a, run_b, s_ref, p_ref, den_ref, max_ref):
    del mix_ref
    seq = q_ref.shape[1]
    q_scale = ATT_HEAD_DIM ** -0.5
    len4 = seq // 4
    len16 = seq // 16

    @pl.when(pl.program_id(1) == 0)
    def _():
        for bi in range(len(DILATIONS)):
            wide = jnp.broadcast_to(diag_ref[0, bi], (ATT_QB, 2 * ATT_KW))
            for vi, off in enumerate(BIAS_OFFSETS):
                rolled = pltpu.roll(wide, ATT_KW - off, 1, stride=1, stride_axis=0)
                t_ref[bi, vi] = rolled[:, :ATT_KW]

    def stage(src_ref, scale, natural_ref):
        def body(cc, carry):
            rows = pl.ds(pl.multiple_of(cc * COPY_ROWS, COPY_ROWS), COPY_ROWS)
            x = src_ref[0, rows, :].astype(F32)
            if scale is not None:
                x = x * scale
                natural_ref[rows, :] = x.astype(BF16)
            stage_ref[rows, :] = x
            return carry

        lax.fori_loop(0, seq // COPY_ROWS, body, 0)

    def deinterleave(dst4_ref, dst16_ref):
        ch = min(COPY_ROWS, len4)
        for r in range(4):
            for cc in range(len4 // ch):
                x = stage_ref[pl.ds(r + 4 * cc * ch, ch, stride=4), :]
                rows = pl.ds(r * len4 + cc * ch, ch)
                stage4_ref[rows, :] = x
                dst4_ref[rows, :] = x.astype(BF16)
        ch = min(COPY_ROWS, len16)
        for r in range(4):
            for r_sub in range(4):
                for cc in range(len16 // ch):
                    x = stage4_ref[pl.ds(r * len4 + r_sub + 4 * cc * ch, ch, stride=4), :]
                    dst16_ref[pl.ds((r + 4 * r_sub) * len16 + cc * ch, ch), :] = x.astype(BF16)

    stage(q_ref, q_scale, q1_ref)
    deinterleave(q4_ref, q16_ref)
    stage(k_ref, None, None)
    deinterleave(k4_ref, k16_ref)
    stage(v_ref, None, None)
    deinterleave(v4_ref, v16_ref)

    def run_branch(bi, d, load_q, load_k, load_v, run_ref, first, last):
        length = seq // d
        kw = min(length, ATT_KW)
        nblk = length // ATT_QB

        def tile_rows(t):
            r = t // nblk
            nb = t - r * nblk
            i0 = nb * ATT_QB
            ws = jnp.clip(i0 - ATT_HALF, 0, length - kw)
            var = jnp.where(nb == 0, 0, jnp.where(nb == nblk - 1, 2, 1))
            base = r * length
            qrows = pl.ds(pl.multiple_of(base + i0, ATT_QB), ATT_QB)
            krows = pl.ds(pl.multiple_of(base + ws, ATT_HALF), kw)
            return qrows, krows, var

        def body(g, carry):
            tiles = [tile_rows(g * ATT_TILE_GROUP + j) for j in range(ATT_TILE_GROUP)]
            for j, (qrows, krows, var) in enumerate(tiles):
                s = lax.dot_general(load_q(qrows), load_k(krows), (((1,), (1,)), ((), ())),
                                    preferred_element_type=F32)
                s_ref[j, :, :kw] = s + t_ref[bi, var, :, :kw]
            for j, (qrows, krows, var) in enumerate(tiles):
                s = s_ref[j, :, :kw]
                m = jnp.max(s, axis=-1, keepdims=True)
                p = jnp.exp(s - m)
                den = jnp.sum(p, axis=-1, keepdims=True)
                p_ref[j, :, :kw] = p.astype(BF16)
                den_ref[j] = jnp.broadcast_to(den, (ATT_QB, ATT_HEAD_DIM))
                max_ref[j] = jnp.broadcast_to(m, (ATT_QB, ATT_HEAD_DIM))
            for j, (qrows, krows, var) in enumerate(tiles):
                num = jnp.dot(p_ref[j, :, :kw], load_v(krows), preferred_element_type=F32)
                den = den_ref[j]
                top = max_ref[j]
                if not first:
                    m_run = run_ref[RUN_MAX, qrows, :]
                    top = jnp.maximum(max_ref[j], m_run)
                    w_new = jnp.exp(max_ref[j] - top)
                    w_run = jnp.exp(m_run - top)
                    num = w_new * num + w_run * run_ref[RUN_NUM, qrows, :]
                    den = w_new * den + w_run * run_ref[RUN_DEN, qrows, :]
                if last:
                    o_ref[0, qrows, :] = (num / den).astype(o_ref.dtype)
                else:
                    run_ref[RUN_NUM, qrows, :] = num
                    run_ref[RUN_MAX, qrows, :] = top
                    run_ref[RUN_DEN, qrows, :] = den
            return carry

        lax.fori_loop(0, d * nblk // ATT_TILE_GROUP, body, 0)

    def regroup(src_ref, dst_ref, n_src, n_dst):
        len_src = seq // n_src
        len_dst = seq // n_dst
        ch = min(COPY_ROWS, len_src)
        for r_dst in range(n_dst):
            for r_sub in range(4):
                r_src = r_dst + n_dst * r_sub
                for cc in range(len_src // ch):
                    src = pl.ds(r_src * len_src + cc * ch, ch)
                    dst = pl.ds(r_dst * len_dst + r_sub + 4 * cc * ch, ch, stride=4)
                    for part in range(RUN_PARTS):
                        dst_ref[part, dst, :] = src_ref[part, src, :]

    run_branch(2, 16, lambda rows: q16_ref[rows, :], lambda rows: k16_ref[rows, :], lambda rows: v16_ref[rows, :],
               run_a, first=True, last=False)
    regroup(run_a, run_b, 16, 4)
    run_branch(1, 4, lambda rows: q4_ref[rows, :], lambda rows: k4_ref[rows, :], lambda rows: v4_ref[rows, :],
               run_b, first=False, last=False)
    regroup(run_b, run_a, 4, 1)
    run_branch(0, 1, lambda rows: q1_ref[rows, :], lambda rows: k_ref[0, rows, :], lambda rows: v_ref[0, rows, :],
               run_a, first=False, last=True)


def _attention(proj, bias_diags, mix):
    b, seq, _ = proj.shape
    q_block0 = (2 * RET_QK_WIDTH + 2 * RET_WIDTH) // ATT_HEAD_DIM
    k_block0 = q_block0 + ATT_HEADS
    v_block0 = k_block0 + ATT_HEADS
    out_block0 = RET_WIDTH // ATT_HEAD_DIM
    nb, nv = len(DILATIONS), len(BIAS_OFFSETS)
    head_rows = pltpu.VMEM((seq, ATT_HEAD_DIM), BF16)
    head_rows_f32 = pltpu.VMEM((seq, ATT_HEAD_DIM), F32)
    return pl.pallas_call(
        _attention_kernel,
        grid=(ATT_HEADS, b),
        in_specs=[
            pl.BlockSpec((1, seq, ATT_HEAD_DIM), lambda h, i: (i, 0, q_block0 + h)),
            pl.BlockSpec((1, seq, ATT_HEAD_DIM), lambda h, i: (i, 0, k_block0 + h)),
            pl.BlockSpec((1, seq, ATT_HEAD_DIM), lambda h, i: (i, 0, v_block0 + h)),
            pl.BlockSpec((1, nb, 1, 2 * ATT_KW), lambda h, i: (h, 0, 0, 0)),
            pl.BlockSpec(memory_space=pl.ANY),
        ],
        out_specs=pl.BlockSpec((1, seq, ATT_HEAD_DIM), lambda h, i: (i, 0, out_block0 + h)),
        out_shape=jax.ShapeDtypeStruct((b, seq, D_MODEL), BF16),
        scratch_shapes=[
            pltpu.VMEM((nb, nv, ATT_QB, ATT_KW), F32),
            head_rows_f32,
            head_rows_f32,
            head_rows,
            head_rows, head_rows, head_rows,
            head_rows, head_rows, head_rows,
            pltpu.VMEM((RUN_PARTS, seq, ATT_HEAD_DIM), F32),
            pltpu.VMEM((RUN_PARTS, seq, ATT_HEAD_DIM), F32),
            pltpu.VMEM((ATT_TILE_GROUP, ATT_QB, ATT_KW), F32),
            pltpu.VMEM((ATT_TILE_GROUP, ATT_QB, ATT_KW), BF16),
            pltpu.VMEM((ATT_TILE_GROUP, ATT_QB, ATT_HEAD_DIM), F32),
            pltpu.VMEM((ATT_TILE_GROUP, ATT_QB, ATT_HEAD_DIM), F32),
        ],
        input_output_aliases={4: 0},
        compiler_params=_params(("arbitrary", "arbitrary")),
        name="dilated_attention",
    )(proj, proj, proj, bias_diags, mix)


def _rotary_tables(seq):
    half = RET_QK_DIM // 2
    inv = ROPE_BASE ** (-jnp.arange(half, dtype=F32) / half)
    ang = jnp.arange(seq, dtype=F32)[:, None] * inv[None, :]
    cos, sin = jnp.cos(ang), jnp.sin(ang)
    return jnp.concatenate([cos, cos], axis=-1), jnp.concatenate([-sin, sin], axis=-1)


def _bucket_of(rel):
    nbk = REL_BUCKETS // 2
    max_exact = nbk // 2
    n = np.abs(rel)
    nf = np.maximum(n, 1).astype(np.float32)
    large = max_exact + (np.log(nf / max_exact) / math.log(REL_MAX_DISTANCE / max_exact)
                         * (nbk - max_exact)).astype(np.int32)
    large = np.minimum(large, nbk - 1)
    return np.where(rel > 0, nbk, 0) + np.where(n < max_exact, n, large)


def _bias_diags(rel_bias_table):
    delta = np.arange(2 * ATT_KW) - ATT_KW
    buckets = np.stack([_bucket_of(delta * d) for d in DILATIONS])
    inside = np.abs(delta) <= ATT_HALF
    bias = jnp.transpose(rel_bias_table.astype(F32))[:, buckets]
    return jnp.where(inside[None, None], bias, NEG_INF)[:, :, None, :]


def _trunk(x, bias_diags, layers, stacks, matrices, cast_ahead):
    b, seq, dm = x.shape
    cos2, sin2 = _rotary_tables(seq)
    xt = x.reshape(b * seq, dm)
    for l, layer in enumerate(layers):
        have = matrices[l]
        todo = [name for name in ("w_out", "w_up") if name not in have] if cast_ahead else []
        proj, done = _norm_matmul(xt, layer["norm_mix_pre"], have["w_in"], relu2=False,
                                  casts=[(stacks[name], l) for name in todo], name="in_proj")
        have.update(zip(todo, done))
        proj = proj.reshape(b, seq, IN_WIDTH)
        mix = _retention(proj, layer["log_decay"], layer["ret_norm_gain"], cos2, sin2)
        mix = _attention(proj, bias_diags, mix)
        xt = _matmul_norm_res(mix.reshape(b * seq, dm), have["w_out"], xt, layer["norm_mix_post"], tk=1024,
                              name="out_proj")
        todo = [(l, "w_down")] if cast_ahead and "w_down" not in have else []
        if cast_ahead and l + 1 < len(layers) and "w_in" not in matrices[l + 1]:
            todo.append((l + 1, "w_in"))
        u, done = _norm_matmul(xt, layer["norm_mlp_pre"], have["w_up"], relu2=True,
                               casts=[(stacks[name], m) for m, name in todo], name="up_proj")
        for (m, name), mat in zip(todo, done):
            matrices[m][name] = mat
        xt = _matmul_norm_res(u, have["w_down"], xt, layer["norm_mlp_post"], tk=2048, name="down_proj")
    return xt.reshape(b, seq, dm)


def kernel(x_prompt, x_sample, rel_bias_table, w_in, ret_log_decay, ret_norm_gain, w_out, w_up, w_down,
           norm_mix_pre, norm_mix_post, norm_mlp_pre, norm_mlp_post):
    depth = w_in.shape[0]
    stacks = {"w_in": w_in.astype(F32), "w_out": w_out.astype(F32), "w_up": w_up.astype(F32),
              "w_down": w_down.astype(F32)}
    layers = []
    for l in range(depth):
        layers.append({
            "log_decay": -jnp.exp(ret_log_decay[l].astype(F32)),
            "ret_norm_gain": ret_norm_gain[l].astype(F32),
            "norm_mix_pre": norm_mix_pre[l].astype(F32),
            "norm_mix_post": norm_mix_post[l].astype(F32),
            "norm_mlp_pre": norm_mlp_pre[l].astype(F32),
            "norm_mlp_post": norm_mlp_post[l].astype(F32),
        })
    matrices = [{} for _ in range(depth)]
    matrices[0]["w_in"] = stacks["w_in"][0].astype(BF16)
    bias_diags = _bias_diags(rel_bias_table)
    y_prompt = _trunk(x_prompt, bias_diags, layers, stacks, matrices, cast_ahead=True)
    y_sample = _trunk(x_sample, bias_diags, layers, stacks, matrices, cast_ahead=False)
    return (y_prompt, y_sample)
```

```python
import functools
import math

import numpy as np
import jax
import jax.numpy as jnp
from jax import lax
from jax.experimental import pallas as pl
from jax.experimental.pallas import tpu as pltpu

BF16 = jnp.bfloat16
F32 = jnp.float32

D_MODEL = 4096
RET_HEADS = 8
RET_QK_DIM = 128
RET_V_DIM = 256
RET_QK_WIDTH = RET_HEADS * RET_QK_DIM
RET_WIDTH = RET_HEADS * RET_V_DIM
ATT_HEADS = 16
ATT_HEAD_DIM = 128
ATT_WIDTH = ATT_HEADS * ATT_HEAD_DIM
IN_WIDTH = 2 * RET_QK_WIDTH + 2 * RET_WIDTH + 3 * ATT_WIDTH
RET_CHUNK = 128
RET_CHUNK_GROUP = 8
ROPE_BASE = 10000.0
DILATIONS = (1, 4, 16)
ATT_HALF = 64
REL_BUCKETS = 32
REL_MAX_DISTANCE = 1024
NORM_EPS = 1e-6
NEG_INF = -1e30

ATT_QB = 128
ATT_KW = 256
BIAS_OFFSETS = (0, -ATT_HALF, -2 * ATT_HALF)
ATT_TILE_GROUP = 16

V7X_VMEM_LIMIT_BYTES = 56 * 1024 * 1024
V7X_VMEM_LIMIT_LARGE_BYTES = 60 * 1024 * 1024


def _params(sem, vmem=V7X_VMEM_LIMIT_BYTES):
    return pltpu.CompilerParams(dimension_semantics=sem, vmem_limit_bytes=vmem)


NORM_ROWS = 16
NORM_SLICES = 8


def _norm_matmul_kernel(xs_ref, g_ref, w_ref, *refs, relu2, n_casts):
    cast_in_refs = refs[:n_casts]
    o_ref = refs[n_casts]
    cast_out_refs = refs[n_casts + 1:2 * n_casts + 1]
    h_even_ref, h_odd_ref = refs[2 * n_casts + 1:]
    i = pl.program_id(0)
    j = pl.program_id(1)
    slice_rows = xs_ref.shape[0]

    def cast_blocks():
        for src_ref, dst_ref in zip(cast_in_refs, cast_out_refs):
            dst_ref[...] = src_ref[...].astype(BF16)

    def normalise_slice(h_ref):
        gain = g_ref[...]
        base = jnp.minimum(j, NORM_SLICES - 1) * slice_rows
        for c in range(slice_rows // NORM_ROWS):
            x = xs_ref[pl.ds(c * NORM_ROWS, NORM_ROWS), :]
            ms = jnp.mean(x * x, axis=-1, keepdims=True)
            rows = pl.ds(pl.multiple_of(base + c * NORM_ROWS, NORM_ROWS), NORM_ROWS)
            h_ref[rows, :] = ((x * lax.rsqrt(ms + NORM_EPS)) * gain).astype(BF16)

    def multiply(h_ref):
        y = jnp.dot(h_ref[...], w_ref[...], preferred_element_type=F32)
        if relu2:
            y = jnp.maximum(y, 0.0)
            y = y * y
        o_ref[...] = y.astype(o_ref.dtype)

    @pl.when(i == 0)
    def _():
        cast_blocks()
        normalise_slice(h_even_ref)

    for parity, h_write_ref, h_read_ref in ((1, h_odd_ref, h_even_ref), (0, h_even_ref, h_odd_ref)):
        for slicing in (True, False):
            @pl.when((i > 0) & (i % 2 == parity) & ((j < NORM_SLICES) == slicing))
            def _(h_write_ref=h_write_ref, h_read_ref=h_read_ref, slicing=slicing):
                cast_blocks()
                if slicing:
                    normalise_slice(h_write_ref)
                multiply(h_read_ref)


def _norm_matmul(x, gain, w, *, relu2, casts=(), tm=1024, tn=1024, name):
    t, k = x.shape
    n = w.shape[1]
    ni, nj = t // tm, n // tn
    assert nj >= NORM_SLICES and tm % (NORM_SLICES * NORM_ROWS) == 0
    slice_rows = tm // NORM_SLICES
    steps = (ni + 1) * nj
    cast_in_specs, cast_out_specs = [], []
    for stack, layer in casts:
        _, rows, cols = stack.shape
        n_blocks = max(nb for nb in (8, 16, 32, 64, 128, 256, 512) if nb <= steps and rows % (nb * NORM_ROWS) == 0)
        block_rows = rows // n_blocks
        cast_in_specs.append(pl.BlockSpec(
            (None, block_rows, cols),
            lambda i, j, layer=layer, last=n_blocks - 1: (layer, jnp.minimum(i * nj + j, last), 0)))
        cast_out_specs.append(pl.BlockSpec(
            (block_rows, cols), lambda i, j, last=n_blocks - 1: (jnp.minimum(i * nj + j, last), 0)))
    outs = pl.pallas_call(
        functools.partial(_norm_matmul_kernel, relu2=relu2, n_casts=len(casts)),
        grid=(ni + 1, nj),
        in_specs=[
            pl.BlockSpec((slice_rows, k),
                         lambda i, j: (jnp.minimum(i, ni - 1) * NORM_SLICES + jnp.minimum(j, NORM_SLICES - 1), 0)),
            pl.BlockSpec((1, k), lambda i, j: (0, 0)),
            pl.BlockSpec((k, tn), lambda i, j: (0, jnp.where(i == 0, 0, j))),
            *cast_in_specs,
        ],
        out_specs=[pl.BlockSpec((tm, tn), lambda i, j: (jnp.maximum(i - 1, 0), jnp.where(i == 0, 0, j))),
                   *cast_out_specs],
        out_shape=[jax.ShapeDtypeStruct((t, n), BF16),
                   *(jax.ShapeDtypeStruct(stack.shape[1:], BF16) for stack, _ in casts)],
        scratch_shapes=[pltpu.VMEM((tm, k), BF16), pltpu.VMEM((tm, k), BF16)],
        compiler_params=_params(("arbitrary", "arbitrary")),
        name=name,
    )(x, gain.reshape(1, k), w, *(stack for stack, _ in casts))
    return outs[0], tuple(outs[1:])


def _matmul_norm_res_kernel(a_ref, w_ref, xs_ref, g_ref, os_ref, acc_even_ref, acc_odd_ref, *, ni):
    i = pl.program_id(0)
    kk = pl.program_id(1)
    slice_rows = os_ref.shape[0]

    @pl.when((i == 0) & (kk == 0))
    def _():
        acc_even_ref[...] = jnp.zeros_like(acc_even_ref)
        acc_odd_ref[...] = jnp.zeros_like(acc_odd_ref)

    def accumulate(acc_ref):
        part = jnp.dot(a_ref[...], w_ref[...], preferred_element_type=F32)
        acc_ref[...] = jnp.where(kk == 0, part, acc_ref[...] + part)

    def finish_slice(acc_ref):
        gain = g_ref[...]
        for c in range(slice_rows // NORM_ROWS):
            rows = pl.ds(pl.multiple_of(kk * slice_rows + c * NORM_ROWS, NORM_ROWS), NORM_ROWS)
            y = acc_ref[rows, :]
            ms = jnp.mean(y * y, axis=-1, keepdims=True)
            local = pl.ds(c * NORM_ROWS, NORM_ROWS)
            os_ref[local, :] = xs_ref[local, :] + (y * lax.rsqrt(ms + NORM_EPS)) * gain

    @pl.when(i == 0)
    def _():
        accumulate(acc_even_ref)

    @pl.when((i > 0) & (i < ni) & (i % 2 == 1))
    def _():
        accumulate(acc_odd_ref)
        finish_slice(acc_even_ref)

    @pl.when((i > 0) & (i < ni) & (i % 2 == 0))
    def _():
        accumulate(acc_even_ref)
        finish_slice(acc_odd_ref)

    @pl.when(i == ni)
    def _():
        finish_slice(acc_odd_ref if (ni - 1) % 2 == 1 else acc_even_ref)


def _matmul_norm_res(a, w, x, gain, *, tm=512, tk, name):
    t, k = a.shape
    n = w.shape[1]
    ni, nk = t // tm, k // tk
    assert tm % (nk * NORM_ROWS) == 0
    slice_rows = tm // nk

    def slice_index(i, kk):
        return (jnp.maximum(i - 1, 0) * nk + jnp.where(i == 0, 0, kk), 0)

    return pl.pallas_call(
        functools.partial(_matmul_norm_res_kernel, ni=ni),
        grid=(ni + 1, nk),
        in_specs=[
            pl.BlockSpec((tm, tk), lambda i, kk: (jnp.minimum(i, ni - 1), jnp.where(i == ni, nk - 1, kk))),
            pl.BlockSpec((tk, n), lambda i, kk: (jnp.where(i == ni, nk - 1, kk), 0)),
            pl.BlockSpec((slice_rows, n), slice_index),
            pl.BlockSpec((1, n), lambda i, kk: (0, 0)),
        ],
        out_specs=pl.BlockSpec((slice_rows, n), slice_index),
        out_shape=jax.ShapeDtypeStruct((t, n), F32),
        scratch_shapes=[pltpu.VMEM((tm, n), F32), pltpu.VMEM((tm, n), F32)],
        compiler_params=_params(("arbitrary", "arbitrary"), V7X_VMEM_LIMIT_LARGE_BYTES),
        name=name,
    )(a, w, x, gain.reshape(1, n))


def _retention_kernel(ld_ref, q_ref, k_ref, v_ref, g_ref, cos_ref, sin_ref, gain_ref, o_ref,
                      qd_ref, y_ref, delta_ref, prev_ref, s_ref, kd_ref):
    seq = q_ref.shape[1]
    c = RET_CHUNK
    nc = seq // c
    grp = RET_CHUNK_GROUP
    h = pl.program_id(1)
    ld_f = ld_ref[0, h]
    ld_b = ld_ref[1, h]

    ri = lax.broadcasted_iota(jnp.int32, (c, c), 0)
    ci = lax.broadcasted_iota(jnp.int32, (c, c), 1)
    diff = (ri - ci).astype(F32)
    mask = jnp.where(diff >= 0, jnp.exp(ld_f * jnp.maximum(diff, 0.0)), jnp.exp(ld_b * jnp.maximum(-diff, 0.0)))
    pos = lax.broadcasted_iota(jnp.int32, (c, RET_QK_DIM), 0).astype(F32)
    qdec_f = jnp.exp(ld_f * (pos + 1.0))
    kdec_f = jnp.exp(ld_f * (c - 1.0 - pos))
    qdec_b = jnp.exp(ld_b * (c - pos))
    kdec_b = jnp.exp(ld_b * pos)
    cd_f = jnp.exp(jnp.full((1, RET_V_DIM), ld_f * c, F32))
    cd_b = jnp.exp(jnp.full((1, RET_V_DIM), ld_b * c, F32))
    gain = gain_ref[0]
    k_scale = RET_QK_DIM ** -0.5
    half = RET_QK_DIM // 2

    def chunk_rows(n):
        return pl.ds(pl.multiple_of(n * c, c), c)

    def phase_a(gi, carry):
        chunks = [gi * grp + j for j in range(grp)]
        for j, n in enumerate(chunks):
            rows = chunk_rows(n)
            cos = cos_ref[rows, :]
            sin = sin_ref[rows, :]
            xq = q_ref[0, rows, :].astype(F32)
            xk = k_ref[0, rows, :].astype(F32)
            q = xq * cos + pltpu.roll(xq, half, 1) * sin
            k = (xk * cos + pltpu.roll(xk, half, 1) * sin) * k_scale
            qd_ref[rows, :] = jnp.concatenate([(q * qdec_f).astype(BF16), (q * qdec_b).astype(BF16)], axis=1)
            kd_ref[j] = jnp.concatenate([(k * kdec_f).astype(BF16), (k * kdec_b).astype(BF16)], axis=1)
            s_ref[j] = lax.dot_general(q.astype(BF16), k.astype(BF16), (((1,), (1,)), ((), ())),
                                       preferred_element_type=F32)
        for j, n in enumerate(chunks):
            rows = chunk_rows(n)
            y_ref[rows, :] = jnp.dot((s_ref[j] * mask).astype(BF16), v_ref[0, rows, :],
                                     preferred_element_type=F32)
        for j, n in enumerate(chunks):
            delta_ref[n] = lax.dot_general(kd_ref[j], v_ref[0, chunk_rows(n), :], (((0,), (0,)), ((), ())),
                                           preferred_element_type=F32)
        return carry

    lax.fori_loop(0, nc // grp, phase_a, 0)

    def scan_f(n, st):
        prev_ref[n, pl.ds(0, RET_QK_DIM), :] = st.astype(BF16)
        return st * cd_f + delta_ref[n, pl.ds(0, RET_QK_DIM), :]

    def scan_b(t, st):
        n = nc - 1 - t
        prev_ref[n, pl.ds(RET_QK_DIM, RET_QK_DIM), :] = st.astype(BF16)
        return st * cd_b + delta_ref[n, pl.ds(RET_QK_DIM, RET_QK_DIM), :]

    zero_state = jnp.zeros((RET_QK_DIM, RET_V_DIM), F32)
    lax.fori_loop(0, nc, scan_f, zero_state)
    lax.fori_loop(0, nc, scan_b, zero_state)

    def phase_c(gi, carry):
        chunks = [gi * grp + j for j in range(grp)]
        ys = []
        for n in chunks:
            rows = chunk_rows(n)
            ys.append(y_ref[rows, :] + jnp.dot(qd_ref[rows, :], prev_ref[n], preferred_element_type=F32))
        for n, y in zip(chunks, ys):
            rows = chunk_rows(n)
            ms = jnp.mean(y * y, axis=-1, keepdims=True)
            yn = (y * lax.rsqrt(ms + NORM_EPS)) * gain
            gate = g_ref[0, rows, :].astype(F32)
            o_ref[0, rows, :] = ((gate * jax.nn.sigmoid(gate)) * yn).astype(o_ref.dtype)
        return carry

    lax.fori_loop(0, nc // grp, phase_c, 0)


def _retention(proj, log_decay, gain, cos2, sin2):
    b, seq, _ = proj.shape
    qk_blocks = RET_QK_WIDTH // RET_QK_DIM
    v_block0 = 2 * RET_QK_WIDTH // RET_V_DIM
    g_block0 = v_block0 + RET_HEADS
    nc = seq // RET_CHUNK
    return pl.pallas_call(
        _retention_kernel,
        grid=(b, RET_HEADS),
        in_specs=[
            pl.BlockSpec(memory_space=pltpu.SMEM),
            pl.BlockSpec((1, seq, RET_QK_DIM), lambda i, h: (i, 0, h)),
            pl.BlockSpec((1, seq, RET_QK_DIM), lambda i, h: (i, 0, qk_blocks + h)),
            pl.BlockSpec((1, seq, RET_V_DIM), lambda i, h: (i, 0, v_block0 + h)),
            pl.BlockSpec((1, seq, RET_V_DIM), lambda i, h: (i, 0, g_block0 + h)),
            pl.BlockSpec((seq, RET_QK_DIM), lambda i, h: (0, 0)),
            pl.BlockSpec((seq, RET_QK_DIM), lambda i, h: (0, 0)),
            pl.BlockSpec((1, 1, RET_V_DIM), lambda i, h: (h, 0, 0)),
        ],
        out_specs=pl.BlockSpec((1, seq, RET_V_DIM), lambda i, h: (i, 0, h)),
        out_shape=jax.ShapeDtypeStruct((b, seq, D_MODEL), BF16),
        scratch_shapes=[
            pltpu.VMEM((seq, 2 * RET_QK_DIM), BF16),
            pltpu.VMEM((seq, RET_V_DIM), F32),
            pltpu.VMEM((nc, 2 * RET_QK_DIM, RET_V_DIM), F32),
            pltpu.VMEM((nc, 2 * RET_QK_DIM, RET_V_DIM), BF16),
            pltpu.VMEM((RET_CHUNK_GROUP, RET_CHUNK, RET_CHUNK), F32),
            pltpu.VMEM((RET_CHUNK_GROUP, RET_CHUNK, 2 * RET_QK_DIM), BF16),
        ],
        compiler_params=_params(("parallel", "arbitrary")),
        name="retention",
    )(log_decay, proj, proj, proj, proj, cos2, sin2, gain.reshape(RET_HEADS, 1, RET_V_DIM))


COPY_ROWS = 128
RUN_NUM, RUN_MAX, RUN_DEN = 0, 1, 2
RUN_PARTS = 3


def _attention_kernel(q_ref, k_ref, v_ref, diag_ref, mix_ref, o_ref,
                      t_ref, stage_ref, stage4_ref, q1_ref, q4_ref, k4_ref, v4_ref, q16_ref, k16_ref, v16_ref,
                      run_a, run_b, s_ref, p_ref, den_ref, max_ref):
    del mix_ref
    seq = q_ref.shape[1]
    q_scale = ATT_HEAD_DIM ** -0.5
    len4 = seq // 4
    len16 = seq // 16

    @pl.when(pl.program_id(1) == 0)
    def _():
        for bi in range(len(DILATIONS)):
            wide = jnp.broadcast_to(diag_ref[0, bi], (ATT_QB, 2 * ATT_KW))
            for vi, off in enumerate(BIAS_OFFSETS):
                rolled = pltpu.roll(wide, ATT_KW - off, 1, stride=1, stride_axis=0)
                t_ref[bi, vi] = rolled[:, :ATT_KW]

    def stage(src_ref, scale, natural_ref):
        def body(cc, carry):
            rows = pl.ds(pl.multiple_of(cc * COPY_ROWS, COPY_ROWS), COPY_ROWS)
            x = src_ref[0, rows, :].astype(F32)
            if scale is not None:
                x = x * scale
                natural_ref[rows, :] = x.astype(BF16)
            stage_ref[rows, :] = x
            return carry

        lax.fori_loop(0, seq // COPY_ROWS, body, 0)

    def deinterleave(dst4_ref, dst16_ref):
        ch = min(COPY_ROWS, len4)
        for r in range(4):
            for cc in range(len4 // ch):
                x = stage_ref[pl.ds(r + 4 * cc * ch, ch, stride=4), :]
                rows = pl.ds(r * len4 + cc * ch, ch)
                stage4_ref[rows, :] = x
                dst4_ref[rows, :] = x.astype(BF16)
        ch = min(COPY_ROWS, len16)
        for r in range(4):
            for r_sub in range(4):
                for cc in range(len16 // ch):
                    x = stage4_ref[pl.ds(r * len4 + r_sub + 4 * cc * ch, ch, stride=4), :]
                    dst16_ref[pl.ds((r + 4 * r_sub) * len16 + cc * ch, ch), :] = x.astype(BF16)

    stage(q_ref, q_scale, q1_ref)
    deinterleave(q4_ref, q16_ref)
    stage(k_ref, None, None)
    deinterleave(k4_ref, k16_ref)
    stage(v_ref, None, None)
    deinterleave(v4_ref, v16_ref)

    def run_branch(bi, d, load_q, load_k, load_v, run_ref, first, last):
        length = seq // d
        kw = min(length, ATT_KW)
        nblk = length // ATT_QB

        def tile_rows(t):
            r = t // nblk
            nb = t - r * nblk
            i0 = nb * ATT_QB
            ws = jnp.clip(i0 - ATT_HALF, 0, length - kw)
            var = jnp.where(nb == 0, 0, jnp.where(nb == nblk - 1, 2, 1))
            base = r * length
            qrows = pl.ds(pl.multiple_of(base + i0, ATT_QB), ATT_QB)
            krows = pl.ds(pl.multiple_of(base + ws, ATT_HALF), kw)
            return qrows, krows, var

        def body(g, carry):
            tiles = [tile_rows(g * ATT_TILE_GROUP + j) for j in range(ATT_TILE_GROUP)]
            for j, (qrows, krows, var) in enumerate(tiles):
                s = lax.dot_general(load_q(qrows), load_k(krows), (((1,), (1,)), ((), ())),
                                    preferred_element_type=F32)
                s_ref[j, :, :kw] = s + t_ref[bi, var, :, :kw]
            for j, (qrows, krows, var) in enumerate(tiles):
                s = s_ref[j, :, :kw]
                m = jnp.max(s, axis=-1, keepdims=True)
                p = jnp.exp(s - m)
                den = jnp.sum(p, axis=-1, keepdims=True)
                p_ref[j, :, :kw] = p.astype(BF16)
                den_ref[j] = jnp.broadcast_to(den, (ATT_QB, ATT_HEAD_DIM))
                max_ref[j] = jnp.broadcast_to(m, (ATT_QB, ATT_HEAD_DIM))
            for j, (qrows, krows, var) in enumerate(tiles):
                num = jnp.dot(p_ref[j, :, :kw], load_v(krows), preferred_element_type=F32)
                den = den_ref[j]
                top = max_ref[j]
                if not first:
                    m_run = run_ref[RUN_MAX, qrows, :]
                    top = jnp.maximum(max_ref[j], m_run)
                    w_new = jnp.exp(max_ref[j] - top)
                    w_run = jnp.exp(m_run - top)
                    num = w_new * num + w_run * run_ref[RUN_NUM, qrows, :]
                    den = w_new * den + w_run * run_ref[RUN_DEN, qrows, :]
                if last:
                    o_ref[0, qrows, :] = (num / den).astype(o_ref.dtype)
                else:
                    run_ref[RUN_NUM, qrows, :] = num
                    run_ref[RUN_MAX, qrows, :] = top
                    run_ref[RUN_DEN, qrows, :] = den
            return carry

        lax.fori_loop(0, d * nblk // ATT_TILE_GROUP, body, 0)

    def regroup(src_ref, dst_ref, n_src, n_dst):
        len_src = seq // n_src
        len_dst = seq // n_dst
        ch = min(COPY_ROWS, len_src)
        for r_dst in range(n_dst):
            for r_sub in range(4):
                r_src = r_dst + n_dst * r_sub
                for cc in range(len_src // ch):
                    src = pl.ds(r_src * len_src + cc * ch, ch)
                    dst = pl.ds(r_dst * len_dst + r_sub + 4 * cc * ch, ch, stride=4)
                    for part in range(RUN_PARTS):
                        dst_ref[part, dst, :] = src_ref[part, src, :]

    run_branch(2, 16, lambda rows: q16_ref[rows, :], lambda rows: k16_ref[rows, :], lambda rows: v16_ref[rows, :],
               run_a, first=True, last=False)
    regroup(run_a, run_b, 16, 4)
    run_branch(1, 4, lambda rows: q4_ref[rows, :], lambda rows: k4_ref[rows, :], lambda rows: v4_ref[rows, :],
               run_b, first=False, last=False)
    regroup(run_b, run_a, 4, 1)
    run_branch(0, 1, lambda rows: q1_ref[rows, :], lambda rows: k_ref[0, rows, :], lambda rows: v_ref[0, rows, :],
               run_a, first=False, last=True)


def _attention(proj, bias_diags, mix):
    b, seq, _ = proj.shape
    q_block0 = (2 * RET_QK_WIDTH + 2 * RET_WIDTH) // ATT_HEAD_DIM
    k_block0 = q_block0 + ATT_HEADS
    v_block0 = k_block0 + ATT_HEADS
    out_block0 = RET_WIDTH // ATT_HEAD_DIM
    nb, nv = len(DILATIONS), len(BIAS_OFFSETS)
    head_rows = pltpu.VMEM((seq, ATT_HEAD_DIM), BF16)
    head_rows_f32 = pltpu.VMEM((seq, ATT_HEAD_DIM), F32)
    return pl.pallas_call(
        _attention_kernel,
        grid=(ATT_HEADS, b),
        in_specs=[
            pl.BlockSpec((1, seq, ATT_HEAD_DIM), lambda h, i: (i, 0, q_block0 + h)),
            pl.BlockSpec((1, seq, ATT_HEAD_DIM), lambda h, i: (i, 0, k_block0 + h)),
            pl.BlockSpec((1, seq, ATT_HEAD_DIM), lambda h, i: (i, 0, v_block0 + h)),
            pl.BlockSpec((1, nb, 1, 2 * ATT_KW), lambda h, i: (h, 0, 0, 0)),
            pl.BlockSpec(memory_space=pl.ANY),
        ],
        out_specs=pl.BlockSpec((1, seq, ATT_HEAD_DIM), lambda h, i: (i, 0, out_block0 + h)),
        out_shape=jax.ShapeDtypeStruct((b, seq, D_MODEL), BF16),
        scratch_shapes=[
            pltpu.VMEM((nb, nv, ATT_QB, ATT_KW), F32),
            head_rows_f32,
            head_rows_f32,
            head_rows,
            head_rows, head_rows, head_rows,
            head_rows, head_rows, head_rows,
            pltpu.VMEM((RUN_PARTS, seq, ATT_HEAD_DIM), F32),
            pltpu.VMEM((RUN_PARTS, seq, ATT_HEAD_DIM), F32),
            pltpu.VMEM((ATT_TILE_GROUP, ATT_QB, ATT_KW), F32),
            pltpu.VMEM((ATT_TILE_GROUP, ATT_QB, ATT_KW), BF16),
            pltpu.VMEM((ATT_TILE_GROUP, ATT_QB, ATT_HEAD_DIM), F32),
            pltpu.VMEM((ATT_TILE_GROUP, ATT_QB, ATT_HEAD_DIM), F32),
        ],
        input_output_aliases={4: 0},
        compiler_params=_params(("arbitrary", "arbitrary")),
        name="dilated_attention",
    )(proj, proj, proj, bias_diags, mix)


def _rotary_tables(seq):
    half = RET_QK_DIM // 2
    inv = ROPE_BASE ** (-jnp.arange(half, dtype=F32) / half)
    ang = jnp.arange(seq, dtype=F32)[:, None] * inv[None, :]
    cos, sin = jnp.cos(ang), jnp.sin(ang)
    return jnp.concatenate([cos, cos], axis=-1), jnp.concatenate([-sin, sin], axis=-1)


def _bucket_of(rel):
    nbk = REL_BUCKETS // 2
    max_exact = nbk // 2
    n = np.abs(rel)
    nf = np.maximum(n, 1).astype(np.float32)
    large = max_exact + (np.log(nf / max_exact) / math.log(REL_MAX_DISTANCE / max_exact)
                         * (nbk - max_exact)).astype(np.int32)
    large = np.minimum(large, nbk - 1)
    return np.where(rel > 0, nbk, 0) + np.where(n < max_exact, n, large)


def _bias_diags(rel_bias_table):
    delta = np.arange(2 * ATT_KW) - ATT_KW
    buckets = np.stack([_bucket_of(delta * d) for d in DILATIONS])
    inside = np.abs(delta) <= ATT_HALF
    bias = jnp.transpose(rel_bias_table.astype(F32))[:, buckets]
    return jnp.where(inside[None, None], bias, NEG_INF)[:, :, None, :]


def _trunk(x, bias_diags, layers, stacks, matrices, cast_ahead):
    b, seq, dm = x.shape
    cos2, sin2 = _rotary_tables(seq)
    xt = x.reshape(b * seq, dm)
    for l, layer in enumerate(layers):
        have = matrices[l]
        todo = [name for name in ("w_out", "w_up") if name not in have] if cast_ahead else []
        proj, done = _norm_matmul(xt, layer["norm_mix_pre"], have["w_in"], relu2=False,
                                  casts=[(stacks[name], l) for name in todo], name="in_proj")
        have.update(zip(todo, done))
        proj = proj.reshape(b, seq, IN_WIDTH)
        mix = _retention(proj, layer["log_decay"], layer["ret_norm_gain"], cos2, sin2)
        mix = _attention(proj, bias_diags, mix)
        xt = _matmul_norm_res(mix.reshape(b * seq, dm), have["w_out"], xt, layer["norm_mix_post"], tk=1024,
                              name="out_proj")
        todo = [(l, "w_down")] if cast_ahead and "w_down" not in have else []
        if cast_ahead and l + 1 < len(layers) and "w_in" not in matrices[l + 1]:
            todo.append((l + 1, "w_in"))
        u, done = _norm_matmul(xt, layer["norm_mlp_pre"], have["w_up"], relu2=True,
                               casts=[(stacks[name], m) for m, name in todo], name="up_proj")
        for (m, name), mat in zip(todo, done):
            matrices[m][name] = mat
        xt = _matmul_norm_res(u, have["w_down"], xt, layer["norm_mlp_post"], tk=2048, name="down_proj")
    return xt.reshape(b, seq, dm)


def kernel(x_prompt, x_sample, rel_bias_table, w_in, ret_log_decay, ret_norm_gain, w_out, w_up, w_down,
           norm_mix_pre, norm_mix_post, norm_mlp_pre, norm_mlp_post):
    depth = w_in.shape[0]
    stacks = {"w_in": w_in.astype(F32), "w_out": w_out.astype(F32), "w_up": w_up.astype(F32),
              "w_down": w_down.astype(F32)}
    layers = []
    for l in range(depth):
        layers.append({
            "log_decay": -jnp.exp(ret_log_decay[l].astype(F32)),
            "ret_norm_gain": ret_norm_gain[l].astype(F32),
            "norm_mix_pre": norm_mix_pre[l].astype(F32),
            "norm_mix_post": norm_mix_post[l].astype(F32),
            "norm_mlp_pre": norm_mlp_pre[l].astype(F32),
            "norm_mlp_post": norm_mlp_post[l].astype(F32),
        })
    matrices = [{} for _ in range(depth)]
    matrices[0]["w_in"] = stacks["w_in"][0].astype(BF16)
    bias_diags = _bias_diags(rel_bias_table)
    y_prompt = _trunk(x_prompt, bias_diags, layers, stacks, matrices, cast_ahead=True)
    y_sample = _trunk(x_sample, bias_diags, layers, stacks, matrices, cast_ahead=False)
    return (y_prompt, y_sample)
```

```python
import functools
import math

import numpy as np
import jax
import jax.numpy as jnp
from jax import lax
from jax.experimental import pallas as pl
from jax.experimental.pallas import tpu as pltpu

BF16 = jnp.bfloat16
F32 = jnp.float32

D_MODEL = 4096
RET_HEADS = 8
RET_QK_DIM = 128
RET_V_DIM = 256
RET_QK_WIDTH = RET_HEADS * RET_QK_DIM
RET_WIDTH = RET_HEADS * RET_V_DIM
ATT_HEADS = 16
ATT_HEAD_DIM = 128
ATT_WIDTH = ATT_HEADS * ATT_HEAD_DIM
IN_WIDTH = 2 * RET_QK_WIDTH + 2 * RET_WIDTH + 3 * ATT_WIDTH
RET_CHUNK = 128
RET_CHUNK_GROUP = 16
ROPE_BASE = 10000.0
DILATIONS = (1, 4, 16)
ATT_HALF = 64
REL_BUCKETS = 32
REL_MAX_DISTANCE = 1024
NORM_EPS = 1e-6
NEG_INF = -1e30

ATT_QB = 128
ATT_KW = 256
BIAS_OFFSETS = (0, -ATT_HALF, -2 * ATT_HALF)
ATT_TILE_GROUP = 32

V7X_VMEM_LIMIT_BYTES = 56 * 1024 * 1024
V7X_VMEM_LIMIT_LARGE_BYTES = 60 * 1024 * 1024


def _params(sem, vmem=V7X_VMEM_LIMIT_BYTES):
    return pltpu.CompilerParams(dimension_semantics=sem, vmem_limit_bytes=vmem)


NORM_ROWS = 16
NORM_SLICES = 8


def _norm_matmul_kernel(xs_ref, g_ref, w_ref, *refs, relu2, n_casts):
    cast_in_refs = refs[:n_casts]
    o_ref = refs[n_casts]
    cast_out_refs = refs[n_casts + 1:2 * n_casts + 1]
    h_even_ref, h_odd_ref = refs[2 * n_casts + 1:]
    i = pl.program_id(0)
    j = pl.program_id(1)
    slice_rows = xs_ref.shape[0]

    def cast_blocks():
        for src_ref, dst_ref in zip(cast_in_refs, cast_out_refs):
            dst_ref[...] = src_ref[...].astype(BF16)

    def normalise_slice(h_ref):
        gain = g_ref[...]
        base = jnp.minimum(j, NORM_SLICES - 1) * slice_rows
        for c in range(slice_rows // NORM_ROWS):
            x = xs_ref[pl.ds(c * NORM_ROWS, NORM_ROWS), :]
            ms = jnp.mean(x * x, axis=-1, keepdims=True)
            rows = pl.ds(pl.multiple_of(base + c * NORM_ROWS, NORM_ROWS), NORM_ROWS)
            h_ref[rows, :] = ((x * lax.rsqrt(ms + NORM_EPS)) * gain).astype(BF16)

    def multiply(h_ref):
        y = jnp.dot(h_ref[...], w_ref[...], preferred_element_type=F32)
        if relu2:
            y = jnp.maximum(y, 0.0)
            y = y * y
        o_ref[...] = y.astype(o_ref.dtype)

    @pl.when(i == 0)
    def _():
        cast_blocks()
        normalise_slice(h_even_ref)

    for parity, h_write_ref, h_read_ref in ((1, h_odd_ref, h_even_ref), (0, h_even_ref, h_odd_ref)):
        for slicing in (True, False):
            @pl.when((i > 0) & (i % 2 == parity) & ((j < NORM_SLICES) == slicing))
            def _(h_write_ref=h_write_ref, h_read_ref=h_read_ref, slicing=slicing):
                cast_blocks()
                if slicing:
                    normalise_slice(h_write_ref)
                multiply(h_read_ref)


def _norm_matmul(x, gain, w, *, relu2, casts=(), tm=1024, tn=1024, name):
    t, k = x.shape
    n = w.shape[1]
    ni, nj = t // tm, n // tn
    assert nj >= NORM_SLICES and tm % (NORM_SLICES * NORM_ROWS) == 0
    slice_rows = tm // NORM_SLICES
    steps = (ni + 1) * nj
    cast_in_specs, cast_out_specs = [], []
    for stack, layer in casts:
        _, rows, cols = stack.shape
        n_blocks = max(nb for nb in (8, 16, 32, 64, 128, 256, 512) if nb <= steps and rows % (nb * NORM_ROWS) == 0)
        block_rows = rows // n_blocks
        cast_in_specs.append(pl.BlockSpec(
            (None, block_rows, cols),
            lambda i, j, layer=layer, last=n_blocks - 1: (layer, jnp.minimum(i * nj + j, last), 0)))
        cast_out_specs.append(pl.BlockSpec(
            (block_rows, cols), lambda i, j, last=n_blocks - 1: (jnp.minimum(i * nj + j, last), 0)))
    outs = pl.pallas_call(
        functools.partial(_norm_matmul_kernel, relu2=relu2, n_casts=len(casts)),
        grid=(ni + 1, nj),
        in_specs=[
            pl.BlockSpec((slice_rows, k),
                         lambda i, j: (jnp.minimum(i, ni - 1) * NORM_SLICES + jnp.minimum(j, NORM_SLICES - 1), 0)),
            pl.BlockSpec((1, k), lambda i, j: (0, 0)),
            pl.BlockSpec((k, tn), lambda i, j: (0, jnp.where(i == 0, 0, j))),
            *cast_in_specs,
        ],
        out_specs=[pl.BlockSpec((tm, tn), lambda i, j: (jnp.maximum(i - 1, 0), jnp.where(i == 0, 0, j))),
                   *cast_out_specs],
        out_shape=[jax.ShapeDtypeStruct((t, n), BF16),
                   *(jax.ShapeDtypeStruct(stack.shape[1:], BF16) for stack, _ in casts)],
        scratch_shapes=[pltpu.VMEM((tm, k), BF16), pltpu.VMEM((tm, k), BF16)],
        compiler_params=_params(("arbitrary", "arbitrary")),
        name=name,
    )(x, gain.reshape(1, k), w, *(stack for stack, _ in casts))
    return outs[0], tuple(outs[1:])


def _matmul_norm_res_kernel(a_ref, w_ref, xs_ref, g_ref, os_ref, acc_even_ref, acc_odd_ref, *, ni):
    i = pl.program_id(0)
    kk = pl.program_id(1)
    slice_rows = os_ref.shape[0]

    @pl.when((i == 0) & (kk == 0))
    def _():
        acc_even_ref[...] = jnp.zeros_like(acc_even_ref)
        acc_odd_ref[...] = jnp.zeros_like(acc_odd_ref)

    def accumulate(acc_ref):
        part = jnp.dot(a_ref[...], w_ref[...], preferred_element_type=F32)
        acc_ref[...] = jnp.where(kk == 0, part, acc_ref[...] + part)

    def finish_slice(acc_ref):
        gain = g_ref[...]
        for c in range(slice_rows // NORM_ROWS):
            rows = pl.ds(pl.multiple_of(kk * slice_rows + c * NORM_ROWS, NORM_ROWS), NORM_ROWS)
            y = acc_ref[rows, :]
            ms = jnp.mean(y * y, axis=-1, keepdims=True)
            local = pl.ds(c * NORM_ROWS, NORM_ROWS)
            os_ref[local, :] = xs_ref[local, :] + (y * lax.rsqrt(ms + NORM_EPS)) * gain

    @pl.when(i == 0)
    def _():
        accumulate(acc_even_ref)

    @pl.when((i > 0) & (i < ni) & (i % 2 == 1))
    def _():
        accumulate(acc_odd_ref)
        finish_slice(acc_even_ref)

    @pl.when((i > 0) & (i < ni) & (i % 2 == 0))
    def _():
        accumulate(acc_even_ref)
        finish_slice(acc_odd_ref)

    @pl.when(i == ni)
    def _():
        finish_slice(acc_odd_ref if (ni - 1) % 2 == 1 else acc_even_ref)


def _matmul_norm_res(a, w, x, gain, *, tm=512, tk, name):
    t, k = a.shape
    n = w.shape[1]
    ni, nk = t // tm, k // tk
    assert tm % (nk * NORM_ROWS) == 0
    slice_rows = tm // nk

    def slice_index(i, kk):
        return (jnp.maximum(i - 1, 0) * nk + jnp.where(i == 0, 0, kk), 0)

    return pl.pallas_call(
        functools.partial(_matmul_norm_res_kernel, ni=ni),
        grid=(ni + 1, nk),
        in_specs=[
            pl.BlockSpec((tm, tk), lambda i, kk: (jnp.minimum(i, ni - 1), jnp.where(i == ni, nk - 1, kk))),
            pl.BlockSpec((tk, n), lambda i, kk: (jnp.where(i == ni, nk - 1, kk), 0)),
            pl.BlockSpec((slice_rows, n), slice_index),
            pl.BlockSpec((1, n), lambda i, kk: (0, 0)),
        ],
        out_specs=pl.BlockSpec((slice_rows, n), slice_index),
        out_shape=jax.ShapeDtypeStruct((t, n), F32),
        scratch_shapes=[pltpu.VMEM((tm, n), F32), pltpu.VMEM((tm, n), F32)],
        compiler_params=_params(("arbitrary", "arbitrary"), V7X_VMEM_LIMIT_LARGE_BYTES),
        name=name,
    )(a, w, x, gain.reshape(1, n))


def _retention_kernel(ld_ref, q_ref, k_ref, v_ref, g_ref, cos_ref, sin_ref, gain_ref, o_ref,
                      qd_ref, y_ref, delta_ref, prev_ref, s_ref, kd_ref):
    seq = q_ref.shape[1]
    c = RET_CHUNK
    nc = seq // c
    grp = s_ref.shape[0]
    h = pl.program_id(1)
    ld_f = ld_ref[0, h]
    ld_b = ld_ref[1, h]

    ri = lax.broadcasted_iota(jnp.int32, (c, c), 0)
    ci = lax.broadcasted_iota(jnp.int32, (c, c), 1)
    diff = (ri - ci).astype(F32)
    mask = jnp.where(diff >= 0, jnp.exp(ld_f * jnp.maximum(diff, 0.0)), jnp.exp(ld_b * jnp.maximum(-diff, 0.0)))
    pos = lax.broadcasted_iota(jnp.int32, (c, RET_QK_DIM), 0).astype(F32)
    qdec_f = jnp.exp(ld_f * (pos + 1.0))
    kdec_f = jnp.exp(ld_f * (c - 1.0 - pos))
    qdec_b = jnp.exp(ld_b * (c - pos))
    kdec_b = jnp.exp(ld_b * pos)
    cd_f = jnp.exp(jnp.full((1, RET_V_DIM), ld_f * c, F32))
    cd_b = jnp.exp(jnp.full((1, RET_V_DIM), ld_b * c, F32))
    gain = gain_ref[0]
    k_scale = RET_QK_DIM ** -0.5
    half = RET_QK_DIM // 2

    def chunk_rows(n):
        return pl.ds(pl.multiple_of(n * c, c), c)

    def phase_a(gi, carry):
        chunks = [gi * grp + j for j in range(grp)]
        for j, n in enumerate(chunks):
            rows = chunk_rows(n)
            cos = cos_ref[rows, :]
            sin = sin_ref[rows, :]
            xq = q_ref[0, rows, :].astype(F32)
            xk = k_ref[0, rows, :].astype(F32)
            q = xq * cos + pltpu.roll(xq, half, 1) * sin
            k = (xk * cos + pltpu.roll(xk, half, 1) * sin) * k_scale
            qd_ref[rows, :] = jnp.concatenate([(q * qdec_f).astype(BF16), (q * qdec_b).astype(BF16)], axis=1)
            kd_ref[j] = jnp.concatenate([(k * kdec_f).astype(BF16), (k * kdec_b).astype(BF16)], axis=1)
            s_ref[j] = lax.dot_general(q.astype(BF16), k.astype(BF16), (((1,), (1,)), ((), ())),
                                       preferred_element_type=F32)
        for j, n in enumerate(chunks):
            rows = chunk_rows(n)
            y_ref[rows, :] = jnp.dot((s_ref[j] * mask).astype(BF16), v_ref[0, rows, :],
                                     preferred_element_type=F32)
        for j, n in enumerate(chunks):
            delta_ref[n] = lax.dot_general(kd_ref[j], v_ref[0, chunk_rows(n), :], (((0,), (0,)), ((), ())),
                                           preferred_element_type=F32)
        return carry

    lax.fori_loop(0, nc // grp, phase_a, 0)

    def scan_f(n, st):
        prev_ref[n, pl.ds(0, RET_QK_DIM), :] = st.astype(BF16)
        return st * cd_f + delta_ref[n, pl.ds(0, RET_QK_DIM), :]

    def scan_b(t, st):
        n = nc - 1 - t
        prev_ref[n, pl.ds(RET_QK_DIM, RET_QK_DIM), :] = st.astype(BF16)
        return st * cd_b + delta_ref[n, pl.ds(RET_QK_DIM, RET_QK_DIM), :]

    zero_state = jnp.zeros((RET_QK_DIM, RET_V_DIM), F32)
    lax.fori_loop(0, nc, scan_f, zero_state)
    lax.fori_loop(0, nc, scan_b, zero_state)

    def phase_c(gi, carry):
        chunks = [gi * grp + j for j in range(grp)]
        ys = []
        for n in chunks:
            rows = chunk_rows(n)
            ys.append(y_ref[rows, :] + jnp.dot(qd_ref[rows, :], prev_ref[n], preferred_element_type=F32))
        for n, y in zip(chunks, ys):
            rows = chunk_rows(n)
            ms = jnp.mean(y * y, axis=-1, keepdims=True)
            yn = (y * lax.rsqrt(ms + NORM_EPS)) * gain
            gate = g_ref[0, rows, :].astype(F32)
            o_ref[0, rows, :] = ((gate * jax.nn.sigmoid(gate)) * yn).astype(o_ref.dtype)
        return carry

    lax.fori_loop(0, nc // grp, phase_c, 0)


def _retention(proj, log_decay, gain, cos2, sin2):
    b, seq, _ = proj.shape
    qk_blocks = RET_QK_WIDTH // RET_QK_DIM
    v_block0 = 2 * RET_QK_WIDTH // RET_V_DIM
    g_block0 = v_block0 + RET_HEADS
    nc = seq // RET_CHUNK
    group = min(RET_CHUNK_GROUP, nc)
    return pl.pallas_call(
        _retention_kernel,
        grid=(b, RET_HEADS),
        in_specs=[
            pl.BlockSpec(memory_space=pltpu.SMEM),
            pl.BlockSpec((1, seq, RET_QK_DIM), lambda i, h: (i, 0, h)),
            pl.BlockSpec((1, seq, RET_QK_DIM), lambda i, h: (i, 0, qk_blocks + h)),
            pl.BlockSpec((1, seq, RET_V_DIM), lambda i, h: (i, 0, v_block0 + h)),
            pl.BlockSpec((1, seq, RET_V_DIM), lambda i, h: (i, 0, g_block0 + h)),
            pl.BlockSpec((seq, RET_QK_DIM), lambda i, h: (0, 0)),
            pl.BlockSpec((seq, RET_QK_DIM), lambda i, h: (0, 0)),
            pl.BlockSpec((1, 1, RET_V_DIM), lambda i, h: (h, 0, 0)),
        ],
        out_specs=pl.BlockSpec((1, seq, RET_V_DIM), lambda i, h: (i, 0, h)),
        out_shape=jax.ShapeDtypeStruct((b, seq, D_MODEL), BF16),
        scratch_shapes=[
            pltpu.VMEM((seq, 2 * RET_QK_DIM), BF16),
            pltpu.VMEM((seq, RET_V_DIM), F32),
            pltpu.VMEM((nc, 2 * RET_QK_DIM, RET_V_DIM), F32),
            pltpu.VMEM((nc, 2 * RET_QK_DIM, RET_V_DIM), BF16),
            pltpu.VMEM((group, RET_CHUNK, RET_CHUNK), F32),
            pltpu.VMEM((group, RET_CHUNK, 2 * RET_QK_DIM), BF16),
        ],
        compiler_params=_params(("parallel", "arbitrary")),
        name="retention",
    )(log_decay, proj, proj, proj, proj, cos2, sin2, gain.reshape(RET_HEADS, 1, RET_V_DIM))


COPY_ROWS = 128
RUN_NUM, RUN_MAX, RUN_DEN = 0, 1, 2
RUN_PARTS = 3


def _attention_kernel(q_ref, k_ref, v_ref, diag_ref, mix_ref, o_ref,
                      t_ref, stage_ref, stage4_ref, q1_ref, q4_ref, k4_ref, v4_ref, q16_ref, k16_ref, v16_ref,
                      run_a, run_b, s_ref, p_ref, den_ref, max_ref):
    del mix_ref
    seq = q_ref.shape[1]
    q_scale = ATT_HEAD_DIM ** -0.5
    len4 = seq // 4
    len16 = seq // 16
    group = s_ref.shape[0]

    @pl.when(pl.program_id(1) == 0)
    def _():
        for bi in range(len(DILATIONS)):
            wide = jnp.broadcast_to(diag_ref[0, bi], (ATT_QB, 2 * ATT_KW))
            for vi, off in enumerate(BIAS_OFFSETS):
                rolled = pltpu.roll(wide, ATT_KW - off, 1, stride=1, stride_axis=0)
                t_ref[bi, vi] = rolled[:, :ATT_KW]

    def stage(src_ref, scale, natural_ref):
        def body(cc, carry):
            rows = pl.ds(pl.multiple_of(cc * COPY_ROWS, COPY_ROWS), COPY_ROWS)
            x = src_ref[0, rows, :].astype(F32)
            if scale is not None:
                x = x * scale
                natural_ref[rows, :] = x.astype(BF16)
            stage_ref[rows, :] = x
            return carry

        lax.fori_loop(0, seq // COPY_ROWS, body, 0)

    def deinterleave(dst4_ref, dst16_ref):
        ch = min(COPY_ROWS, len4)
        for r in range(4):
            for cc in range(len4 // ch):
                x = stage_ref[pl.ds(r + 4 * cc * ch, ch, stride=4), :]
                rows = pl.ds(r * len4 + cc * ch, ch)
                stage4_ref[rows, :] = x
                dst4_ref[rows, :] = x.astype(BF16)
        ch = min(COPY_ROWS, len16)
        for r in range(4):
            for r_sub in range(4):
                for cc in range(len16 // ch):
                    x = stage4_ref[pl.ds(r * len4 + r_sub + 4 * cc * ch, ch, stride=4), :]
                    dst16_ref[pl.ds((r + 4 * r_sub) * len16 + cc * ch, ch), :] = x.astype(BF16)

    stage(q_ref, q_scale, q1_ref)
    deinterleave(q4_ref, q16_ref)
    stage(k_ref, None, None)
    deinterleave(k4_ref, k16_ref)
    stage(v_ref, None, None)
    deinterleave(v4_ref, v16_ref)

    def run_branch(bi, d, load_q, load_k, load_v, run_ref, first, last):
        length = seq // d
        kw = min(length, ATT_KW)
        nblk = length // ATT_QB

        def tile_rows(t):
            r = t // nblk
            nb = t - r * nblk
            i0 = nb * ATT_QB
            ws = jnp.clip(i0 - ATT_HALF, 0, length - kw)
            var = jnp.where(nb == 0, 0, jnp.where(nb == nblk - 1, 2, 1))
            base = r * length
            qrows = pl.ds(pl.multiple_of(base + i0, ATT_QB), ATT_QB)
            krows = pl.ds(pl.multiple_of(base + ws, ATT_HALF), kw)
            return qrows, krows, var

        def body(g, carry):
            tiles = [tile_rows(g * group + j) for j in range(group)]
            for j, (qrows, krows, var) in enumerate(tiles):
                s = lax.dot_general(load_q(qrows), load_k(krows), (((1,), (1,)), ((), ())),
                                    preferred_element_type=F32)
                s_ref[j, :, :kw] = s + t_ref[bi, var, :, :kw]
            for j, (qrows, krows, var) in enumerate(tiles):
                s = s_ref[j, :, :kw]
                m = jnp.max(s, axis=-1, keepdims=True)
                p = jnp.exp(s - m)
                den = jnp.sum(p, axis=-1, keepdims=True)
                p_ref[j, :, :kw] = p.astype(BF16)
                den_ref[j] = jnp.broadcast_to(den, (ATT_QB, ATT_HEAD_DIM))
                max_ref[j] = jnp.broadcast_to(m, (ATT_QB, ATT_HEAD_DIM))
            for j, (qrows, krows, var) in enumerate(tiles):
                num = jnp.dot(p_ref[j, :, :kw], load_v(krows), preferred_element_type=F32)
                den = den_ref[j]
                top = max_ref[j]
                if not first:
                    m_run = run_ref[RUN_MAX, qrows, :]
                    top = jnp.maximum(max_ref[j], m_run)
                    w_new = jnp.exp(max_ref[j] - top)
                    w_run = jnp.exp(m_run - top)
                    num = w_new * num + w_run * run_ref[RUN_NUM, qrows, :]
                    den = w_new * den + w_run * run_ref[RUN_DEN, qrows, :]
                if last:
                    o_ref[0, qrows, :] = (num / den).astype(o_ref.dtype)
                else:
                    run_ref[RUN_NUM, qrows, :] = num
                    run_ref[RUN_MAX, qrows, :] = top
                    run_ref[RUN_DEN, qrows, :] = den
            return carry

        lax.fori_loop(0, d * nblk // group, body, 0)

    def regroup(src_ref, dst_ref, n_src, n_dst):
        len_src = seq // n_src
        len_dst = seq // n_dst
        ch = min(COPY_ROWS, len_src)
        for r_dst in range(n_dst):
            for r_sub in range(4):
                r_src = r_dst + n_dst * r_sub
                for cc in range(len_src // ch):
                    src = pl.ds(r_src * len_src + cc * ch, ch)
                    dst = pl.ds(r_dst * len_dst + r_sub + 4 * cc * ch, ch, stride=4)
                    for part in range(RUN_PARTS):
                        dst_ref[part, dst, :] = src_ref[part, src, :]

    run_branch(2, 16, lambda rows: q16_ref[rows, :], lambda rows: k16_ref[rows, :], lambda rows: v16_ref[rows, :],
               run_a, first=True, last=False)
    regroup(run_a, run_b, 16, 4)
    run_branch(1, 4, lambda rows: q4_ref[rows, :], lambda rows: k4_ref[rows, :], lambda rows: v4_ref[rows, :],
               run_b, first=False, last=False)
    regroup(run_b, run_a, 4, 1)
    run_branch(0, 1, lambda rows: q1_ref[rows, :], lambda rows: k_ref[0, rows, :], lambda rows: v_ref[0, rows, :],
               run_a, first=False, last=True)


def _attention(proj, bias_diags, mix):
    b, seq, _ = proj.shape
    q_block0 = (2 * RET_QK_WIDTH + 2 * RET_WIDTH) // ATT_HEAD_DIM
    k_block0 = q_block0 + ATT_HEADS
    v_block0 = k_block0 + ATT_HEADS
    out_block0 = RET_WIDTH // ATT_HEAD_DIM
    nb, nv = len(DILATIONS), len(BIAS_OFFSETS)
    head_rows = pltpu.VMEM((seq, ATT_HEAD_DIM), BF16)
    head_rows_f32 = pltpu.VMEM((seq, ATT_HEAD_DIM), F32)
    group = min(ATT_TILE_GROUP, seq // ATT_QB)
    return pl.pallas_call(
        _attention_kernel,
        grid=(ATT_HEADS, b),
        in_specs=[
            pl.BlockSpec((1, seq, ATT_HEAD_DIM), lambda h, i: (i, 0, q_block0 + h)),
            pl.BlockSpec((1, seq, ATT_HEAD_DIM), lambda h, i: (i, 0, k_block0 + h)),
            pl.BlockSpec((1, seq, ATT_HEAD_DIM), lambda h, i: (i, 0, v_block0 + h)),
            pl.BlockSpec((1, nb, 1, 2 * ATT_KW), lambda h, i: (h, 0, 0, 0)),
            pl.BlockSpec(memory_space=pl.ANY),
        ],
        out_specs=pl.BlockSpec((1, seq, ATT_HEAD_DIM), lambda h, i: (i, 0, out_block0 + h)),
        out_shape=jax.ShapeDtypeStruct((b, seq, D_MODEL), BF16),
        scratch_shapes=[
            pltpu.VMEM((nb, nv, ATT_QB, ATT_KW), F32),
            head_rows_f32,
            head_rows_f32,
            head_rows,
            head_rows, head_rows, head_rows,
            head_rows, head_rows, head_rows,
            pltpu.VMEM((RUN_PARTS, seq, ATT_HEAD_DIM), F32),
            pltpu.VMEM((RUN_PARTS, seq, ATT_HEAD_DIM), F32),
            pltpu.VMEM((group, ATT_QB, ATT_KW), F32),
            pltpu.VMEM((group, ATT_QB, ATT_KW), BF16),
            pltpu.VMEM((group, ATT_QB, ATT_HEAD_DIM), F32),
            pltpu.VMEM((group, ATT_QB, ATT_HEAD_DIM), F32),
        ],
        input_output_aliases={4: 0},
        compiler_params=_params(("arbitrary", "arbitrary")),
        name="dilated_attention",
    )(proj, proj, proj, bias_diags, mix)


def _rotary_tables(seq):
    half = RET_QK_DIM // 2
    inv = ROPE_BASE ** (-jnp.arange(half, dtype=F32) / half)
    ang = jnp.arange(seq, dtype=F32)[:, None] * inv[None, :]
    cos, sin = jnp.cos(ang), jnp.sin(ang)
    return jnp.concatenate([cos, cos], axis=-1), jnp.concatenate([-sin, sin], axis=-1)


def _bucket_of(rel):
    nbk = REL_BUCKETS // 2
    max_exact = nbk // 2
    n = np.abs(rel)
    nf = np.maximum(n, 1).astype(np.float32)
    large = max_exact + (np.log(nf / max_exact) / math.log(REL_MAX_DISTANCE / max_exact)
                         * (nbk - max_exact)).astype(np.int32)
    large = np.minimum(large, nbk - 1)
    return np.where(rel > 0, nbk, 0) + np.where(n < max_exact, n, large)


def _bias_diags(rel_bias_table):
    delta = np.arange(2 * ATT_KW) - ATT_KW
    buckets = np.stack([_bucket_of(delta * d) for d in DILATIONS])
    inside = np.abs(delta) <= ATT_HALF
    bias = jnp.transpose(rel_bias_table.astype(F32))[:, buckets]
    return jnp.where(inside[None, None], bias, NEG_INF)[:, :, None, :]


def _trunk(x, bias_diags, layers, stacks, matrices, cast_ahead):
    b, seq, dm = x.shape
    cos2, sin2 = _rotary_tables(seq)
    xt = x.reshape(b * seq, dm)
    for l, layer in enumerate(layers):
        have = matrices[l]
        todo = [name for name in ("w_out", "w_up") if name not in have] if cast_ahead else []
        proj, done = _norm_matmul(xt, layer["norm_mix_pre"], have["w_in"], relu2=False,
                                  casts=[(stacks[name], l) for name in todo], name="in_proj")
        have.update(zip(todo, done))
        proj = proj.reshape(b, seq, IN_WIDTH)
        mix = _retention(proj, layer["log_decay"], layer["ret_norm_gain"], cos2, sin2)
        mix = _attention(proj, bias_diags, mix)
        xt = _matmul_norm_res(mix.reshape(b * seq, dm), have["w_out"], xt, layer["norm_mix_post"], tk=1024,
                              name="out_proj")
        todo = [(l, "w_down")] if cast_ahead and "w_down" not in have else []
        if cast_ahead and l + 1 < len(layers) and "w_in" not in matrices[l + 1]:
            todo.append((l + 1, "w_in"))
        u, done = _norm_matmul(xt, layer["norm_mlp_pre"], have["w_up"], relu2=True,
                               casts=[(stacks[name], m) for m, name in todo], name="up_proj")
        for (m, name), mat in zip(todo, done):
            matrices[m][name] = mat
        xt = _matmul_norm_res(u, have["w_down"], xt, layer["norm_mlp_post"], tk=2048, name="down_proj")
    return xt.reshape(b, seq, dm)


def kernel(x_prompt, x_sample, rel_bias_table, w_in, ret_log_decay, ret_norm_gain, w_out, w_up, w_down,
           norm_mix_pre, norm_mix_post, norm_mlp_pre, norm_mlp_post):
    depth = w_in.shape[0]
    stacks = {"w_in": w_in.astype(F32), "w_out": w_out.astype(F32), "w_up": w_up.astype(F32),
              "w_down": w_down.astype(F32)}
    layers = []
    for l in range(depth):
        layers.append({
            "log_decay": -jnp.exp(ret_log_decay[l].astype(F32)),
            "ret_norm_gain": ret_norm_gain[l].astype(F32),
            "norm_mix_pre": norm_mix_pre[l].astype(F32),
            "norm_mix_post": norm_mix_post[l].astype(F32),
            "norm_mlp_pre": norm_mlp_pre[l].astype(F32),
            "norm_mlp_post": norm_mlp_post[l].astype(F32),
        })
    matrices = [{} for _ in range(depth)]
    matrices[0]["w_in"] = stacks["w_in"][0].astype(BF16)
    bias_diags = _bias_diags(rel_bias_table)
    y_prompt = _trunk(x_prompt, bias_diags, layers, stacks, matrices, cast_ahead=True)
    y_sample = _trunk(x_sample, bias_diags, layers, stacks, matrices, cast_ahead=False)
    return (y_prompt, y_sample)
```

```python
import functools
import math

import numpy as np
import jax
import jax.numpy as jnp
from jax import lax
from jax.experimental import pallas as pl
from jax.experimental.pallas import tpu as pltpu

BF16 = jnp.bfloat16
F32 = jnp.float32

D_MODEL = 4096
RET_HEADS = 8
RET_QK_DIM = 128
RET_V_DIM = 256
RET_QK_WIDTH = RET_HEADS * RET_QK_DIM
RET_WIDTH = RET_HEADS * RET_V_DIM
ATT_HEADS = 16
ATT_HEAD_DIM = 128
ATT_WIDTH = ATT_HEADS * ATT_HEAD_DIM
IN_WIDTH = 2 * RET_QK_WIDTH + 2 * RET_WIDTH + 3 * ATT_WIDTH
RET_CHUNK = 128
RET_CHUNK_GROUP = 16
ROPE_BASE = 10000.0
DILATIONS = (1, 4, 16)
ATT_HALF = 64
REL_BUCKETS = 32
REL_MAX_DISTANCE = 1024
NORM_EPS = 1e-6
NEG_INF = -1e30

ATT_QB = 128
ATT_KW = 256
BIAS_OFFSETS = (0, -ATT_HALF, -2 * ATT_HALF)
ATT_TILE_GROUP = 32

V7X_VMEM_LIMIT_BYTES = 56 * 1024 * 1024
V7X_VMEM_LIMIT_LARGE_BYTES = 60 * 1024 * 1024


def _params(sem, vmem=V7X_VMEM_LIMIT_BYTES):
    return pltpu.CompilerParams(dimension_semantics=sem, vmem_limit_bytes=vmem)


NORM_ROWS = 16
NORM_SLICES = 8


def _norm_matmul_kernel(xs_ref, g_ref, w_ref, *refs, relu2, n_casts):
    cast_in_refs = refs[:n_casts]
    o_ref = refs[n_casts]
    cast_out_refs = refs[n_casts + 1:2 * n_casts + 1]
    h_even_ref, h_odd_ref = refs[2 * n_casts + 1:]
    i = pl.program_id(0)
    j = pl.program_id(1)
    slice_rows = xs_ref.shape[0]

    def cast_blocks():
        for src_ref, dst_ref in zip(cast_in_refs, cast_out_refs):
            dst_ref[...] = src_ref[...].astype(BF16)

    def normalise_slice(h_ref):
        gain = g_ref[...]
        base = jnp.minimum(j, NORM_SLICES - 1) * slice_rows
        for c in range(slice_rows // NORM_ROWS):
            x = xs_ref[pl.ds(c * NORM_ROWS, NORM_ROWS), :]
            ms = jnp.mean(x * x, axis=-1, keepdims=True)
            rows = pl.ds(pl.multiple_of(base + c * NORM_ROWS, NORM_ROWS), NORM_ROWS)
            h_ref[rows, :] = ((x * lax.rsqrt(ms + NORM_EPS)) * gain).astype(BF16)

    def multiply(h_ref):
        y = jnp.dot(h_ref[...], w_ref[...], preferred_element_type=F32)
        if relu2:
            y = jnp.maximum(y, 0.0)
            y = y * y
        o_ref[...] = y.astype(o_ref.dtype)

    @pl.when(i == 0)
    def _():
        cast_blocks()
        normalise_slice(h_even_ref)

    for parity, h_write_ref, h_read_ref in ((1, h_odd_ref, h_even_ref), (0, h_even_ref, h_odd_ref)):
        for slicing in (True, False):
            @pl.when((i > 0) & (i % 2 == parity) & ((j < NORM_SLICES) == slicing))
            def _(h_write_ref=h_write_ref, h_read_ref=h_read_ref, slicing=slicing):
                cast_blocks()
                if slicing:
                    normalise_slice(h_write_ref)
                multiply(h_read_ref)


def _norm_matmul(x, gain, w, *, relu2, casts=(), tm=1024, tn=1024, name):
    t, k = x.shape
    n = w.shape[1]
    ni, nj = t // tm, n // tn
    assert nj >= NORM_SLICES and tm % (NORM_SLICES * NORM_ROWS) == 0
    slice_rows = tm // NORM_SLICES
    steps = (ni + 1) * nj
    cast_in_specs, cast_out_specs = [], []
    for stack, layer in casts:
        _, rows, cols = stack.shape
        n_blocks = max(nb for nb in (8, 16, 32, 64, 128, 256, 512) if nb <= steps and rows % (nb * NORM_ROWS) == 0)
        block_rows = rows // n_blocks
        cast_in_specs.append(pl.BlockSpec(
            (None, block_rows, cols),
            lambda i, j, layer=layer, last=n_blocks - 1: (layer, jnp.minimum(i * nj + j, last), 0)))
        cast_out_specs.append(pl.BlockSpec(
            (block_rows, cols), lambda i, j, last=n_blocks - 1: (jnp.minimum(i * nj + j, last), 0)))
    outs = pl.pallas_call(
        functools.partial(_norm_matmul_kernel, relu2=relu2, n_casts=len(casts)),
        grid=(ni + 1, nj),
        in_specs=[
            pl.BlockSpec((slice_rows, k),
                         lambda i, j: (jnp.minimum(i, ni - 1) * NORM_SLICES + jnp.minimum(j, NORM_SLICES - 1), 0)),
            pl.BlockSpec((1, k), lambda i, j: (0, 0)),
            pl.BlockSpec((k, tn), lambda i, j: (0, jnp.where(i == 0, 0, j))),
            *cast_in_specs,
        ],
        out_specs=[pl.BlockSpec((tm, tn), lambda i, j: (jnp.maximum(i - 1, 0), jnp.where(i == 0, 0, j))),
                   *cast_out_specs],
        out_shape=[jax.ShapeDtypeStruct((t, n), BF16),
                   *(jax.ShapeDtypeStruct(stack.shape[1:], BF16) for stack, _ in casts)],
        scratch_shapes=[pltpu.VMEM((tm, k), BF16), pltpu.VMEM((tm, k), BF16)],
        compiler_params=_params(("arbitrary", "arbitrary")),
        name=name,
    )(x, gain.reshape(1, k), w, *(stack for stack, _ in casts))
    return outs[0], tuple(outs[1:])


def _matmul_norm_res_kernel(a_ref, w_ref, xs_ref, g_ref, os_ref, acc_even_ref, acc_odd_ref, *, ni):
    i = pl.program_id(0)
    kk = pl.program_id(1)
    slice_rows = os_ref.shape[0]

    @pl.when((i == 0) & (kk == 0))
    def _():
        acc_even_ref[...] = jnp.zeros_like(acc_even_ref)
        acc_odd_ref[...] = jnp.zeros_like(acc_odd_ref)

    def accumulate(acc_ref):
        part = jnp.dot(a_ref[...], w_ref[...], preferred_element_type=F32)
        acc_ref[...] = jnp.where(kk == 0, part, acc_ref[...] + part)

    def finish_slice(acc_ref):
        gain = g_ref[...]
        for c in range(slice_rows // NORM_ROWS):
            rows = pl.ds(pl.multiple_of(kk * slice_rows + c * NORM_ROWS, NORM_ROWS), NORM_ROWS)
            y = acc_ref[rows, :]
            ms = jnp.mean(y * y, axis=-1, keepdims=True)
            local = pl.ds(c * NORM_ROWS, NORM_ROWS)
            os_ref[local, :] = xs_ref[local, :] + (y * lax.rsqrt(ms + NORM_EPS)) * gain

    @pl.when(i == 0)
    def _():
        accumulate(acc_even_ref)

    @pl.when((i > 0) & (i < ni) & (i % 2 == 1))
    def _():
        accumulate(acc_odd_ref)
        finish_slice(acc_even_ref)

    @pl.when((i > 0) & (i < ni) & (i % 2 == 0))
    def _():
        accumulate(acc_even_ref)
        finish_slice(acc_odd_ref)

    @pl.when(i == ni)
    def _():
        finish_slice(acc_odd_ref if (ni - 1) % 2 == 1 else acc_even_ref)


def _matmul_norm_res(a, w, x, gain, *, tm=512, tk, name):
    t, k = a.shape
    n = w.shape[1]
    ni, nk = t // tm, k // tk
    assert tm % (nk * NORM_ROWS) == 0
    slice_rows = tm // nk

    def slice_index(i, kk):
        return (jnp.maximum(i - 1, 0) * nk + jnp.where(i == 0, 0, kk), 0)

    return pl.pallas_call(
        functools.partial(_matmul_norm_res_kernel, ni=ni),
        grid=(ni + 1, nk),
        in_specs=[
            pl.BlockSpec((tm, tk), lambda i, kk: (jnp.minimum(i, ni - 1), jnp.where(i == ni, nk - 1, kk))),
            pl.BlockSpec((tk, n), lambda i, kk: (jnp.where(i == ni, nk - 1, kk), 0)),
            pl.BlockSpec((slice_rows, n), slice_index),
            pl.BlockSpec((1, n), lambda i, kk: (0, 0)),
        ],
        out_specs=pl.BlockSpec((slice_rows, n), slice_index),
        out_shape=jax.ShapeDtypeStruct((t, n), F32),
        scratch_shapes=[pltpu.VMEM((tm, n), F32), pltpu.VMEM((tm, n), F32)],
        compiler_params=_params(("arbitrary", "arbitrary"), V7X_VMEM_LIMIT_LARGE_BYTES),
        name=name,
    )(a, w, x, gain.reshape(1, n))


def _retention_kernel(ld_ref, q_ref, k_ref, v_ref, g_ref, cos_ref, sin_ref, gain_ref, o_ref,
                      qd_ref, y_ref, delta_ref, prev_ref, s_ref, kd_ref):
    seq = q_ref.shape[1]
    c = RET_CHUNK
    nc = seq // c
    grp = s_ref.shape[0]
    h = pl.program_id(1)
    ld_f = ld_ref[0, h]
    ld_b = ld_ref[1, h]

    ri = lax.broadcasted_iota(jnp.int32, (c, c), 0)
    ci = lax.broadcasted_iota(jnp.int32, (c, c), 1)
    diff = (ri - ci).astype(F32)
    mask = jnp.where(diff >= 0, jnp.exp(ld_f * jnp.maximum(diff, 0.0)), jnp.exp(ld_b * jnp.maximum(-diff, 0.0)))
    pos = lax.broadcasted_iota(jnp.int32, (c, RET_QK_DIM), 0).astype(F32)
    qdec_f = jnp.exp(ld_f * (pos + 1.0))
    kdec_f = jnp.exp(ld_f * (c - 1.0 - pos))
    qdec_b = jnp.exp(ld_b * (c - pos))
    kdec_b = jnp.exp(ld_b * pos)
    cd_f = jnp.exp(jnp.full((1, RET_V_DIM), ld_f * c, F32))
    cd_b = jnp.exp(jnp.full((1, RET_V_DIM), ld_b * c, F32))
    gain = gain_ref[0]
    k_scale = RET_QK_DIM ** -0.5
    half = RET_QK_DIM // 2

    def chunk_rows(n):
        return pl.ds(pl.multiple_of(n * c, c), c)

    def phase_a(gi, carry):
        chunks = [gi * grp + j for j in range(grp)]
        for j, n in enumerate(chunks):
            rows = chunk_rows(n)
            cos = cos_ref[rows, :]
            sin = sin_ref[rows, :]
            xq = q_ref[0, rows, :].astype(F32)
            xk = k_ref[0, rows, :].astype(F32)
            q = xq * cos + pltpu.roll(xq, half, 1) * sin
            k = (xk * cos + pltpu.roll(xk, half, 1) * sin) * k_scale
            qd_ref[rows, :] = jnp.concatenate([(q * qdec_f).astype(BF16), (q * qdec_b).astype(BF16)], axis=1)
            kd_ref[j] = jnp.concatenate([(k * kdec_f).astype(BF16), (k * kdec_b).astype(BF16)], axis=1)
            s_ref[j] = lax.dot_general(q.astype(BF16), k.astype(BF16), (((1,), (1,)), ((), ())),
                                       preferred_element_type=F32)
        for j, n in enumerate(chunks):
            rows = chunk_rows(n)
            y_ref[rows, :] = jnp.dot((s_ref[j] * mask).astype(BF16), v_ref[0, rows, :],
                                     preferred_element_type=F32)
        for j, n in enumerate(chunks):
            delta_ref[n] = lax.dot_general(kd_ref[j], v_ref[0, chunk_rows(n), :], (((0,), (0,)), ((), ())),
                                           preferred_element_type=F32)
        return carry

    lax.fori_loop(0, nc // grp, phase_a, 0)

    def scan_f(n, st):
        prev_ref[n, pl.ds(0, RET_QK_DIM), :] = st.astype(BF16)
        return st * cd_f + delta_ref[n, pl.ds(0, RET_QK_DIM), :]

    def scan_b(t, st):
        n = nc - 1 - t
        prev_ref[n, pl.ds(RET_QK_DIM, RET_QK_DIM), :] = st.astype(BF16)
        return st * cd_b + delta_ref[n, pl.ds(RET_QK_DIM, RET_QK_DIM), :]

    zero_state = jnp.zeros((RET_QK_DIM, RET_V_DIM), F32)
    lax.fori_loop(0, nc, scan_f, zero_state)
    lax.fori_loop(0, nc, scan_b, zero_state)

    def phase_c(gi, carry):
        chunks = [gi * grp + j for j in range(grp)]
        ys = []
        for n in chunks:
            rows = chunk_rows(n)
            ys.append(y_ref[rows, :] + jnp.dot(qd_ref[rows, :], prev_ref[n], preferred_element_type=F32))
        for n, y in zip(chunks, ys):
            rows = chunk_rows(n)
            ms = jnp.mean(y * y, axis=-1, keepdims=True)
            yn = (y * lax.rsqrt(ms + NORM_EPS)) * gain
            gate = g_ref[0, rows, :].astype(F32)
            o_ref[0, rows, :] = ((gate * jax.nn.sigmoid(gate)) * yn).astype(o_ref.dtype)
        return carry

    lax.fori_loop(0, nc // grp, phase_c, 0)


def _retention(proj, log_decay, gain, cos2, sin2):
    b, seq, _ = proj.shape
    qk_blocks = RET_QK_WIDTH // RET_QK_DIM
    v_block0 = 2 * RET_QK_WIDTH // RET_V_DIM
    g_block0 = v_block0 + RET_HEADS
    nc = seq // RET_CHUNK
    group = min(RET_CHUNK_GROUP, nc)
    return pl.pallas_call(
        _retention_kernel,
        grid=(b, RET_HEADS),
        in_specs=[
            pl.BlockSpec(memory_space=pltpu.SMEM),
            pl.BlockSpec((1, seq, RET_QK_DIM), lambda i, h: (i, 0, h)),
            pl.BlockSpec((1, seq, RET_QK_DIM), lambda i, h: (i, 0, qk_blocks + h)),
            pl.BlockSpec((1, seq, RET_V_DIM), lambda i, h: (i, 0, v_block0 + h)),
            pl.BlockSpec((1, seq, RET_V_DIM), lambda i, h: (i, 0, g_block0 + h)),
            pl.BlockSpec((seq, RET_QK_DIM), lambda i, h: (0, 0)),
            pl.BlockSpec((seq, RET_QK_DIM), lambda i, h: (0, 0)),
            pl.BlockSpec((1, 1, RET_V_DIM), lambda i, h: (h, 0, 0)),
        ],
        out_specs=pl.BlockSpec((1, seq, RET_V_DIM), lambda i, h: (i, 0, h)),
        out_shape=jax.ShapeDtypeStruct((b, seq, D_MODEL), BF16),
        scratch_shapes=[
            pltpu.VMEM((seq, 2 * RET_QK_DIM), BF16),
            pltpu.VMEM((seq, RET_V_DIM), F32),
            pltpu.VMEM((nc, 2 * RET_QK_DIM, RET_V_DIM), F32),
            pltpu.VMEM((nc, 2 * RET_QK_DIM, RET_V_DIM), BF16),
            pltpu.VMEM((group, RET_CHUNK, RET_CHUNK), F32),
            pltpu.VMEM((group, RET_CHUNK, 2 * RET_QK_DIM), BF16),
        ],
        compiler_params=_params(("arbitrary", "arbitrary")),
        name="retention",
    )(log_decay, proj, proj, proj, proj, cos2, sin2, gain.reshape(RET_HEADS, 1, RET_V_DIM))


COPY_ROWS = 128
RUN_NUM, RUN_MAX, RUN_DEN = 0, 1, 2
RUN_PARTS = 3


def _attention_kernel(q_ref, k_ref, v_ref, diag_ref, mix_ref, o_ref,
                      t_ref, stage_ref, stage4_ref, q1_ref, q4_ref, k4_ref, v4_ref, q16_ref, k16_ref, v16_ref,
                      run_a, run_b, s_ref, p_ref, den_ref, max_ref):
    del mix_ref
    seq = q_ref.shape[1]
    q_scale = ATT_HEAD_DIM ** -0.5
    len4 = seq // 4
    len16 = seq // 16
    group = s_ref.shape[0]

    @pl.when(pl.program_id(1) == 0)
    def _():
        for bi in range(len(DILATIONS)):
            wide = jnp.broadcast_to(diag_ref[0, bi], (ATT_QB, 2 * ATT_KW))
            for vi, off in enumerate(BIAS_OFFSETS):
                rolled = pltpu.roll(wide, ATT_KW - off, 1, stride=1, stride_axis=0)
                t_ref[bi, vi] = rolled[:, :ATT_KW]

    def stage(src_ref, scale, natural_ref):
        def body(cc, carry):
            rows = pl.ds(pl.multiple_of(cc * COPY_ROWS, COPY_ROWS), COPY_ROWS)
            x = src_ref[0, rows, :].astype(F32)
            if scale is not None:
                x = x * scale
                natural_ref[rows, :] = x.astype(BF16)
            stage_ref[rows, :] = x
            return carry

        lax.fori_loop(0, seq // COPY_ROWS, body, 0)

    def deinterleave(dst4_ref, dst16_ref):
        ch = min(COPY_ROWS, len4)
        for r in range(4):
            for cc in range(len4 // ch):
                x = stage_ref[pl.ds(r + 4 * cc * ch, ch, stride=4), :]
                rows = pl.ds(r * len4 + cc * ch, ch)
                stage4_ref[rows, :] = x
                dst4_ref[rows, :] = x.astype(BF16)
        ch = min(COPY_ROWS, len16)
        for r in range(4):
            for r_sub in range(4):
                for cc in range(len16 // ch):
                    x = stage4_ref[pl.ds(r * len4 + r_sub + 4 * cc * ch, ch, stride=4), :]
                    dst16_ref[pl.ds((r + 4 * r_sub) * len16 + cc * ch, ch), :] = x.astype(BF16)

    stage(q_ref, q_scale, q1_ref)
    deinterleave(q4_ref, q16_ref)
    stage(k_ref, None, None)
    deinterleave(k4_ref, k16_ref)
    stage(v_ref, None, None)
    deinterleave(v4_ref, v16_ref)

    def run_branch(bi, d, load_q, load_k, load_v, run_ref, first, last):
        length = seq // d
        kw = min(length, ATT_KW)
        nblk = length // ATT_QB

        def tile_rows(t):
            r = t // nblk
            nb = t - r * nblk
            i0 = nb * ATT_QB
            ws = jnp.clip(i0 - ATT_HALF, 0, length - kw)
            var = jnp.where(nb == 0, 0, jnp.where(nb == nblk - 1, 2, 1))
            base = r * length
            qrows = pl.ds(pl.multiple_of(base + i0, ATT_QB), ATT_QB)
            krows = pl.ds(pl.multiple_of(base + ws, ATT_HALF), kw)
            return qrows, krows, var

        def body(g, carry):
            tiles = [tile_rows(g * group + j) for j in range(group)]
            for j, (qrows, krows, var) in enumerate(tiles):
                s = lax.dot_general(load_q(qrows), load_k(krows), (((1,), (1,)), ((), ())),
                                    preferred_element_type=F32)
                s_ref[j, :, :kw] = s + t_ref[bi, var, :, :kw]
            for j, (qrows, krows, var) in enumerate(tiles):
                s = s_ref[j, :, :kw]
                m = jnp.max(s, axis=-1, keepdims=True)
                p = jnp.exp(s - m)
                den = jnp.sum(p, axis=-1, keepdims=True)
                p_ref[j, :, :kw] = p.astype(BF16)
                den_ref[j] = jnp.broadcast_to(den, (ATT_QB, ATT_HEAD_DIM))
                max_ref[j] = jnp.broadcast_to(m, (ATT_QB, ATT_HEAD_DIM))
            for j, (qrows, krows, var) in enumerate(tiles):
                num = jnp.dot(p_ref[j, :, :kw], load_v(krows), preferred_element_type=F32)
                den = den_ref[j]
                top = max_ref[j]
                if not first:
                    m_run = run_ref[RUN_MAX, qrows, :]
                    top = jnp.maximum(max_ref[j], m_run)
                    w_new = jnp.exp(max_ref[j] - top)
                    w_run = jnp.exp(m_run - top)
                    num = w_new * num + w_run * run_ref[RUN_NUM, qrows, :]
                    den = w_new * den + w_run * run_ref[RUN_DEN, qrows, :]
                if last:
                    o_ref[0, qrows, :] = (num / den).astype(o_ref.dtype)
                else:
                    run_ref[RUN_NUM, qrows, :] = num
                    run_ref[RUN_MAX, qrows, :] = top
                    run_ref[RUN_DEN, qrows, :] = den
            return carry

        lax.fori_loop(0, d * nblk // group, body, 0)

    def regroup(src_ref, dst_ref, n_src, n_dst):
        len_src = seq // n_src
        len_dst = seq // n_dst
        ch = min(COPY_ROWS, len_src)
        for r_dst in range(n_dst):
            for r_sub in range(4):
                r_src = r_dst + n_dst * r_sub
                for cc in range(len_src // ch):
                    src = pl.ds(r_src * len_src + cc * ch, ch)
                    dst = pl.ds(r_dst * len_dst + r_sub + 4 * cc * ch, ch, stride=4)
                    for part in range(RUN_PARTS):
                        dst_ref[part, dst, :] = src_ref[part, src, :]

    run_branch(2, 16, lambda rows: q16_ref[rows, :], lambda rows: k16_ref[rows, :], lambda rows: v16_ref[rows, :],
               run_a, first=True, last=False)
    regroup(run_a, run_b, 16, 4)
    run_branch(1, 4, lambda rows: q4_ref[rows, :], lambda rows: k4_ref[rows, :], lambda rows: v4_ref[rows, :],
               run_b, first=False, last=False)
    regroup(run_b, run_a, 4, 1)
    run_branch(0, 1, lambda rows: q1_ref[rows, :], lambda rows: k_ref[0, rows, :], lambda rows: v_ref[0, rows, :],
               run_a, first=False, last=True)


def _attention(proj, bias_diags, mix):
    b, seq, _ = proj.shape
    q_block0 = (2 * RET_QK_WIDTH + 2 * RET_WIDTH) // ATT_HEAD_DIM
    k_block0 = q_block0 + ATT_HEADS
    v_block0 = k_block0 + ATT_HEADS
    out_block0 = RET_WIDTH // ATT_HEAD_DIM
    nb, nv = len(DILATIONS), len(BIAS_OFFSETS)
    head_rows = pltpu.VMEM((seq, ATT_HEAD_DIM), BF16)
    head_rows_f32 = pltpu.VMEM((seq, ATT_HEAD_DIM), F32)
    group = min(ATT_TILE_GROUP, seq // ATT_QB)
    return pl.pallas_call(
        _attention_kernel,
        grid=(ATT_HEADS, b),
        in_specs=[
            pl.BlockSpec((1, seq, ATT_HEAD_DIM), lambda h, i: (i, 0, q_block0 + h)),
            pl.BlockSpec((1, seq, ATT_HEAD_DIM), lambda h, i: (i, 0, k_block0 + h)),
            pl.BlockSpec((1, seq, ATT_HEAD_DIM), lambda h, i: (i, 0, v_block0 + h)),
            pl.BlockSpec((1, nb, 1, 2 * ATT_KW), lambda h, i: (h, 0, 0, 0)),
            pl.BlockSpec(memory_space=pl.ANY),
        ],
        out_specs=pl.BlockSpec((1, seq, ATT_HEAD_DIM), lambda h, i: (i, 0, out_block0 + h)),
        out_shape=jax.ShapeDtypeStruct((b, seq, D_MODEL), BF16),
        scratch_shapes=[
            pltpu.VMEM((nb, nv, ATT_QB, ATT_KW), F32),
            head_rows_f32,
            head_rows_f32,
            head_rows,
            head_rows, head_rows, head_rows,
            head_rows, head_rows, head_rows,
            pltpu.VMEM((RUN_PARTS, seq, ATT_HEAD_DIM), F32),
            pltpu.VMEM((RUN_PARTS, seq, ATT_HEAD_DIM), F32),
            pltpu.VMEM((group, ATT_QB, ATT_KW), F32),
            pltpu.VMEM((group, ATT_QB, ATT_KW), BF16),
            pltpu.VMEM((group, ATT_QB, ATT_HEAD_DIM), F32),
            pltpu.VMEM((group, ATT_QB, ATT_HEAD_DIM), F32),
        ],
        input_output_aliases={4: 0},
        compiler_params=_params(("arbitrary", "arbitrary")),
        name="dilated_attention",
    )(proj, proj, proj, bias_diags, mix)


def _rotary_tables(seq):
    half = RET_QK_DIM // 2
    inv = ROPE_BASE ** (-jnp.arange(half, dtype=F32) / half)
    ang = jnp.arange(seq, dtype=F32)[:, None] * inv[None, :]
    cos, sin = jnp.cos(ang), jnp.sin(ang)
    return jnp.concatenate([cos, cos], axis=-1), jnp.concatenate([-sin, sin], axis=-1)


def _bucket_of(rel):
    nbk = REL_BUCKETS // 2
    max_exact = nbk // 2
    n = np.abs(rel)
    nf = np.maximum(n, 1).astype(np.float32)
    large = max_exact + (np.log(nf / max_exact) / math.log(REL_MAX_DISTANCE / max_exact)
                         * (nbk - max_exact)).astype(np.int32)
    large = np.minimum(large, nbk - 1)
    return np.where(rel > 0, nbk, 0) + np.where(n < max_exact, n, large)


def _bias_diags(rel_bias_table):
    delta = np.arange(2 * ATT_KW) - ATT_KW
    buckets = np.stack([_bucket_of(delta * d) for d in DILATIONS])
    inside = np.abs(delta) <= ATT_HALF
    bias = jnp.transpose(rel_bias_table.astype(F32))[:, buckets]
    return jnp.where(inside[None, None], bias, NEG_INF)[:, :, None, :]


def _trunk(x, bias_diags, layers, stacks, matrices, cast_ahead):
    b, seq, dm = x.shape
    cos2, sin2 = _rotary_tables(seq)
    xt = x.reshape(b * seq, dm)
    for l, layer in enumerate(layers):
        have = matrices[l]
        todo = [name for name in ("w_out", "w_up") if name not in have] if cast_ahead else []
        proj, done = _norm_matmul(xt, layer["norm_mix_pre"], have["w_in"], relu2=False,
                                  casts=[(stacks[name], l) for name in todo], name="in_proj")
        have.update(zip(todo, done))
        proj = proj.reshape(b, seq, IN_WIDTH)
        mix = _retention(proj, layer["log_decay"], layer["ret_norm_gain"], cos2, sin2)
        mix = _attention(proj, bias_diags, mix)
        xt = _matmul_norm_res(mix.reshape(b * seq, dm), have["w_out"], xt, layer["norm_mix_post"], tk=1024,
                              name="out_proj")
        todo = [(l, "w_down")] if cast_ahead and "w_down" not in have else []
        if cast_ahead and l + 1 < len(layers) and "w_in" not in matrices[l + 1]:
            todo.append((l + 1, "w_in"))
        u, done = _norm_matmul(xt, layer["norm_mlp_pre"], have["w_up"], relu2=True,
                               casts=[(stacks[name], m) for m, name in todo], name="up_proj")
        for (m, name), mat in zip(todo, done):
            matrices[m][name] = mat
        xt = _matmul_norm_res(u, have["w_down"], xt, layer["norm_mlp_post"], tk=2048, name="down_proj")
    return xt.reshape(b, seq, dm)


def kernel(x_prompt, x_sample, rel_bias_table, w_in, ret_log_decay, ret_norm_gain, w_out, w_up, w_down,
           norm_mix_pre, norm_mix_post, norm_mlp_pre, norm_mlp_post):
    depth = w_in.shape[0]
    stacks = {"w_in": w_in.astype(F32), "w_out": w_out.astype(F32), "w_up": w_up.astype(F32),
              "w_down": w_down.astype(F32)}
    layers = []
    for l in range(depth):
        layers.append({
            "log_decay": -jnp.exp(ret_log_decay[l].astype(F32)),
            "ret_norm_gain": ret_norm_gain[l].astype(F32),
            "norm_mix_pre": norm_mix_pre[l].astype(F32),
            "norm_mix_post": norm_mix_post[l].astype(F32),
            "norm_mlp_pre": norm_mlp_pre[l].astype(F32),
            "norm_mlp_post": norm_mlp_post[l].astype(F32),
        })
    matrices = [{} for _ in range(depth)]
    matrices[0]["w_in"] = stacks["w_in"][0].astype(BF16)
    bias_diags = _bias_diags(rel_bias_table)
    y_prompt = _trunk(x_prompt, bias_diags, layers, stacks, matrices, cast_ahead=True)
    y_sample = _trunk(x_sample, bias_diags, layers, stacks, matrices, cast_ahead=False)
    return (y_prompt, y_sample)
```

```python
import functools
import math

import numpy as np
import jax
import jax.numpy as jnp
from jax import lax
from jax.experimental import pallas as pl
from jax.experimental.pallas import tpu as pltpu

BF16 = jnp.bfloat16
F32 = jnp.float32

RET_HEADS = 8
RET_QK_DIM = 128
RET_V_DIM = 256
RET_QK_WIDTH = RET_HEADS * RET_QK_DIM
RET_WIDTH = RET_HEADS * RET_V_DIM
ATT_HEADS = 16
ATT_HEAD_DIM = 128
ATT_WIDTH = ATT_HEADS * ATT_HEAD_DIM
IN_WIDTH = 2 * RET_QK_WIDTH + 2 * RET_WIDTH + 3 * ATT_WIDTH
RET_CHUNK = 128
RET_CHUNK_GROUP = 32
ROPE_BASE = 10000.0
DILATIONS = (1, 4, 16)
ATT_HALF = 64
REL_BUCKETS = 32
REL_MAX_DISTANCE = 1024
NORM_EPS = 1e-6
NEG_INF = -1e30

ATT_QB = 128
ATT_KW = 256
BIAS_OFFSETS = (0, -ATT_HALF, -2 * ATT_HALF)
ATT_TILE_GROUP = 32

V7X_VMEM_LIMIT_BYTES = 56 * 1024 * 1024
V7X_VMEM_LIMIT_LARGE_BYTES = 60 * 1024 * 1024


def _params(sem, vmem=V7X_VMEM_LIMIT_BYTES):
    return pltpu.CompilerParams(dimension_semantics=sem, vmem_limit_bytes=vmem)


NORM_ROWS = 16
NORM_SLICES = 8


def _norm_matmul_kernel(xs_ref, g_ref, w_ref, *refs, relu2, n_casts):
    cast_in_refs = refs[:n_casts]
    o_ref = refs[n_casts]
    cast_out_refs = refs[n_casts + 1:2 * n_casts + 1]
    h_even_ref, h_odd_ref = refs[2 * n_casts + 1:]
    i = pl.program_id(0)
    j = pl.program_id(1)
    slice_rows = xs_ref.shape[0]

    def cast_blocks():
        for src_ref, dst_ref in zip(cast_in_refs, cast_out_refs):
            dst_ref[...] = src_ref[...].astype(BF16)

    def normalise_slice(h_ref):
        gain = g_ref[...]
        base = jnp.minimum(j, NORM_SLICES - 1) * slice_rows
        for c in range(slice_rows // NORM_ROWS):
            x = xs_ref[pl.ds(c * NORM_ROWS, NORM_ROWS), :]
            ms = jnp.mean(x * x, axis=-1, keepdims=True)
            rows = pl.ds(pl.multiple_of(base + c * NORM_ROWS, NORM_ROWS), NORM_ROWS)
            h_ref[rows, :] = ((x * lax.rsqrt(ms + NORM_EPS)) * gain).astype(BF16)

    def multiply(h_ref):
        y = jnp.dot(h_ref[...], w_ref[...], preferred_element_type=F32)
        if relu2:
            y = jnp.maximum(y, 0.0)
            y = y * y
        o_ref[...] = y.astype(o_ref.dtype)

    @pl.when(i == 0)
    def _():
        cast_blocks()
        normalise_slice(h_even_ref)

    for parity, h_write_ref, h_read_ref in ((1, h_odd_ref, h_even_ref), (0, h_even_ref, h_odd_ref)):
        for slicing in (True, False):
            @pl.when((i > 0) & (i % 2 == parity) & ((j < NORM_SLICES) == slicing))
            def _(h_write_ref=h_write_ref, h_read_ref=h_read_ref, slicing=slicing):
                cast_blocks()
                if slicing:
                    normalise_slice(h_write_ref)
                multiply(h_read_ref)


def _norm_matmul(x, gain, w, *, relu2, casts=(), tm=1024, tn=1024, name):
    t, k = x.shape
    n = w.shape[1]
    ni, nj = t // tm, n // tn
    assert nj >= NORM_SLICES and tm % (NORM_SLICES * NORM_ROWS) == 0
    slice_rows = tm // NORM_SLICES
    steps = (ni + 1) * nj
    cast_in_specs, cast_out_specs = [], []
    for stack, layer in casts:
        _, rows, cols = stack.shape
        n_blocks = max(nb for nb in (8, 16, 32, 64, 128, 256, 512) if nb <= steps and rows % (nb * NORM_ROWS) == 0)
        block_rows = rows // n_blocks
        cast_in_specs.append(pl.BlockSpec(
            (None, block_rows, cols),
            lambda i, j, layer=layer, last=n_blocks - 1: (layer, jnp.minimum(i * nj + j, last), 0)))
        cast_out_specs.append(pl.BlockSpec(
            (block_rows, cols), lambda i, j, last=n_blocks - 1: (jnp.minimum(i * nj + j, last), 0)))
    outs = pl.pallas_call(
        functools.partial(_norm_matmul_kernel, relu2=relu2, n_casts=len(casts)),
        grid=(ni + 1, nj),
        in_specs=[
            pl.BlockSpec((slice_rows, k),
                         lambda i, j: (jnp.minimum(i, ni - 1) * NORM_SLICES + jnp.minimum(j, NORM_SLICES - 1), 0)),
            pl.BlockSpec((1, k), lambda i, j: (0, 0)),
            pl.BlockSpec((k, tn), lambda i, j: (0, jnp.where(i == 0, 0, j))),
            *cast_in_specs,
        ],
        out_specs=[pl.BlockSpec((tm, tn), lambda i, j: (jnp.maximum(i - 1, 0), jnp.where(i == 0, 0, j))),
                   *cast_out_specs],
        out_shape=[jax.ShapeDtypeStruct((t, n), BF16),
                   *(jax.ShapeDtypeStruct(stack.shape[1:], BF16) for stack, _ in casts)],
        scratch_shapes=[pltpu.VMEM((tm, k), BF16), pltpu.VMEM((tm, k), BF16)],
        compiler_params=_params(("arbitrary", "arbitrary")),
        name=name,
    )(x, gain.reshape(1, k), w, *(stack for stack, _ in casts))
    return outs[0], tuple(outs[1:])


def _matmul_norm_res_kernel(*refs, ni, first_blocks):
    a_refs = refs[:-6]
    w_ref, xs_ref, g_ref, os_ref, acc_even_ref, acc_odd_ref = refs[-6:]
    i = pl.program_id(0)
    kk = pl.program_id(1)
    slice_rows = os_ref.shape[0]

    def left_block():
        if len(a_refs) == 1:
            return a_refs[0][...]
        return jnp.where(kk < first_blocks, a_refs[0][...], a_refs[1][...])

    @pl.when((i == 0) & (kk == 0))
    def _():
        acc_even_ref[...] = jnp.zeros_like(acc_even_ref)
        acc_odd_ref[...] = jnp.zeros_like(acc_odd_ref)

    def accumulate(acc_ref):
        part = jnp.dot(left_block(), w_ref[...], preferred_element_type=F32)
        acc_ref[...] = jnp.where(kk == 0, part, acc_ref[...] + part)

    def finish_slice(acc_ref):
        gain = g_ref[...]
        for c in range(slice_rows // NORM_ROWS):
            rows = pl.ds(pl.multiple_of(kk * slice_rows + c * NORM_ROWS, NORM_ROWS), NORM_ROWS)
            y = acc_ref[rows, :]
            ms = jnp.mean(y * y, axis=-1, keepdims=True)
            local = pl.ds(c * NORM_ROWS, NORM_ROWS)
            os_ref[local, :] = xs_ref[local, :] + (y * lax.rsqrt(ms + NORM_EPS)) * gain

    @pl.when(i == 0)
    def _():
        accumulate(acc_even_ref)

    @pl.when((i > 0) & (i < ni) & (i % 2 == 1))
    def _():
        accumulate(acc_odd_ref)
        finish_slice(acc_even_ref)

    @pl.when((i > 0) & (i < ni) & (i % 2 == 0))
    def _():
        accumulate(acc_even_ref)
        finish_slice(acc_odd_ref)

    @pl.when(i == ni)
    def _():
        finish_slice(acc_odd_ref if (ni - 1) % 2 == 1 else acc_even_ref)


def _matmul_norm_res(a_parts, w, x, gain, *, tm=512, tk, name):
    t = a_parts[0].shape[0]
    k, n = w.shape
    ni, nk = t // tm, k // tk
    assert tm % (nk * NORM_ROWS) == 0 and sum(a.shape[1] for a in a_parts) == k
    slice_rows = tm // nk
    first_blocks = a_parts[0].shape[1] // tk
    assert first_blocks * tk == a_parts[0].shape[1]

    def slice_index(i, kk):
        return (jnp.maximum(i - 1, 0) * nk + jnp.where(i == 0, 0, kk), 0)

    def left_spec(lo, hi):
        def index(i, kk):
            step = jnp.where(i == ni, nk - 1, kk)
            return (jnp.minimum(i, ni - 1), jnp.clip(step, lo, hi - 1) - lo)
        return pl.BlockSpec((tm, tk), index)

    bounds = [0, first_blocks, nk] if len(a_parts) == 2 else [0, nk]
    return pl.pallas_call(
        functools.partial(_matmul_norm_res_kernel, ni=ni, first_blocks=first_blocks),
        grid=(ni + 1, nk),
        in_specs=[
            *(left_spec(lo, hi) for lo, hi in zip(bounds[:-1], bounds[1:])),
            pl.BlockSpec((tk, n), lambda i, kk: (jnp.where(i == ni, nk - 1, kk), 0)),
            pl.BlockSpec((slice_rows, n), slice_index),
            pl.BlockSpec((1, n), lambda i, kk: (0, 0)),
        ],
        out_specs=pl.BlockSpec((slice_rows, n), slice_index),
        out_shape=jax.ShapeDtypeStruct((t, n), F32),
        scratch_shapes=[pltpu.VMEM((tm, n), F32), pltpu.VMEM((tm, n), F32)],
        compiler_params=_params(("arbitrary", "arbitrary"), V7X_VMEM_LIMIT_LARGE_BYTES),
        name=name,
    )(*a_parts, w, x, gain.reshape(1, n))


def _retention_kernel(ld_ref, q_ref, k_ref, v_ref, g_ref, cos_ref, sin_ref, gain_ref, o_ref,
                      qd_ref, y_ref, delta_ref, prev_ref, s_ref, kd_ref):
    seq = q_ref.shape[1]
    c = RET_CHUNK
    nc = seq // c
    grp = s_ref.shape[0]
    h = pl.program_id(1)
    ld_f = ld_ref[0, h]
    ld_b = ld_ref[1, h]

    ri = lax.broadcasted_iota(jnp.int32, (c, c), 0)
    ci = lax.broadcasted_iota(jnp.int32, (c, c), 1)
    diff = (ri - ci).astype(F32)
    mask = jnp.where(diff >= 0, jnp.exp(ld_f * jnp.maximum(diff, 0.0)), jnp.exp(ld_b * jnp.maximum(-diff, 0.0)))
    pos = lax.broadcasted_iota(jnp.int32, (c, RET_QK_DIM), 0).astype(F32)
    qdec_f = jnp.exp(ld_f * (pos + 1.0))
    kdec_f = jnp.exp(ld_f * (c - 1.0 - pos))
    qdec_b = jnp.exp(ld_b * (c - pos))
    kdec_b = jnp.exp(ld_b * pos)
    cd_f = jnp.exp(jnp.full((1, RET_V_DIM), ld_f * c, F32))
    cd_b = jnp.exp(jnp.full((1, RET_V_DIM), ld_b * c, F32))
    gain = gain_ref[0]
    k_scale = RET_QK_DIM ** -0.5
    half = RET_QK_DIM // 2

    def chunk_rows(n):
        return pl.ds(pl.multiple_of(n * c, c), c)

    def phase_a(gi, carry):
        chunks = [gi * grp + j for j in range(grp)]
        for j, n in enumerate(chunks):
            rows = chunk_rows(n)
            cos = cos_ref[rows, :]
            sin = sin_ref[rows, :]
            xq = q_ref[0, rows, :].astype(F32)
            xk = k_ref[0, rows, :].astype(F32)
            q = xq * cos + pltpu.roll(xq, half, 1) * sin
            k = (xk * cos + pltpu.roll(xk, half, 1) * sin) * k_scale
            qd_ref[rows, :] = jnp.concatenate([(q * qdec_f).astype(BF16), (q * qdec_b).astype(BF16)], axis=1)
            kd_ref[j] = jnp.concatenate([(k * kdec_f).astype(BF16), (k * kdec_b).astype(BF16)], axis=1)
            s_ref[j] = lax.dot_general(q.astype(BF16), k.astype(BF16), (((1,), (1,)), ((), ())),
                                       preferred_element_type=F32)
        for j, n in enumerate(chunks):
            rows = chunk_rows(n)
            y_ref[rows, :] = jnp.dot((s_ref[j] * mask).astype(BF16), v_ref[0, rows, :],
                                     preferred_element_type=F32)
        for j, n in enumerate(chunks):
            delta_ref[n] = lax.dot_general(kd_ref[j], v_ref[0, chunk_rows(n), :], (((0,), (0,)), ((), ())),
                                           preferred_element_type=F32)
        return carry

    lax.fori_loop(0, nc // grp, phase_a, 0)

    def scan_f(n, st):
        prev_ref[n, pl.ds(0, RET_QK_DIM), :] = st.astype(BF16)
        return st * cd_f + delta_ref[n, pl.ds(0, RET_QK_DIM), :]

    def scan_b(t, st):
        n = nc - 1 - t
        prev_ref[n, pl.ds(RET_QK_DIM, RET_QK_DIM), :] = st.astype(BF16)
        return st * cd_b + delta_ref[n, pl.ds(RET_QK_DIM, RET_QK_DIM), :]

    zero_state = jnp.zeros((RET_QK_DIM, RET_V_DIM), F32)
    lax.fori_loop(0, nc, scan_f, zero_state)
    lax.fori_loop(0, nc, scan_b, zero_state)

    def phase_c(gi, carry):
        chunks = [gi * grp + j for j in range(grp)]
        ys = []
        for n in chunks:
            rows = chunk_rows(n)
            ys.append(y_ref[rows, :] + jnp.dot(qd_ref[rows, :], prev_ref[n], preferred_element_type=F32))
        for n, y in zip(chunks, ys):
            rows = chunk_rows(n)
            ms = jnp.mean(y * y, axis=-1, keepdims=True)
            yn = (y * lax.rsqrt(ms + NORM_EPS)) * gain
            gate = g_ref[0, rows, :].astype(F32)
            o_ref[0, rows, :] = ((gate * jax.nn.sigmoid(gate)) * yn).astype(o_ref.dtype)
        return carry

    lax.fori_loop(0, nc // grp, phase_c, 0)


def _retention(proj, log_decay, gain, cos2, sin2):
    b, seq, _ = proj.shape
    qk_blocks = RET_QK_WIDTH // RET_QK_DIM
    v_block0 = 2 * RET_QK_WIDTH // RET_V_DIM
    g_block0 = v_block0 + RET_HEADS
    nc = seq // RET_CHUNK
    group = min(RET_CHUNK_GROUP, nc)
    return pl.pallas_call(
        _retention_kernel,
        grid=(b, RET_HEADS),
        in_specs=[
            pl.BlockSpec(memory_space=pltpu.SMEM),
            pl.BlockSpec((1, seq, RET_QK_DIM), lambda i, h: (i, 0, h)),
            pl.BlockSpec((1, seq, RET_QK_DIM), lambda i, h: (i, 0, qk_blocks + h)),
            pl.BlockSpec((1, seq, RET_V_DIM), lambda i, h: (i, 0, v_block0 + h)),
            pl.BlockSpec((1, seq, RET_V_DIM), lambda i, h: (i, 0, g_block0 + h)),
            pl.BlockSpec((seq, RET_QK_DIM), lambda i, h: (0, 0)),
            pl.BlockSpec((seq, RET_QK_DIM), lambda i, h: (0, 0)),
            pl.BlockSpec((1, 1, RET_V_DIM), lambda i, h: (h, 0, 0)),
        ],
        out_specs=pl.BlockSpec((1, seq, RET_V_DIM), lambda i, h: (i, 0, h)),
        out_shape=jax.ShapeDtypeStruct((b, seq, RET_WIDTH), BF16),
        scratch_shapes=[
            pltpu.VMEM((seq, 2 * RET_QK_DIM), BF16),
            pltpu.VMEM((seq, RET_V_DIM), F32),
            pltpu.VMEM((nc, 2 * RET_QK_DIM, RET_V_DIM), F32),
            pltpu.VMEM((nc, 2 * RET_QK_DIM, RET_V_DIM), BF16),
            pltpu.VMEM((group, RET_CHUNK, RET_CHUNK), F32),
            pltpu.VMEM((group, RET_CHUNK, 2 * RET_QK_DIM), BF16),
        ],
        compiler_params=_params(("arbitrary", "arbitrary")),
        name="retention",
    )(log_decay, proj, proj, proj, proj, cos2, sin2, gain.reshape(RET_HEADS, 1, RET_V_DIM))


COPY_ROWS = 128
RUN_NUM, RUN_MAX, RUN_DEN = 0, 1, 2
RUN_PARTS = 3


def _attention_kernel(q_ref, k_ref, v_ref, diag_ref, o_ref,
                      t_ref, stage_ref, stage4_ref, q1_ref, q4_ref, k4_ref, v4_ref, q16_ref, k16_ref, v16_ref,
                      run_a, run_b, s_ref, p_ref, max_ref):
    seq = q_ref.shape[1]
    q_scale = ATT_HEAD_DIM ** -0.5
    len4 = seq // 4
    len16 = seq // 16
    group = s_ref.shape[0]

    @pl.when(pl.program_id(1) == 0)
    def _():
        for bi in range(len(DILATIONS)):
            wide = jnp.broadcast_to(diag_ref[0, bi], (ATT_QB, 2 * ATT_KW))
            for vi, off in enumerate(BIAS_OFFSETS):
                rolled = pltpu.roll(wide, ATT_KW - off, 1, stride=1, stride_axis=0)
                t_ref[bi, vi] = rolled[:, :ATT_KW]

    def stage(src_ref, scale, natural_ref):
        def body(cc, carry):
            rows = pl.ds(pl.multiple_of(cc * COPY_ROWS, COPY_ROWS), COPY_ROWS)
            x = src_ref[0, rows, :].astype(F32)
            if scale is not None:
                x = x * scale
                natural_ref[rows, :] = x.astype(BF16)
            stage_ref[rows, :] = x
            return carry

        lax.fori_loop(0, seq // COPY_ROWS, body, 0)

    def deinterleave(dst4_ref, dst16_ref):
        ch = min(COPY_ROWS, len4)
        for r in range(4):
            for cc in range(len4 // ch):
                x = stage_ref[pl.ds(r + 4 * cc * ch, ch, stride=4), :]
                rows = pl.ds(r * len4 + cc * ch, ch)
                stage4_ref[rows, :] = x
                dst4_ref[rows, :] = x.astype(BF16)
        ch = min(COPY_ROWS, len16)
        for r in range(4):
            for r_sub in range(4):
                for cc in range(len16 // ch):
                    x = stage4_ref[pl.ds(r * len4 + r_sub + 4 * cc * ch, ch, stride=4), :]
                    dst16_ref[pl.ds((r + 4 * r_sub) * len16 + cc * ch, ch), :] = x.astype(BF16)

    stage(q_ref, q_scale, q1_ref)
    deinterleave(q4_ref, q16_ref)
    stage(k_ref, None, None)
    deinterleave(k4_ref, k16_ref)
    stage(v_ref, None, None)
    deinterleave(v4_ref, v16_ref)

    def run_branch(bi, d, load_q, load_k, load_v, run_ref, first, last):
        length = seq // d
        kw = min(length, ATT_KW)
        nblk = length // ATT_QB

        def tile_rows(t):
            r = t // nblk
            nb = t - r * nblk
            i0 = nb * ATT_QB
            ws = jnp.clip(i0 - ATT_HALF, 0, length - kw)
            var = jnp.where(nb == 0, 0, jnp.where(nb == nblk - 1, 2, 1))
            base = r * length
            qrows = pl.ds(pl.multiple_of(base + i0, ATT_QB), ATT_QB)
            krows = pl.ds(pl.multiple_of(base + ws, ATT_HALF), kw)
            return qrows, krows, var

        def body(g, carry):
            tiles = [tile_rows(g * group + j) for j in range(group)]
            for j, (qrows, krows, var) in enumerate(tiles):
                s = lax.dot_general(load_q(qrows), load_k(krows), (((1,), (1,)), ((), ())),
                                    preferred_element_type=F32)
                s_ref[j, :, :kw] = s + t_ref[bi, var, :, :kw]
            for j, (qrows, krows, var) in enumerate(tiles):
                s = s_ref[j, :, :kw]
                m = jnp.max(s, axis=-1, keepdims=True)
                p_ref[j, :, :kw] = jnp.exp((s - m).astype(BF16))
                max_ref[j] = jnp.broadcast_to(m, (ATT_QB, ATT_HEAD_DIM))
            for j, (qrows, krows, var) in enumerate(tiles):
                v_ones = jnp.concatenate([load_v(krows), jnp.ones((kw, ATT_HEAD_DIM), BF16)], axis=1)
                both = jnp.dot(p_ref[j, :, :kw], v_ones, preferred_element_type=F32)
                num = both[:, :ATT_HEAD_DIM]
                den = both[:, ATT_HEAD_DIM:]
                top = max_ref[j]
                if not first:
                    m_run = run_ref[RUN_MAX, qrows, :]
                    top = jnp.maximum(max_ref[j], m_run)
                    w_new = jnp.exp(max_ref[j] - top)
                    w_run = jnp.exp(m_run - top)
                    num = w_new * num + w_run * run_ref[RUN_NUM, qrows, :]
                    den = w_new * den + w_run * run_ref[RUN_DEN, qrows, :]
                if last:
                    o_ref[0, qrows, :] = (num / den).astype(o_ref.dtype)
                else:
                    run_ref[RUN_NUM, qrows, :] = num
                    run_ref[RUN_MAX, qrows, :] = top
                    run_ref[RUN_DEN, qrows, :] = den
            return carry

        lax.fori_loop(0, d * nblk // group, body, 0)

    def regroup(src_ref, dst_ref, n_src, n_dst):
        len_src = seq // n_src
        len_dst = seq // n_dst
        ch = min(COPY_ROWS, len_src)
        for r_dst in range(n_dst):
            for r_sub in range(4):
                r_src = r_dst + n_dst * r_sub
                for cc in range(len_src // ch):
                    src = pl.ds(r_src * len_src + cc * ch, ch)
                    dst = pl.ds(r_dst * len_dst + r_sub + 4 * cc * ch, ch, stride=4)
                    for part in range(RUN_PARTS):
                        dst_ref[part, dst, :] = src_ref[part, src, :]

    run_branch(2, 16, lambda rows: q16_ref[rows, :], lambda rows: k16_ref[rows, :], lambda rows: v16_ref[rows, :],
               run_a, first=True, last=False)
    regroup(run_a, run_b, 16, 4)
    run_branch(1, 4, lambda rows: q4_ref[rows, :], lambda rows: k4_ref[rows, :], lambda rows: v4_ref[rows, :],
               run_b, first=False, last=False)
    regroup(run_b, run_a, 4, 1)
    run_branch(0, 1, lambda rows: q1_ref[rows, :], lambda rows: k_ref[0, rows, :], lambda rows: v_ref[0, rows, :],
               run_a, first=False, last=True)


def _attention(proj, bias_diags):
    b, seq, _ = proj.shape
    q_block0 = (2 * RET_QK_WIDTH + 2 * RET_WIDTH) // ATT_HEAD_DIM
    k_block0 = q_block0 + ATT_HEADS
    v_block0 = k_block0 + ATT_HEADS
    nb, nv = len(DILATIONS), len(BIAS_OFFSETS)
    head_rows = pltpu.VMEM((seq, ATT_HEAD_DIM), BF16)
    head_rows_f32 = pltpu.VMEM((seq, ATT_HEAD_DIM), F32)
    group = min(ATT_TILE_GROUP, seq // ATT_QB)
    return pl.pallas_call(
        _attention_kernel,
        grid=(ATT_HEADS, b),
        in_specs=[
            pl.BlockSpec((1, seq, ATT_HEAD_DIM), lambda h, i: (i, 0, q_block0 + h)),
            pl.BlockSpec((1, seq, ATT_HEAD_DIM), lambda h, i: (i, 0, k_block0 + h)),
            pl.BlockSpec((1, seq, ATT_HEAD_DIM), lambda h, i: (i, 0, v_block0 + h)),
            pl.BlockSpec((1, nb, 1, 2 * ATT_KW), lambda h, i: (h, 0, 0, 0)),
        ],
        out_specs=pl.BlockSpec((1, seq, ATT_HEAD_DIM), lambda h, i: (i, 0, h)),
        out_shape=jax.ShapeDtypeStruct((b, seq, ATT_WIDTH), BF16),
        scratch_shapes=[
            pltpu.VMEM((nb, nv, ATT_QB, ATT_KW), F32),
            head_rows_f32,
            head_rows_f32,
            head_rows,
            head_rows, head_rows, head_rows,
            head_rows, head_rows, head_rows,
            pltpu.VMEM((RUN_PARTS, seq, ATT_HEAD_DIM), F32),
            pltpu.VMEM((RUN_PARTS, seq, ATT_HEAD_DIM), F32),
            pltpu.VMEM((group, ATT_QB, ATT_KW), F32),
            pltpu.VMEM((group, ATT_QB, ATT_KW), BF16),
            pltpu.VMEM((group, ATT_QB, ATT_HEAD_DIM), F32),
        ],
        compiler_params=_params(("arbitrary", "arbitrary")),
        name="dilated_attention",
    )(proj, proj, proj, bias_diags)


def _rotary_tables(seq):
    half = RET_QK_DIM // 2
    inv = ROPE_BASE ** (-jnp.arange(half, dtype=F32) / half)
    ang = jnp.arange(seq, dtype=F32)[:, None] * inv[None, :]
    cos, sin = jnp.cos(ang), jnp.sin(ang)
    return jnp.concatenate([cos, cos], axis=-1), jnp.concatenate([-sin, sin], axis=-1)


def _bucket_of(rel):
    nbk = REL_BUCKETS // 2
    max_exact = nbk // 2
    n = np.abs(rel)
    nf = np.maximum(n, 1).astype(np.float32)
    large = max_exact + (np.log(nf / max_exact) / math.log(REL_MAX_DISTANCE / max_exact)
                         * (nbk - max_exact)).astype(np.int32)
    large = np.minimum(large, nbk - 1)
    return np.where(rel > 0, nbk, 0) + np.where(n < max_exact, n, large)


def _bias_diags(rel_bias_table):
    delta = np.arange(2 * ATT_KW) - ATT_KW
    buckets = np.stack([_bucket_of(delta * d) for d in DILATIONS])
    inside = np.abs(delta) <= ATT_HALF
    bias = jnp.transpose(rel_bias_table.astype(F32))[:, buckets]
    return jnp.where(inside[None, None], bias, NEG_INF)[:, :, None, :]


def _trunk(x, bias_diags, layers, stacks, matrices, cast_ahead):
    b, seq, dm = x.shape
    cos2, sin2 = _rotary_tables(seq)
    xt = x.reshape(b * seq, dm)
    for l, layer in enumerate(layers):
        have = matrices[l]
        todo = [name for name in ("w_out", "w_up") if name not in have] if cast_ahead else []
        proj, done = _norm_matmul(xt, layer["norm_mix_pre"], have["w_in"], relu2=False,
                                  casts=[(stacks[name], l) for name in todo], name="in_proj")
        have.update(zip(todo, done))
        proj = proj.reshape(b, seq, IN_WIDTH)
        y_ret = _retention(proj, layer["log_decay"], layer["ret_norm_gain"], cos2, sin2)
        y_att = _attention(proj, bias_diags)
        mixed = (y_ret.reshape(b * seq, RET_WIDTH), y_att.reshape(b * seq, ATT_WIDTH))
        xt = _matmul_norm_res(mixed, have["w_out"], xt, layer["norm_mix_post"], tk=1024, name="out_proj")
        todo = [(l, "w_down")] if cast_ahead and "w_down" not in have else []
        if cast_ahead and l + 1 < len(layers) and "w_in" not in matrices[l + 1]:
            todo.append((l + 1, "w_in"))
        u, done = _norm_matmul(xt, layer["norm_mlp_pre"], have["w_up"], relu2=True,
                               casts=[(stacks[name], m) for m, name in todo], name="up_proj")
        for (m, name), mat in zip(todo, done):
            matrices[m][name] = mat
        xt = _matmul_norm_res((u,), have["w_down"], xt, layer["norm_mlp_post"], tk=2048, name="down_proj")
    return xt.reshape(b, seq, dm)


def kernel(x_prompt, x_sample, rel_bias_table, w_in, ret_log_decay, ret_norm_gain, w_out, w_up, w_down,
           norm_mix_pre, norm_mix_post, norm_mlp_pre, norm_mlp_post):
    depth = w_in.shape[0]
    stacks = {"w_in": w_in.astype(F32), "w_out": w_out.astype(F32), "w_up": w_up.astype(F32),
              "w_down": w_down.astype(F32)}
    layers = []
    for l in range(depth):
        layers.append({
            "log_decay": -jnp.exp(ret_log_decay[l].astype(F32)),
            "ret_norm_gain": ret_norm_gain[l].astype(F32),
            "norm_mix_pre": norm_mix_pre[l].astype(F32),
            "norm_mix_post": norm_mix_post[l].astype(F32),
            "norm_mlp_pre": norm_mlp_pre[l].astype(F32),
            "norm_mlp_post": norm_mlp_post[l].astype(F32),
        })
    matrices = [{} for _ in range(depth)]
    matrices[0]["w_in"] = stacks["w_in"][0].astype(BF16)
    bias_diags = _bias_diags(rel_bias_table)
    y_prompt = _trunk(x_prompt, bias_diags, layers, stacks, matrices, cast_ahead=True)
    y_sample = _trunk(x_sample, bias_diags, layers, stacks, matrices, cast_ahead=False)
    return (y_prompt, y_sample)
```
